```python
import jax, jax.numpy as jnp
from jax import lax
import numpy as np

D_MODEL = 4096
BATCH = 4
SEQ = 2048
DEPTH = 1
DEC_BATCH = 32
DEC_SEQ = 4
PAST_LEN = 8192
PAGE_SIZE = 128

D_M = D_MODEL // 2
DH_M = 256
NH_M = D_M // DH_M
CONV_W = 4
QKV_BLOCK = 4
MLSTM_CHUNK = 64
FGATE_BIAS_LO = 3.0
FGATE_BIAS_HI = 6.0
D_N = D_MODEL // 2
DH_N = 128
H_N = D_N // DH_N
G_N = 4
J_N = H_N // G_N
KVW = G_N * DH_N
N_KV_SETS = 4
CMP_STRIDE = 16
CMP_LEN = 2 * CMP_STRIDE
CMP_HID = 2 * DH_N
SEL_LEN = 64
N_SEL = 16
WINDOW = 512
WIN_QBLK = 128
SEL_QBLK = 16
SCALE_N = DH_N ** -0.5
PEER_HEADS = 8
N_KEYS = 128
N_EXPERTS = N_KEYS * N_KEYS
PEER_TOPK = 16
PEER_QDIM = 256
PEER_QHALF = PEER_QDIM // 2
PEER_TBLK = 64
IN_SIZES = (D_M, D_M, D_M, NH_M, NH_M, D_N, N_KV_SETS * KVW, 2 * KVW, 3 * H_N, D_MODEL, D_MODEL)
IN_COLS = sum(IN_SIZES)
EPS = 1e-6
NEG = -1e30
BIG = 1e9

kernel_name = 'hybrid_mlstm_nsa_peer_step'


def rmsnorm(x, g):
    xf = x.astype(jnp.float32)
    xf = xf * lax.rsqrt(jnp.mean(xf * xf, axis=-1, keepdims=True) + EPS)
    return (xf * g.astype(jnp.float32)).astype(x.dtype)


def blockdiag(x, w):
    xb = x.reshape(x.shape[:-1] + (w.shape[0], QKV_BLOCK))
    return jnp.einsum('...ni,nio->...no', xb, w).reshape(x.shape)


def mlstm_chunk_step(carry, xs):
    C0, n0, m0 = carry
    q, k, v, a, lf = xs
    L = q.shape[2]
    b = jnp.cumsum(lf, axis=-1)
    causal = jnp.tril(jnp.ones((L, L), dtype=bool))
    dmat = jnp.where(causal, b[..., :, None] - b[..., None, :] + a[..., None, :], -jnp.inf)
    inter = b + m0[..., None]
    m = jnp.maximum(inter, jnp.max(dmat, axis=-1))
    w_intra = jnp.exp(dmat - m[..., None])
    w_inter = jnp.exp(inter - m)
    qk = jnp.einsum('bhtd,bhsd->bhts', q, k) * w_intra
    num = w_inter[..., None] * jnp.einsum('bhtd,bhde->bhte', q, C0) + jnp.einsum('bhts,bhse->bhte', qk, v)
    den = w_inter * jnp.einsum('bhtd,bhd->bht', q, n0) + jnp.sum(qk, axis=-1)
    h = num / jnp.maximum(jnp.abs(den), jnp.exp(-m))[..., None]
    m_end = m[..., -1]
    w_end = w_intra[..., -1, :]
    decay = jnp.exp(inter[..., -1] - m_end)
    C1 = decay[..., None, None] * C0 + jnp.einsum('bhsd,bhse->bhde', k * w_end[..., None], v)
    n1 = decay[..., None] * n0 + jnp.einsum('bhs,bhsd->bhd', w_end, k)
    return (C1, n1, m_end), h


def mlstm_sequence(q, k, v, a, lf, C0, n0, m0):
    B, NH, L, _ = q.shape
    ch = MLSTM_CHUNK if L % MLSTM_CHUNK == 0 else L
    nc = L // ch

    def split(t):
        return jnp.moveaxis(t.reshape(t.shape[:2] + (nc, ch) + t.shape[3:]), 2, 0)

    (C1, n1, m1), h = lax.scan(mlstm_chunk_step, (C0, n0, m0), (split(q), split(k), split(v), split(a), split(lf)))
    h = jnp.moveaxis(h, 0, 2).reshape(B, NH, L, DH_M)
    return h, C1, n1, m1


def compress_blocks(rows, pe, w1, w2):
    B, T = rows.shape[:2]
    tc = -(-T // CMP_STRIDE)
    rows = jnp.pad(rows, ((0, 0), (0, tc * CMP_STRIDE - T), (0, 0), (0, 0)))
    chunks = rows.reshape(B, tc, CMP_STRIDE, G_N, DH_N)
    first = jnp.einsum('bclgd,ldh->bcgh', chunks[:, :-1] + pe[:CMP_STRIDE, None, :], w1[:CMP_STRIDE])
    second = jnp.einsum('bclgd,ldh->bcgh', chunks[:, 1:] + pe[CMP_STRIDE:, None, :], w1[CMP_STRIDE:])
    return jnp.einsum('bcgh,hd->bcgd', jax.nn.gelu(first + second), w2)


def cmp_sel_attend(qg, rows, q_pos, cmp_pe, cmp_w1, cmp_w2):
    B, L = qg.shape[:2]
    T = rows[0].shape[1]
    k_cmp = compress_blocks(rows[0], cmp_pe[0], cmp_w1[0], cmp_w2[0])
    v_cmp = compress_blocks(rows[1], cmp_pe[1], cmp_w1[1], cmp_w2[1])
    n_c = k_cmp.shape[1]
    cstart = jnp.arange(n_c) * CMP_STRIDE
    cmask = (cstart[None, :] + CMP_LEN <= q_pos[:, None] + 1)[None, :, None, None, :]
    s = jnp.einsum('bqgjd,bcgd->bqgjc', qg, k_cmp).astype(jnp.float32) * SCALE_N
    p_cmp = jax.nn.softmax(jnp.where(cmask, s, NEG), axis=-1) * cmask
    o_cmp = jnp.einsum('bqgjc,bcgd->bqgjd', p_cmp.astype(v_cmp.dtype), v_cmp)
    n_s = -(-T // SEL_LEN)
    sstart = jnp.arange(n_s) * SEL_LEN
    cover = ((cstart[:, None] < sstart[None, :] + SEL_LEN) & (cstart[:, None] + CMP_LEN > sstart[None, :])).astype(jnp.float32)
    imp = jnp.einsum('bqgjc,cs->bqgs', p_cmp, cover)
    blk = jnp.arange(n_s)[None, :]
    cur = (q_pos // SEL_LEN)[:, None]
    valid = blk <= cur
    forced = (blk == 0) | (blk >= cur - 1)
    score = jnp.where(valid[None, :, None, :], imp + jnp.where(forced, BIG, 0.0)[None, :, None, :], -BIG)
    n_sel = min(N_SEL, n_s)
    _, idx = lax.top_k(score, n_sel)
    pad_s = n_s * SEL_LEN - T

    def to_blocks(r):
        r = jnp.pad(r, ((0, 0), (0, pad_s), (0, 0), (0, 0)))
        return r.reshape(B, n_s, SEL_LEN, G_N, DH_N).transpose(0, 3, 1, 2, 4)

    k_sel = to_blocks(rows[2])
    v_sel = to_blocks(rows[3])
    bi = jnp.arange(B)[:, None, None, None]
    gi = jnp.arange(G_N)[None, None, :, None]

    def sel_block(args):
        qq, ii, pp = args
        kb = k_sel[bi, gi, ii]
        vb = v_sel[bi, gi, ii]
        ss = jnp.einsum('bqgjd,bqgkrd->bqgjkr', qq, kb).astype(jnp.float32) * SCALE_N
        kpos = ii[..., None] * SEL_LEN + jnp.arange(SEL_LEN)
        keep = (kpos <= pp[None, :, None, None, None])[:, :, :, None]
        ss = jnp.where(keep, ss, NEG)
        pr = jax.nn.softmax(ss.reshape(ss.shape[:4] + (-1,)), axis=-1).reshape(ss.shape)
        return jnp.einsum('bqgjkr,bqgkrd->bqgjd', pr.astype(vb.dtype), vb)

    qb = SEL_QBLK if L % SEL_QBLK == 0 else L
    nqb = L // qb
    o_sel = lax.map(sel_block, (
        jnp.moveaxis(qg.reshape((B, nqb, qb) + qg.shape[2:]), 1, 0),
        jnp.moveaxis(idx.reshape((B, nqb, qb) + idx.shape[2:]), 1, 0),
        q_pos.reshape(nqb, qb)))
    o_sel = jnp.moveaxis(o_sel, 0, 1).reshape(qg.shape)
    return o_cmp, o_sel


def window_attend(qw, kw, vw, q_pos, k_pos):
    s = jnp.einsum('bnqgjd,bnkgd->bnqgjk', qw, kw).astype(jnp.float32) * SCALE_N
    kp = k_pos[:, None, :]
    qp = q_pos[:, :, None]
    keep = (kp <= qp) & (kp > qp - WINDOW) & (kp >= 0)
    p = jax.nn.softmax(jnp.where(keep[None, :, :, None, None, :], s, NEG), axis=-1)
    return jnp.einsum('bnqgjk,bnkgd->bnqgjd', p.astype(vw.dtype), vw)


def token_mixers(xn, w_in, w_conv, b_conv, w_bd, b_gates, w_hnorm, w_skip, cmp_pe, cmp_w1, cmp_w2,
                 w_br_m, w_br_n, w_out, conv_prev, C0, n0, m0, kv_past, win_past):
    B, L, _ = xn.shape
    f32 = jnp.float32
    z = xn @ w_in
    xm, vm, om, ig, fg, qn, kvn, wn, gn, g_m, g_n = jnp.split(z, np.cumsum(IN_SIZES)[:-1].tolist(), axis=-1)
    xm_ext = jnp.concatenate([conv_prev.astype(xm.dtype), xm], axis=1)
    conv = b_conv + sum(xm_ext[:, w:w + L] * w_conv[w] for w in range(CONV_W))
    c = jax.nn.silu(conv)
    new_conv = xm_ext[:, L:]

    def heads(t):
        return t.reshape(B, L, NH_M, DH_M).transpose(0, 2, 1, 3).astype(f32)

    q_m = heads(blockdiag(c, w_bd[0]))
    k_m = heads(blockdiag(c, w_bd[1])) * (DH_M ** -0.5)
    v_m = heads(blockdiag(vm, w_bd[2]))
    a_m = (ig + b_gates[0]).astype(f32).transpose(0, 2, 1)
    lf_m = jax.nn.log_sigmoid((fg + b_gates[1]).astype(f32)).transpose(0, 2, 1)
    h_m, C1, n1, m1 = mlstm_sequence(q_m, k_m, v_m, a_m, lf_m, C0.astype(f32), n0.astype(f32), m0.astype(f32))
    h_m = h_m.transpose(0, 2, 1, 3) * jax.nn.sigmoid(om.astype(f32)).reshape(B, L, NH_M, DH_M)
    mu = jnp.mean(h_m, axis=-1, keepdims=True)
    var = jnp.mean(jnp.square(h_m - mu), axis=-1, keepdims=True)
    h_m = ((h_m - mu) * lax.rsqrt(var + EPS)).reshape(B, L, D_M)
    y_m = (h_m * w_hnorm.astype(f32) + w_skip.astype(f32) * c.astype(f32)).astype(xn.dtype)
    qg = qn.reshape(B, L, G_N, J_N, DH_N)
    kv_new = kvn.reshape(B, L, N_KV_SETS, G_N, DH_N)
    win_new = wn.reshape(B, L, 2, G_N, DH_N)
    if kv_past is None:
        rows = [kv_new[:, :, s] for s in range(N_KV_SETS)]
    else:
        rows = [jnp.concatenate([kv_past[s], kv_new[:, :, s]], axis=1) for s in range(N_KV_SETS)]
    T = rows[0].shape[1]
    q_pos = (T - L) + jnp.arange(L)
    o_cmp, o_sel = cmp_sel_attend(qg, rows, q_pos, cmp_pe, cmp_w1, cmp_w2)
    if win_past is None:
        nb = L // WIN_QBLK
        nwb = WINDOW // WIN_QBLK
        wpad = jnp.pad(win_new, ((0, 0), (WINDOW, 0), (0, 0), (0, 0), (0, 0)))
        wblk = wpad.reshape(B, nb + nwb, WIN_QBLK, 2, G_N, DH_N)
        bidx = jnp.arange(nb)[:, None] + jnp.arange(nwb + 1)[None, :]
        kw = wblk[:, bidx].reshape(B, nb, (nwb + 1) * WIN_QBLK, 2, G_N, DH_N)
        k_pos = ((bidx * WIN_QBLK - WINDOW)[..., None] + jnp.arange(WIN_QBLK)).reshape(nb, -1)
        qw = qg.reshape(B, nb, WIN_QBLK, G_N, J_N, DH_N)
        q_pos_w = q_pos.reshape(nb, WIN_QBLK)
        new_win = win_new[:, -min(WINDOW, L):]
    else:
        wbuf = win_past.shape[1]
        wrows = jnp.concatenate([win_past, win_new], axis=1)
        kw = wrows[:, None]
        k_pos = ((T - L - wbuf) + jnp.arange(wbuf + L))[None]
        qw = qg[:, None]
        q_pos_w = q_pos[None]
        new_win = wrows[:, -wbuf:]
    o_win = window_attend(qw, kw[..., 0, :, :], kw[..., 1, :, :], q_pos_w, k_pos).reshape(qg.shape)
    gts = jax.nn.sigmoid(gn.astype(f32)).reshape(B, L, G_N, J_N, 3)
    y_n = (gts[..., 0:1] * o_cmp + gts[..., 1:2] * o_sel + gts[..., 2:3] * o_win).astype(xn.dtype).reshape(B, L, D_N)
    merged = jax.nn.sigmoid(g_m) * (y_m @ w_br_m) + jax.nn.sigmoid(g_n) * (y_n @ w_br_n)
    return merged @ w_out, kv_new, new_win, new_conv, C1, n1, m1


def peer_ffn(xn, w_pq, pkeys, w_u, w_v):
    B, L, D = xn.shape
    n = B * L
    npad = -(-n // PEER_TBLK) * PEER_TBLK
    x = jnp.pad(xn.reshape(n, D), ((0, npad - n), (0, 0)))
    q = (x @ w_pq).reshape(npad, PEER_HEADS, 2, PEER_QHALF)
    s = jnp.einsum('nhcd,hckd->nhck', q, pkeys).astype(jnp.float32)
    s1, i1 = lax.top_k(s[:, :, 0], PEER_TOPK)
    s2, i2 = lax.top_k(s[:, :, 1], PEER_TOPK)
    cand = (s1[..., :, None] + s2[..., None, :]).reshape(npad, PEER_HEADS, PEER_TOPK * PEER_TOPK)
    cs, ci = lax.top_k(cand, PEER_TOPK)
    eid = (jnp.take_along_axis(i1, ci // PEER_TOPK, axis=-1) * N_KEYS
           + jnp.take_along_axis(i2, ci % PEER_TOPK, axis=-1))
    gate = jax.nn.softmax(cs, axis=-1)
    nb = npad // PEER_TBLK
    E = PEER_HEADS * PEER_TOPK

    def block(args):
        xb, eb, gb = args
        u = w_u[eb]
        act = jax.nn.gelu(jnp.einsum('td,ted->te', xb, u).astype(jnp.float32)) * gb
        return jnp.einsum('te,ted->td', act.astype(xb.dtype), w_v[eb])

    out = lax.map(block, (x.reshape(nb, PEER_TBLK, D), eid.reshape(nb, PEER_TBLK, E), gate.reshape(nb, PEER_TBLK, E)))
    return out.reshape(npad, D)[:n].reshape(B, L, D)


def setup_inputs(seed: int = 0) -> dict:
    key = jax.random.key(seed)
    ks = jax.random.split(key, 32)
    f32 = jnp.float32
    n_pages = PAST_LEN // PAGE_SIZE
    n_pool = (5 * DEC_BATCH * n_pages) // 4
    w_buf = min(WINDOW, PAST_LEN)

    def nrm(k, shape, scale):
        return jax.random.normal(k, shape, f32) * scale

    page_table = jax.random.permutation(ks[0], n_pool)[:DEC_BATCH * n_pages].reshape(DEC_BATCH, n_pages).astype(jnp.int32)
    f_bias = jnp.linspace(FGATE_BIAS_LO, FGATE_BIAS_HI, NH_M, dtype=f32)
    b_gates = jnp.stack([nrm(ks[1], (DEPTH, NH_M), 0.1), f_bias + nrm(ks[2], (DEPTH, NH_M), 0.1)], axis=1)
    return {
        'x_prompt': nrm(ks[3], (BATCH, SEQ, D_MODEL), 1.0),
        'x_sample': nrm(ks[4], (DEC_BATCH, DEC_SEQ, D_MODEL), 1.0),
        'cache_kv': nrm(ks[5], (DEPTH, n_pool, PAGE_SIZE, N_KV_SETS, G_N, DH_N), 1.0),
        'state_win_kv': nrm(ks[6], (DEPTH, DEC_BATCH, w_buf, 2, G_N, DH_N), 1.0),
        'state_conv': nrm(ks[7], (DEPTH, DEC_BATCH, CONV_W - 1, D_M), 1.0),
        'state_C': nrm(ks[8], (DEPTH, DEC_BATCH, NH_M, DH_M, DH_M), 0.1),
        'state_n': nrm(ks[9], (DEPTH, DEC_BATCH, NH_M, DH_M), 0.5),
        'state_m': nrm(ks[10], (DEPTH, DEC_BATCH, NH_M), 1.0),
        'page_table': page_table,
        'norm_mix': 1.0 + nrm(ks[11], (DEPTH, D_MODEL), 0.02),
        'norm_ffn': 1.0 + nrm(ks[12], (DEPTH, D_MODEL), 0.02),
        'norm_final': 1.0 + nrm(ks[13], (D_MODEL,), 0.02),
        'w_in': nrm(ks[14], (DEPTH, D_MODEL, IN_COLS), D_MODEL ** -0.5),
        'w_conv': nrm(ks[15], (DEPTH, CONV_W, D_M), CONV_W ** -0.5),
        'b_conv': nrm(ks[16], (DEPTH, D_M), 0.02),
        'w_bd': nrm(ks[17], (DEPTH, 3, D_M // QKV_BLOCK, QKV_BLOCK, QKV_BLOCK), QKV_BLOCK ** -0.5),
        'b_gates': b_gates,
        'w_hnorm': 1.0 + nrm(ks[18], (DEPTH, D_M), 0.02),
        'w_skip': 1.0 + nrm(ks[19], (DEPTH, D_M), 0.02),
        'cmp_pe': nrm(ks[20], (DEPTH, 2, CMP_LEN, DH_N), 0.1),
        'cmp_w1': nrm(ks[21], (DEPTH, 2, CMP_LEN, DH_N, CMP_HID), (CMP_LEN * DH_N) ** -0.5),
        'cmp_w2': nrm(ks[22], (DEPTH, 2, CMP_HID, DH_N), CMP_HID ** -0.5),
        'w_br_m': nrm(ks[23], (DEPTH, D_M, D_MODEL), D_M ** -0.5),
        'w_br_n': nrm(ks[24], (DEPTH, D_N, D_MODEL), D_N ** -0.5),
        'w_out': nrm(ks[25], (DEPTH, D_MODEL, D_MODEL), D_MODEL ** -0.5),
        'w_pq': nrm(ks[26], (DEPTH, D_MODEL, PEER_HEADS * PEER_QDIM), D_MODEL ** -0.5),
        'peer_keys': nrm(ks[27], (DEPTH, PEER_HEADS, 2, N_KEYS, PEER_QHALF), PEER_QHALF ** -0.5),
        'peer_u': nrm(ks[28], (DEPTH, N_EXPERTS, D_MODEL), D_MODEL ** -0.5),
        'peer_v': nrm(ks[29], (DEPTH, N_EXPERTS, D_MODEL), PEER_HEADS ** -0.5),
    }


def reference(x_prompt, x_sample, cache_kv, state_win_kv, state_conv, state_C, state_n, state_m, page_table,
              norm_mix, norm_ffn, norm_final, w_in, w_conv, b_conv, w_bd, b_gates, w_hnorm, w_skip,
              cmp_pe, cmp_w1, cmp_w2, w_br_m, w_br_n, w_out, w_pq, peer_keys, peer_u, peer_v):
    f32 = jnp.float32
    xp, xs = x_prompt, x_sample
    Bp = xp.shape[0]
    Bs = xs.shape[0]
    past = page_table.shape[1] * cache_kv.shape[2]
    kvp_l, kvs_l, winp_l, wins_l, convp_l, convs_l = [], [], [], [], [], []
    Cp_l, Cs_l, np_l, ns_l, mp_l, ms_l = [], [], [], [], [], []
    for l in range(DEPTH):
        mix_w = (w_in[l], w_conv[l], b_conv[l], w_bd[l], b_gates[l], w_hnorm[l], w_skip[l],
                 cmp_pe[l], cmp_w1[l], cmp_w2[l], w_br_m[l], w_br_n[l], w_out[l])
        yp, kvp, winp, convp, Cp, n_p, m_p = token_mixers(
            rmsnorm(xp, norm_mix[l]), *mix_w,
            jnp.zeros((Bp, CONV_W - 1, D_M), xp.dtype), jnp.zeros((Bp, NH_M, DH_M, DH_M), f32),
            jnp.zeros((Bp, NH_M, DH_M), f32), jnp.zeros((Bp, NH_M), f32), None, None)
        kv_past = [cache_kv[l][page_table, :, s].reshape(Bs, past, G_N, DH_N) for s in range(N_KV_SETS)]
        ys, kvs, wins, convs, Cs, n_s, m_s = token_mixers(
            rmsnorm(xs, norm_mix[l]), *mix_w,
            state_conv[l], state_C[l], state_n[l], state_m[l], kv_past, state_win_kv[l])
        xp = xp + yp
        xs = xs + ys
        peer_w = (w_pq[l], peer_keys[l], peer_u[l], peer_v[l])
        xp = xp + peer_ffn(rmsnorm(xp, norm_ffn[l]), *peer_w)
        xs = xs + peer_ffn(rmsnorm(xs, norm_ffn[l]), *peer_w)
        kvp_l.append(kvp); kvs_l.append(kvs); winp_l.append(winp); wins_l.append(wins)
        convp_l.append(convp); convs_l.append(convs); Cp_l.append(Cp); Cs_l.append(Cs)
        np_l.append(n_p); ns_l.append(n_s); mp_l.append(m_p); ms_l.append(m_s)
    y_prompt = rmsnorm(xp, norm_final)
    y_sample = rmsnorm(xs, norm_final)
    new_kv_prompt = jnp.stack(kvp_l)
    new_kv_sample = jnp.stack(kvs_l)
    win_prompt = jnp.stack(winp_l)
    win_sample = jnp.stack(wins_l)
    conv_prompt = jnp.stack(convp_l)
    conv_sample = jnp.stack(convs_l)
    C_prompt = jnp.stack(Cp_l)
    C_sample = jnp.stack(Cs_l)
    n_prompt = jnp.stack(np_l)
    n_sample = jnp.stack(ns_l)
    m_prompt = jnp.stack(mp_l)
    m_sample = jnp.stack(ms_l)
    return (y_prompt, y_sample, new_kv_prompt, new_kv_sample, win_prompt, win_sample, conv_prompt, conv_sample, C_prompt, C_sample, n_prompt, n_sample, m_prompt, m_sample)
```

```python
import functools

import jax
import jax.numpy as jnp
import numpy as np
from jax import lax
from jax.experimental import pallas as pl
from jax.experimental.pallas import tpu as pltpu

D_MODEL = 4096
DEPTH = 1
D_M = D_MODEL // 2
DH_M = 256
NH_M = D_M // DH_M
CONV_W = 4
QKV_BLOCK = 4
MLSTM_CHUNK = 64
D_N = D_MODEL // 2
DH_N = 128
H_N = D_N // DH_N
G_N = 4
J_N = H_N // G_N
KVW = G_N * DH_N
N_KV_SETS = 4
CMP_STRIDE = 16
CMP_LEN = 2 * CMP_STRIDE
CMP_HID = 2 * DH_N
SEL_LEN = 64
N_SEL = 16
WINDOW = 512
WIN_QBLK = 128
SEL_QBLK = 16
SCALE_N = DH_N ** -0.5
PEER_HEADS = 8
N_KEYS = 128
PEER_TOPK = 16
PEER_QDIM = 256
PEER_QHALF = PEER_QDIM // 2
PEER_TBLK = 64
IN_SIZES = (D_M, D_M, D_M, NH_M, NH_M, D_N, N_KV_SETS * KVW, 2 * KVW, 3 * H_N, D_MODEL, D_MODEL)
EPS = 1e-6
NEG = -1e30
BIG = 1e9

VMEM_LIMIT_BYTES = 48 * 1024 * 1024


def _mm_kernel(a_ref, b_ref, o_ref, acc_ref):
    k = pl.program_id(2)

    @pl.when(k == 0)
    def _():
        acc_ref[...] = jnp.zeros_like(acc_ref)

    acc_ref[...] += jnp.dot(a_ref[...].astype(jnp.bfloat16), b_ref[...].astype(jnp.bfloat16),
                            preferred_element_type=jnp.float32)

    @pl.when(k == pl.num_programs(2) - 1)
    def _():
        o_ref[...] = acc_ref[...]


def _pick(n, cands):
    for c in cands:
        if n % c == 0:
            return c
    return n


def pmatmul(a, b):
    m, kd = a.shape
    _, n = b.shape
    tm = _pick(m, (512, 256, 128))
    tn = _pick(n, (512, 256, 128))
    tk = _pick(kd, (1024, 512, 256, 128))
    return pl.pallas_call(
        _mm_kernel,
        grid=(m // tm, n // tn, kd // tk),
        in_specs=[pl.BlockSpec((tm, tk), lambda i, j, k: (i, k)),
                  pl.BlockSpec((tk, tn), lambda i, j, k: (k, j))],
        out_specs=pl.BlockSpec((tm, tn), lambda i, j, k: (i, j)),
        out_shape=jax.ShapeDtypeStruct((m, n), jnp.float32),
        scratch_shapes=[pltpu.VMEM((tm, tn), jnp.float32)],
        compiler_params=pltpu.CompilerParams(
            dimension_semantics=("parallel", "parallel", "arbitrary"),
            vmem_limit_bytes=VMEM_LIMIT_BYTES),
    )(a, b)


def pmatmul_nd(x, w):
    lead = x.shape[:-1]
    return pmatmul(x.reshape(-1, x.shape[-1]), w).reshape(lead + (w.shape[-1],))


def rmsnorm(x, g):
    xf = x.astype(jnp.float32)
    xf = xf * lax.rsqrt(jnp.mean(xf * xf, axis=-1, keepdims=True) + EPS)
    return (xf * g.astype(jnp.float32)).astype(x.dtype)


def blockdiag(x, w):
    xb = x.reshape(x.shape[:-1] + (w.shape[0], QKV_BLOCK))
    return jnp.einsum('...ni,nio->...no', xb, w).reshape(x.shape)


def mlstm_chunk_step(carry, xs):
    C0, n0, m0 = carry
    q, k, v, a, lf = xs
    L = q.shape[2]
    b = jnp.cumsum(lf, axis=-1)
    causal = jnp.tril(jnp.ones((L, L), dtype=bool))
    dmat = jnp.where(causal, b[..., :, None] - b[..., None, :] + a[..., None, :], -jnp.inf)
    inter = b + m0[..., None]
    m = jnp.maximum(inter, jnp.max(dmat, axis=-1))
    w_intra = jnp.exp(dmat - m[..., None])
    w_inter = jnp.exp(inter - m)
    qk = jnp.einsum('bhtd,bhsd->bhts', q, k) * w_intra
    num = w_inter[..., None] * jnp.einsum('bhtd,bhde->bhte', q, C0) + jnp.einsum('bhts,bhse->bhte', qk, v)
    den = w_inter * jnp.einsum('bhtd,bhd->bht', q, n0) + jnp.sum(qk, axis=-1)
    h = num / jnp.maximum(jnp.abs(den), jnp.exp(-m))[..., None]
    m_end = m[..., -1]
    w_end = w_intra[..., -1, :]
    decay = jnp.exp(inter[..., -1] - m_end)
    C1 = decay[..., None, None] * C0 + jnp.einsum('bhsd,bhse->bhde', k * w_end[..., None], v)
    n1 = decay[..., None] * n0 + jnp.einsum('bhs,bhsd->bhd', w_end, k)
    return (C1, n1, m_end), h


def mlstm_sequence(q, k, v, a, lf, C0, n0, m0):
    B, NH, L, _ = q.shape
    ch = MLSTM_CHUNK if L % MLSTM_CHUNK == 0 else L
    nc = L // ch

    def split(t):
        return jnp.moveaxis(t.reshape(t.shape[:2] + (nc, ch) + t.shape[3:]), 2, 0)

    (C1, n1, m1), h = lax.scan(mlstm_chunk_step, (C0, n0, m0), (split(q), split(k), split(v), split(a), split(lf)))
    h = jnp.moveaxis(h, 0, 2).reshape(B, NH, L, DH_M)
    return h, C1, n1, m1


def compress_blocks(rows, pe, w1, w2):
    B, T = rows.shape[:2]
    tc = -(-T // CMP_STRIDE)
    rows = jnp.pad(rows, ((0, 0), (0, tc * CMP_STRIDE - T), (0, 0), (0, 0)))
    chunks = rows.reshape(B, tc, CMP_STRIDE, G_N, DH_N)
    first = jnp.einsum('bclgd,ldh->bcgh', chunks[:, :-1] + pe[:CMP_STRIDE, None, :], w1[:CMP_STRIDE])
    second = jnp.einsum('bclgd,ldh->bcgh', chunks[:, 1:] + pe[CMP_STRIDE:, None, :], w1[CMP_STRIDE:])
    return jnp.einsum('bcgh,hd->bcgd', jax.nn.gelu(first + second), w2)


def cmp_sel_attend(qg, rows, q_pos, cmp_pe, cmp_w1, cmp_w2):
    B, L = qg.shape[:2]
    T = rows[0].shape[1]
    k_cmp = compress_blocks(rows[0], cmp_pe[0], cmp_w1[0], cmp_w2[0])
    v_cmp = compress_blocks(rows[1], cmp_pe[1], cmp_w1[1], cmp_w2[1])
    n_c = k_cmp.shape[1]
    cstart = jnp.arange(n_c) * CMP_STRIDE
    cmask = (cstart[None, :] + CMP_LEN <= q_pos[:, None] + 1)[None, :, None, None, :]
    s = jnp.einsum('bqgjd,bcgd->bqgjc', qg, k_cmp).astype(jnp.float32) * SCALE_N
    p_cmp = jax.nn.softmax(jnp.where(cmask, s, NEG), axis=-1) * cmask
    o_cmp = jnp.einsum('bqgjc,bcgd->bqgjd', p_cmp.astype(v_cmp.dtype), v_cmp)
    n_s = -(-T // SEL_LEN)
    sstart = jnp.arange(n_s) * SEL_LEN
    cover = ((cstart[:, None] < sstart[None, :] + SEL_LEN) & (cstart[:, None] + CMP_LEN > sstart[None, :])).astype(jnp.float32)
    imp = jnp.einsum('bqgjc,cs->bqgs', p_cmp, cover)
    blk = jnp.arange(n_s)[None, :]
    cur = (q_pos // SEL_LEN)[:, None]
    valid = blk <= cur
    forced = (blk == 0) | (blk >= cur - 1)
    score = jnp.where(valid[None, :, None, :], imp + jnp.where(forced, BIG, 0.0)[None, :, None, :], -BIG)
    n_sel = min(N_SEL, n_s)
    _, idx = lax.top_k(score, n_sel)
    pad_s = n_s * SEL_LEN - T

    def to_blocks(r):
        r = jnp.pad(r, ((0, 0), (0, pad_s), (0, 0), (0, 0)))
        return r.reshape(B, n_s, SEL_LEN, G_N, DH_N).transpose(0, 3, 1, 2, 4)

    k_sel = to_blocks(rows[2])
    v_sel = to_blocks(rows[3])
    bi = jnp.arange(B)[:, None, None, None]
    gi = jnp.arange(G_N)[None, None, :, None]

    def sel_block(args):
        qq, ii, pp = args
        kb = k_sel[bi, gi, ii]
        vb = v_sel[bi, gi, ii]
        ss = jnp.einsum('bqgjd,bqgkrd->bqgjkr', qq, kb).astype(jnp.float32) * SCALE_N
        kpos = ii[..., None] * SEL_LEN + jnp.arange(SEL_LEN)
        keep = (kpos <= pp[None, :, None, None, None])[:, :, :, None]
        ss = jnp.where(keep, ss, NEG)
        pr = jax.nn.softmax(ss.reshape(ss.shape[:4] + (-1,)), axis=-1).reshape(ss.shape)
        return jnp.einsum('bqgjkr,bqgkrd->bqgjd', pr.astype(vb.dtype), vb)

    qb = SEL_QBLK if L % SEL_QBLK == 0 else L
    nqb = L // qb
    o_sel = lax.map(sel_block, (
        jnp.moveaxis(qg.reshape((B, nqb, qb) + qg.shape[2:]), 1, 0),
        jnp.moveaxis(idx.reshape((B, nqb, qb) + idx.shape[2:]), 1, 0),
        q_pos.reshape(nqb, qb)))
    o_sel = jnp.moveaxis(o_sel, 0, 1).reshape(qg.shape)
    return o_cmp, o_sel


def window_attend(qw, kw, vw, q_pos, k_pos):
    s = jnp.einsum('bnqgjd,bnkgd->bnqgjk', qw, kw).astype(jnp.float32) * SCALE_N
    kp = k_pos[:, None, :]
    qp = q_pos[:, :, None]
    keep = (kp <= qp) & (kp > qp - WINDOW) & (kp >= 0)
    p = jax.nn.softmax(jnp.where(keep[None, :, :, None, None, :], s, NEG), axis=-1)
    return jnp.einsum('bnqgjk,bnkgd->bnqgjd', p.astype(vw.dtype), vw)


def token_mixers(xn, w_in, w_conv, b_conv, w_bd, b_gates, w_hnorm, w_skip, cmp_pe, cmp_w1, cmp_w2,
                 w_br_m, w_br_n, w_out, conv_prev, C0, n0, m0, kv_past, win_past):
    B, L, _ = xn.shape
    f32 = jnp.float32
    offs = np.concatenate([[0], np.cumsum(IN_SIZES)]).tolist()
    xn2 = xn.reshape(B * L, D_MODEL)
    parts = []
    for i, sz in enumerate(IN_SIZES):
        w_seg = w_in[:, offs[i]:offs[i + 1]]
        pad = (-sz) % 128
        if pad:
            w_seg = jnp.pad(w_seg, ((0, 0), (0, pad)))
        parts.append(pmatmul(xn2, w_seg)[:, :sz].reshape(B, L, sz))
    xm, vm, om, ig, fg, qn, kvn, wn, gn, g_m, g_n = parts
    xm_ext = jnp.concatenate([conv_prev.astype(xm.dtype), xm], axis=1)
    conv = b_conv + sum(xm_ext[:, w:w + L] * w_conv[w] for w in range(CONV_W))
    c = jax.nn.silu(conv)
    new_conv = xm_ext[:, L:]

    def heads(t):
        return t.reshape(B, L, NH_M, DH_M).transpose(0, 2, 1, 3).astype(f32)

    q_m = heads(blockdiag(c, w_bd[0]))
    k_m = heads(blockdiag(c, w_bd[1])) * (DH_M ** -0.5)
    v_m = heads(blockdiag(vm, w_bd[2]))
    a_m = (ig + b_gates[0]).astype(f32).transpose(0, 2, 1)
    lf_m = jax.nn.log_sigmoid((fg + b_gates[1]).astype(f32)).transpose(0, 2, 1)
    h_m, C1, n1, m1 = mlstm_sequence(q_m, k_m, v_m, a_m, lf_m, C0.astype(f32), n0.astype(f32), m0.astype(f32))
    h_m = h_m.transpose(0, 2, 1, 3) * jax.nn.sigmoid(om.astype(f32)).reshape(B, L, NH_M, DH_M)
    mu = jnp.mean(h_m, axis=-1, keepdims=True)
    var = jnp.mean(jnp.square(h_m - mu), axis=-1, keepdims=True)
    h_m = ((h_m - mu) * lax.rsqrt(var + EPS)).reshape(B, L, D_M)
    y_m = (h_m * w_hnorm.astype(f32) + w_skip.astype(f32) * c.astype(f32)).astype(xn.dtype)
    qg = qn.reshape(B, L, G_N, J_N, DH_N)
    kv_new = kvn.reshape(B, L, N_KV_SETS, G_N, DH_N)
    win_new = wn.reshape(B, L, 2, G_N, DH_N)
    if kv_past is None:
        rows = [kv_new[:, :, s] for s in range(N_KV_SETS)]
    else:
        rows = [jnp.concatenate([kv_past[s], kv_new[:, :, s]], axis=1) for s in range(N_KV_SETS)]
    T = rows[0].shape[1]
    q_pos = (T - L) + jnp.arange(L)
    o_cmp, o_sel = cmp_sel_attend(qg, rows, q_pos, cmp_pe, cmp_w1, cmp_w2)
    if win_past is None:
        nb = L // WIN_QBLK
        nwb = WINDOW // WIN_QBLK
        wpad = jnp.pad(win_new, ((0, 0), (WINDOW, 0), (0, 0), (0, 0), (0, 0)))
        wblk = wpad.reshape(B, nb + nwb, WIN_QBLK, 2, G_N, DH_N)
        bidx = jnp.arange(nb)[:, None] + jnp.arange(nwb + 1)[None, :]
        kw = wblk[:, bidx].reshape(B, nb, (nwb + 1) * WIN_QBLK, 2, G_N, DH_N)
        k_pos = ((bidx * WIN_QBLK - WINDOW)[..., None] + jnp.arange(WIN_QBLK)).reshape(nb, -1)
        qw = qg.reshape(B, nb, WIN_QBLK, G_N, J_N, DH_N)
        q_pos_w = q_pos.reshape(nb, WIN_QBLK)
        new_win = win_new[:, -min(WINDOW, L):]
    else:
        wbuf = win_past.shape[1]
        wrows = jnp.concatenate([win_past, win_new], axis=1)
        kw = wrows[:, None]
        k_pos = ((T - L - wbuf) + jnp.arange(wbuf + L))[None]
        qw = qg[:, None]
        q_pos_w = q_pos[None]
        new_win = wrows[:, -wbuf:]
    o_win = window_attend(qw, kw[..., 0, :, :], kw[..., 1, :, :], q_pos_w, k_pos).reshape(qg.shape)
    gts = jax.nn.sigmoid(gn.astype(f32)).reshape(B, L, G_N, J_N, 3)
    y_n = (gts[..., 0:1] * o_cmp + gts[..., 1:2] * o_sel + gts[..., 2:3] * o_win).astype(xn.dtype).reshape(B, L, D_N)
    merged = jax.nn.sigmoid(g_m) * pmatmul_nd(y_m, w_br_m) + jax.nn.sigmoid(g_n) * pmatmul_nd(y_n, w_br_n)
    return pmatmul_nd(merged, w_out), kv_new, new_win, new_conv, C1, n1, m1


def peer_ffn(xn, w_pq, pkeys, w_u, w_v):
    B, L, D = xn.shape
    n = B * L
    npad = -(-n // PEER_TBLK) * PEER_TBLK
    x = jnp.pad(xn.reshape(n, D), ((0, npad - n), (0, 0)))
    q = pmatmul(x, w_pq).reshape(npad, PEER_HEADS, 2, PEER_QHALF)
    s = jnp.einsum('nhcd,hckd->nhck', q, pkeys).astype(jnp.float32)
    s1, i1 = lax.top_k(s[:, :, 0], PEER_TOPK)
    s2, i2 = lax.top_k(s[:, :, 1], PEER_TOPK)
    cand = (s1[..., :, None] + s2[..., None, :]).reshape(npad, PEER_HEADS, PEER_TOPK * PEER_TOPK)
    cs, ci = lax.top_k(cand, PEER_TOPK)
    eid = (jnp.take_along_axis(i1, ci // PEER_TOPK, axis=-1) * N_KEYS
           + jnp.take_along_axis(i2, ci % PEER_TOPK, axis=-1))
    gate = jax.nn.softmax(cs, axis=-1)
    nb = npad // PEER_TBLK
    E = PEER_HEADS * PEER_TOPK

    def block(args):
        xb, eb, gb = args
        u = w_u[eb]
        act = jax.nn.gelu(jnp.einsum('td,ted->te', xb, u).astype(jnp.float32)) * gb
        return jnp.einsum('te,ted->td', act.astype(xb.dtype), w_v[eb])

    out = lax.map(block, (x.reshape(nb, PEER_TBLK, D), eid.reshape(nb, PEER_TBLK, E), gate.reshape(nb, PEER_TBLK, E)))
    return out.reshape(npad, D)[:n].reshape(B, L, D)


def kernel(x_prompt, x_sample, cache_kv, state_win_kv, state_conv, state_C, state_n, state_m, page_table,
           norm_mix, norm_ffn, norm_final, w_in, w_conv, b_conv, w_bd, b_gates, w_hnorm, w_skip,
           cmp_pe, cmp_w1, cmp_w2, w_br_m, w_br_n, w_out, w_pq, peer_keys, peer_u, peer_v):
    f32 = jnp.float32
    xp, xs = x_prompt, x_sample
    Bp = xp.shape[0]
    Bs = xs.shape[0]
    past = page_table.shape[1] * cache_kv.shape[2]
    l = 0
    mix_w = (w_in[l], w_conv[l], b_conv[l], w_bd[l], b_gates[l], w_hnorm[l], w_skip[l],
             cmp_pe[l], cmp_w1[l], cmp_w2[l], w_br_m[l], w_br_n[l], w_out[l])
    yp, kvp, winp, convp, Cp, n_p, m_p = token_mixers(
        rmsnorm(xp, norm_mix[l]), *mix_w,
        jnp.zeros((Bp, CONV_W - 1, D_M), xp.dtype), jnp.zeros((Bp, NH_M, DH_M, DH_M), f32),
        jnp.zeros((Bp, NH_M, DH_M), f32), jnp.zeros((Bp, NH_M), f32), None, None)
    kv_past = [cache_kv[l][page_table, :, s].reshape(Bs, past, G_N, DH_N) for s in range(N_KV_SETS)]
    ys, kvs, wins, convs, Cs, n_s, m_s = token_mixers(
        rmsnorm(xs, norm_mix[l]), *mix_w,
        state_conv[l], state_C[l], state_n[l], state_m[l], kv_past, state_win_kv[l])
    xp = xp + yp
    xs = xs + ys
    peer_w = (w_pq[l], peer_keys[l], peer_u[l], peer_v[l])
    xp = xp + peer_ffn(rmsnorm(xp, norm_ffn[l]), *peer_w)
    xs = xs + peer_ffn(rmsnorm(xs, norm_ffn[l]), *peer_w)
    y_prompt = rmsnorm(xp, norm_final)
    y_sample = rmsnorm(xs, norm_final)
    st = lambda t: t[None]
    return (y_prompt, y_sample, st(kvp), st(kvs), st(winp), st(wins), st(convp), st(convs),
            st(Cp), st(Cs), st(n_p), st(n_s), st(m_p), st(m_s))
```

```python
import functools

import jax
import jax.numpy as jnp
import numpy as np
from jax import lax
from jax.experimental import pallas as pl
from jax.experimental.pallas import tpu as pltpu

D_MODEL = 4096
DEPTH = 1
D_M = D_MODEL // 2
DH_M = 256
NH_M = D_M // DH_M
CONV_W = 4
QKV_BLOCK = 4
MLSTM_CHUNK = 64
D_N = D_MODEL // 2
DH_N = 128
H_N = D_N // DH_N
G_N = 4
J_N = H_N // G_N
KVW = G_N * DH_N
N_KV_SETS = 4
CMP_STRIDE = 16
CMP_LEN = 2 * CMP_STRIDE
CMP_HID = 2 * DH_N
SEL_LEN = 64
N_SEL = 16
WINDOW = 512
WIN_QBLK = 128
SEL_QBLK = 16
SCALE_N = DH_N ** -0.5
PEER_HEADS = 8
N_KEYS = 128
PEER_TOPK = 16
PEER_QDIM = 256
PEER_QHALF = PEER_QDIM // 2
PEER_TBLK = 64
IN_SIZES = (D_M, D_M, D_M, NH_M, NH_M, D_N, N_KV_SETS * KVW, 2 * KVW, 3 * H_N, D_MODEL, D_MODEL)
EPS = 1e-6
NEG = -1e30
BIG = 1e9

VMEM_LIMIT_BYTES = 48 * 1024 * 1024
PEER_VMEM_LIMIT_BYTES = 56 * 1024 * 1024
PEER_SUB = 4


def _mm_kernel(a_ref, b_ref, o_ref, acc_ref):
    k = pl.program_id(2)

    @pl.when(k == 0)
    def _():
        acc_ref[...] = jnp.zeros_like(acc_ref)

    acc_ref[...] += jnp.dot(a_ref[...].astype(jnp.bfloat16), b_ref[...].astype(jnp.bfloat16),
                            preferred_element_type=jnp.float32)

    @pl.when(k == pl.num_programs(2) - 1)
    def _():
        o_ref[...] = acc_ref[...]


def _pick(n, cands):
    for c in cands:
        if n % c == 0:
            return c
    return n


def pmatmul(a, b):
    m, kd = a.shape
    _, n = b.shape
    tm = _pick(m, (512, 256, 128))
    tn = _pick(n, (512, 256, 128))
    tk = _pick(kd, (1024, 512, 256, 128))
    return pl.pallas_call(
        _mm_kernel,
        grid=(m // tm, n // tn, kd // tk),
        in_specs=[pl.BlockSpec((tm, tk), lambda i, j, k: (i, k)),
                  pl.BlockSpec((tk, tn), lambda i, j, k: (k, j))],
        out_specs=pl.BlockSpec((tm, tn), lambda i, j, k: (i, j)),
        out_shape=jax.ShapeDtypeStruct((m, n), jnp.float32),
        scratch_shapes=[pltpu.VMEM((tm, tn), jnp.float32)],
        compiler_params=pltpu.CompilerParams(
            dimension_semantics=("parallel", "parallel", "arbitrary"),
            vmem_limit_bytes=VMEM_LIMIT_BYTES),
    )(a, b)


def pmatmul_nd(x, w):
    lead = x.shape[:-1]
    return pmatmul(x.reshape(-1, x.shape[-1]), w).reshape(lead + (w.shape[-1],))


NEG_INF = float('-inf')


def _gelu_tanh(x):
    return 0.5 * x * (1.0 + jnp.tanh(np.float32(np.sqrt(2.0 / np.pi)) * (x + np.float32(0.044715) * (x * x * x))))


def _extract_top(s, k):
    rows = lax.broadcasted_iota(jnp.int32, s.shape, 0)
    nrow = s.shape[0]
    work = s
    taken = jnp.zeros(s.shape, jnp.bool_)
    tops = []
    for _ in range(k):
        m = jnp.max(work, axis=0, keepdims=True)
        first = jnp.min(jnp.where(work == m, rows, nrow), axis=0, keepdims=True)
        hit = rows == first
        taken = jnp.logical_or(taken, hit)
        work = jnp.where(hit, NEG_INF, work)
        tops.append(m)
    return tops, taken


def _peer_route_kernel(qt_ref, keys_ref, s1_ref, e1_ref, s2_ref, e2_ref, tau_ref):
    halves = []
    for c in range(2):
        q = qt_ref[c * PEER_QHALF:(c + 1) * PEER_QHALF, :].astype(jnp.bfloat16)
        s = jnp.dot(keys_ref[c].astype(jnp.bfloat16), q, preferred_element_type=jnp.float32)
        tops, taken = _extract_top(s, PEER_TOPK)
        halves.append((s, tops, taken))
    (sa, ta, ma), (sb, tb, mb) = halves
    tb_col = jnp.concatenate(tb, axis=0)
    cand = jnp.concatenate([ta[a] + tb_col for a in range(PEER_TOPK)], axis=0)
    ctops, _ = _extract_top(cand, PEER_TOPK)
    cmax = ctops[0]
    z = jnp.exp(ctops[0] - cmax)
    for r in range(1, PEER_TOPK):
        z = z + jnp.exp(ctops[r] - cmax)
    tau_ref[...] = ctops[PEER_TOPK - 1]
    s1_ref[...] = jnp.where(ma, sa, NEG_INF)
    s2_ref[...] = jnp.where(mb, sb, NEG_INF)
    e1_ref[...] = jnp.where(ma, jnp.exp(sa - ta[0]), 0.0) / z
    e2_ref[...] = jnp.where(mb, jnp.exp(sb - tb[0]), 0.0)


def peer_route(qt, pkeys, tb):
    n = qt.shape[1]
    big = jax.ShapeDtypeStruct((PEER_HEADS, N_KEYS, n), jnp.float32)
    bspec = pl.BlockSpec((None, N_KEYS, tb), lambda i, h: (h, 0, i))
    return pl.pallas_call(
        _peer_route_kernel,
        grid=(n // tb, PEER_HEADS),
        in_specs=[pl.BlockSpec((2 * PEER_QHALF, tb), lambda i, h: (h, i)),
                  pl.BlockSpec((None, 2, N_KEYS, PEER_QHALF), lambda i, h: (h, 0, 0, 0))],
        out_specs=[bspec, bspec, bspec, bspec, pl.BlockSpec((None, 1, tb), lambda i, h: (h, 0, i))],
        out_shape=[big, big, big, big, jax.ShapeDtypeStruct((PEER_HEADS, 1, n), jnp.float32)],
        compiler_params=pltpu.CompilerParams(dimension_semantics=("parallel", "parallel")),
        name="peer_route",
    )(qt, pkeys)


def _peer_dense_kernel(xt_ref, u_ref, vt_ref, s1_ref, e1_ref, s2_ref, e2_ref, tau_ref, o_ref, *, sub):
    j = pl.program_id(1)

    @pl.when(j == 0)
    def _():
        o_ref[...] = jnp.zeros_like(o_ref)

    ht = jnp.dot(u_ref[...], xt_ref[...], preferred_element_type=jnp.float32)
    acts = []
    for a in range(sub):
        i1 = j * sub + a
        g = None
        for h in range(PEER_HEADS):
            s1row = s1_ref[h, pl.ds(i1, 1), :]
            e1row = e1_ref[h, pl.ds(i1, 1), :]
            c = s2_ref[h] + s1row
            t = jnp.where(c >= tau_ref[h], e2_ref[h], 0.0) * e1row
            g = t if g is None else g + t
        acts.append((_gelu_tanh(ht[a * N_KEYS:(a + 1) * N_KEYS, :]) * g).astype(jnp.bfloat16))
    act = jnp.concatenate(acts, axis=0) if sub > 1 else acts[0]
    o_ref[...] += jnp.dot(vt_ref[...], act, preferred_element_type=jnp.float32)


def peer_dense(xt, u_bf, vt_bf, s1, e1, s2, e2, tau, tb, sub, vmem_limit):
    d, n = xt.shape
    e = u_bf.shape[0]
    te = sub * N_KEYS
    once = dict(pipeline_mode=pl.Buffered(1))
    rspec = pl.BlockSpec((PEER_HEADS, N_KEYS, tb), lambda i, j: (0, 0, i), **once)
    return pl.pallas_call(
        functools.partial(_peer_dense_kernel, sub=sub),
        grid=(n // tb, e // te),
        in_specs=[pl.BlockSpec((d, tb), lambda i, j: (0, i), **once),
                  pl.BlockSpec((te, d), lambda i, j: (j, 0)),
                  pl.BlockSpec((d, te), lambda i, j: (0, j)),
                  rspec, rspec, rspec, rspec,
                  pl.BlockSpec((PEER_HEADS, 1, tb), lambda i, j: (0, 0, i), **once)],
        out_specs=pl.BlockSpec((d, tb), lambda i, j: (0, i)),
        out_shape=jax.ShapeDtypeStruct((d, n), jnp.float32),
        compiler_params=pltpu.CompilerParams(dimension_semantics=("parallel", "arbitrary"),
                                             vmem_limit_bytes=vmem_limit),
        name="peer_dense",
    )(xt, u_bf, vt_bf, s1, e1, s2, e2, tau)


def _sel_attn_kernel(q_ref, k_ref, v_ref, m_ref, o_ref, *, tq):
    qi = pl.program_id(2)
    t = k_ref.shape[0]
    n_s = m_ref.shape[1]
    k = k_ref[...].astype(jnp.bfloat16)
    v = v_ref[...].astype(jnp.bfloat16)
    blk_of_key = lax.broadcasted_iota(jnp.int32, (n_s, t), 1) // SEL_LEN
    expand = (blk_of_key == lax.broadcasted_iota(jnp.int32, (n_s, t), 0)).astype(jnp.bfloat16)
    picked = jnp.dot(m_ref[...].astype(jnp.bfloat16), expand, preferred_element_type=jnp.float32)
    qpos = qi * tq + lax.broadcasted_iota(jnp.int32, (tq, t), 0)
    kpos = lax.broadcasted_iota(jnp.int32, (tq, t), 1)
    keep = jnp.logical_and(picked > 0.5, kpos <= qpos)
    outs = []
    for j in range(J_N):
        q = q_ref[:, j * DH_N:(j + 1) * DH_N].astype(jnp.bfloat16)
        s = lax.dot_general(q, k, (((1,), (1,)), ((), ())), preferred_element_type=jnp.float32) * SCALE_N
        s = jnp.where(keep, s, NEG)
        m = jnp.max(s, axis=-1, keepdims=True)
        e = jnp.exp(s - m)
        den = jnp.sum(e, axis=-1, keepdims=True)
        o = jnp.dot(e.astype(jnp.bfloat16), v, preferred_element_type=jnp.float32)
        outs.append(o / den)
    o_ref[...] = jnp.concatenate(outs, axis=1)


def sel_attention(qn, kvn, selmask, b, l, tq=128):
    n_s = selmask.shape[-1]
    nq = l // tq
    return pl.pallas_call(
        functools.partial(_sel_attn_kernel, tq=tq),
        grid=(b, G_N, nq),
        in_specs=[pl.BlockSpec((tq, J_N * DH_N), lambda bi, g, qi: (bi * nq + qi, g)),
                  pl.BlockSpec((l, DH_N), lambda bi, g, qi: (bi, 2 * G_N + g)),
                  pl.BlockSpec((l, DH_N), lambda bi, g, qi: (bi, 3 * G_N + g)),
                  pl.BlockSpec((None, None, tq, n_s), lambda bi, g, qi: (bi, g, qi, 0))],
        out_specs=pl.BlockSpec((tq, J_N * DH_N), lambda bi, g, qi: (bi * nq + qi, g)),
        out_shape=jax.ShapeDtypeStruct((b * l, G_N * J_N * DH_N), jnp.float32),
        compiler_params=pltpu.CompilerParams(dimension_semantics=("parallel", "parallel", "arbitrary"),
                                             vmem_limit_bytes=VMEM_LIMIT_BYTES),
        name="nsa_sel_attn",
    )(qn, kvn, kvn, selmask)


def rmsnorm(x, g):
    xf = x.astype(jnp.float32)
    xf = xf * lax.rsqrt(jnp.mean(xf * xf, axis=-1, keepdims=True) + EPS)
    return (xf * g.astype(jnp.float32)).astype(x.dtype)


def blockdiag(x, w):
    xb = x.reshape(x.shape[:-1] + (w.shape[0], QKV_BLOCK))
    return jnp.einsum('...ni,nio->...no', xb, w).reshape(x.shape)


def mlstm_chunk_step(carry, xs):
    C0, n0, m0 = carry
    q, k, v, a, lf = xs
    L = q.shape[2]
    b = jnp.cumsum(lf, axis=-1)
    causal = jnp.tril(jnp.ones((L, L), dtype=bool))
    dmat = jnp.where(causal, b[..., :, None] - b[..., None, :] + a[..., None, :], -jnp.inf)
    inter = b + m0[..., None]
    m = jnp.maximum(inter, jnp.max(dmat, axis=-1))
    w_intra = jnp.exp(dmat - m[..., None])
    w_inter = jnp.exp(inter - m)
    qk = jnp.einsum('bhtd,bhsd->bhts', q, k) * w_intra
    num = w_inter[..., None] * jnp.einsum('bhtd,bhde->bhte', q, C0) + jnp.einsum('bhts,bhse->bhte', qk, v)
    den = w_inter * jnp.einsum('bhtd,bhd->bht', q, n0) + jnp.sum(qk, axis=-1)
    h = num / jnp.maximum(jnp.abs(den), jnp.exp(-m))[..., None]
    m_end = m[..., -1]
    w_end = w_intra[..., -1, :]
    decay = jnp.exp(inter[..., -1] - m_end)
    C1 = decay[..., None, None] * C0 + jnp.einsum('bhsd,bhse->bhde', k * w_end[..., None], v)
    n1 = decay[..., None] * n0 + jnp.einsum('bhs,bhsd->bhd', w_end, k)
    return (C1, n1, m_end), h


def mlstm_sequence(q, k, v, a, lf, C0, n0, m0):
    B, NH, L, _ = q.shape
    ch = MLSTM_CHUNK if L % MLSTM_CHUNK == 0 else L
    nc = L // ch

    def split(t):
        return jnp.moveaxis(t.reshape(t.shape[:2] + (nc, ch) + t.shape[3:]), 2, 0)

    (C1, n1, m1), h = lax.scan(mlstm_chunk_step, (C0, n0, m0), (split(q), split(k), split(v), split(a), split(lf)))
    h = jnp.moveaxis(h, 0, 2).reshape(B, NH, L, DH_M)
    return h, C1, n1, m1


def compress_blocks(rows, pe, w1, w2):
    B, T = rows.shape[:2]
    tc = -(-T // CMP_STRIDE)
    rows = jnp.pad(rows, ((0, 0), (0, tc * CMP_STRIDE - T), (0, 0), (0, 0)))
    chunks = rows.reshape(B, tc, CMP_STRIDE, G_N, DH_N)
    first = jnp.einsum('bclgd,ldh->bcgh', chunks[:, :-1] + pe[:CMP_STRIDE, None, :], w1[:CMP_STRIDE])
    second = jnp.einsum('bclgd,ldh->bcgh', chunks[:, 1:] + pe[CMP_STRIDE:, None, :], w1[CMP_STRIDE:])
    return jnp.einsum('bcgh,hd->bcgd', jax.nn.gelu(first + second), w2)


def cmp_sel_attend(qg, rows, q_pos, cmp_pe, cmp_w1, cmp_w2, dense_sel=None):
    B, L = qg.shape[:2]
    T = rows[0].shape[1]
    k_cmp = compress_blocks(rows[0], cmp_pe[0], cmp_w1[0], cmp_w2[0])
    v_cmp = compress_blocks(rows[1], cmp_pe[1], cmp_w1[1], cmp_w2[1])
    n_c = k_cmp.shape[1]
    cstart = jnp.arange(n_c) * CMP_STRIDE
    cmask = (cstart[None, :] + CMP_LEN <= q_pos[:, None] + 1)[None, :, None, None, :]
    s = jnp.einsum('bqgjd,bcgd->bqgjc', qg, k_cmp).astype(jnp.float32) * SCALE_N
    p_cmp = jax.nn.softmax(jnp.where(cmask, s, NEG), axis=-1) * cmask
    o_cmp = jnp.einsum('bqgjc,bcgd->bqgjd', p_cmp.astype(v_cmp.dtype), v_cmp)
    n_s = -(-T // SEL_LEN)
    sstart = jnp.arange(n_s) * SEL_LEN
    cover = ((cstart[:, None] < sstart[None, :] + SEL_LEN) & (cstart[:, None] + CMP_LEN > sstart[None, :])).astype(jnp.float32)
    imp = jnp.einsum('bqgjc,cs->bqgs', p_cmp, cover)
    blk = jnp.arange(n_s)[None, :]
    cur = (q_pos // SEL_LEN)[:, None]
    valid = blk <= cur
    forced = (blk == 0) | (blk >= cur - 1)
    score = jnp.where(valid[None, :, None, :], imp + jnp.where(forced, BIG, 0.0)[None, :, None, :], -BIG)
    n_sel = min(N_SEL, n_s)
    _, idx = lax.top_k(score, n_sel)
    if dense_sel is not None:
        qn2, kvn2 = dense_sel
        selmask = (jnp.sum(jax.nn.one_hot(idx, n_s, dtype=jnp.float32), axis=-2) > 0).astype(jnp.float32)
        o_sel = sel_attention(qn2, kvn2, selmask.transpose(0, 2, 1, 3), B, L)
        return o_cmp, o_sel.reshape(qg.shape)
    pad_s = n_s * SEL_LEN - T

    def to_blocks(r):
        r = jnp.pad(r, ((0, 0), (0, pad_s), (0, 0), (0, 0)))
        return r.reshape(B, n_s, SEL_LEN, G_N, DH_N).transpose(0, 3, 1, 2, 4)

    k_sel = to_blocks(rows[2])
    v_sel = to_blocks(rows[3])
    bi = jnp.arange(B)[:, None, None, None]
    gi = jnp.arange(G_N)[None, None, :, None]

    def sel_block(args):
        qq, ii, pp = args
        kb = k_sel[bi, gi, ii]
        vb = v_sel[bi, gi, ii]
        ss = jnp.einsum('bqgjd,bqgkrd->bqgjkr', qq, kb).astype(jnp.float32) * SCALE_N
        kpos = ii[..., None] * SEL_LEN + jnp.arange(SEL_LEN)
        keep = (kpos <= pp[None, :, None, None, None])[:, :, :, None]
        ss = jnp.where(keep, ss, NEG)
        pr = jax.nn.softmax(ss.reshape(ss.shape[:4] + (-1,)), axis=-1).reshape(ss.shape)
        return jnp.einsum('bqgjkr,bqgkrd->bqgjd', pr.astype(vb.dtype), vb)

    qb = SEL_QBLK if L % SEL_QBLK == 0 else L
    nqb = L // qb
    o_sel = lax.map(sel_block, (
        jnp.moveaxis(qg.reshape((B, nqb, qb) + qg.shape[2:]), 1, 0),
        jnp.moveaxis(idx.reshape((B, nqb, qb) + idx.shape[2:]), 1, 0),
        q_pos.reshape(nqb, qb)))
    o_sel = jnp.moveaxis(o_sel, 0, 1).reshape(qg.shape)
    return o_cmp, o_sel


def window_attend(qw, kw, vw, q_pos, k_pos):
    s = jnp.einsum('bnqgjd,bnkgd->bnqgjk', qw, kw).astype(jnp.float32) * SCALE_N
    kp = k_pos[:, None, :]
    qp = q_pos[:, :, None]
    keep = (kp <= qp) & (kp > qp - WINDOW) & (kp >= 0)
    p = jax.nn.softmax(jnp.where(keep[None, :, :, None, None, :], s, NEG), axis=-1)
    return jnp.einsum('bnqgjk,bnkgd->bnqgjd', p.astype(vw.dtype), vw)


def token_mixers(xn, w_in, w_conv, b_conv, w_bd, b_gates, w_hnorm, w_skip, cmp_pe, cmp_w1, cmp_w2,
                 w_br_m, w_br_n, w_out, conv_prev, C0, n0, m0, kv_past, win_past):
    B, L, _ = xn.shape
    f32 = jnp.float32
    offs = np.concatenate([[0], np.cumsum(IN_SIZES)]).tolist()
    xn2 = xn.reshape(B * L, D_MODEL)
    parts = []
    for i, sz in enumerate(IN_SIZES):
        w_seg = w_in[:, offs[i]:offs[i + 1]]
        pad = (-sz) % 128
        if pad:
            w_seg = jnp.pad(w_seg, ((0, 0), (0, pad)))
        parts.append(pmatmul(xn2, w_seg)[:, :sz].reshape(B, L, sz))
    xm, vm, om, ig, fg, qn, kvn, wn, gn, g_m, g_n = parts
    xm_ext = jnp.concatenate([conv_prev.astype(xm.dtype), xm], axis=1)
    conv = b_conv + sum(xm_ext[:, w:w + L] * w_conv[w] for w in range(CONV_W))
    c = jax.nn.silu(conv)
    new_conv = xm_ext[:, L:]

    def heads(t):
        return t.reshape(B, L, NH_M, DH_M).transpose(0, 2, 1, 3).astype(f32)

    q_m = heads(blockdiag(c, w_bd[0]))
    k_m = heads(blockdiag(c, w_bd[1])) * (DH_M ** -0.5)
    v_m = heads(blockdiag(vm, w_bd[2]))
    a_m = (ig + b_gates[0]).astype(f32).transpose(0, 2, 1)
    lf_m = jax.nn.log_sigmoid((fg + b_gates[1]).astype(f32)).transpose(0, 2, 1)
    h_m, C1, n1, m1 = mlstm_sequence(q_m, k_m, v_m, a_m, lf_m, C0.astype(f32), n0.astype(f32), m0.astype(f32))
    h_m = h_m.transpose(0, 2, 1, 3) * jax.nn.sigmoid(om.astype(f32)).reshape(B, L, NH_M, DH_M)
    mu = jnp.mean(h_m, axis=-1, keepdims=True)
    var = jnp.mean(jnp.square(h_m - mu), axis=-1, keepdims=True)
    h_m = ((h_m - mu) * lax.rsqrt(var + EPS)).reshape(B, L, D_M)
    y_m = (h_m * w_hnorm.astype(f32) + w_skip.astype(f32) * c.astype(f32)).astype(xn.dtype)
    qg = qn.reshape(B, L, G_N, J_N, DH_N)
    kv_new = kvn.reshape(B, L, N_KV_SETS, G_N, DH_N)
    win_new = wn.reshape(B, L, 2, G_N, DH_N)
    if kv_past is None:
        rows = [kv_new[:, :, s] for s in range(N_KV_SETS)]
    else:
        rows = [jnp.concatenate([kv_past[s], kv_new[:, :, s]], axis=1) for s in range(N_KV_SETS)]
    T = rows[0].shape[1]
    q_pos = (T - L) + jnp.arange(L)
    dense_sel = (qn.reshape(B * L, D_N), kvn.reshape(B * L, N_KV_SETS * KVW)) if kv_past is None else None
    o_cmp, o_sel = cmp_sel_attend(qg, rows, q_pos, cmp_pe, cmp_w1, cmp_w2, dense_sel)
    if win_past is None:
        nb = L // WIN_QBLK
        nwb = WINDOW // WIN_QBLK
        wpad = jnp.pad(win_new, ((0, 0), (WINDOW, 0), (0, 0), (0, 0), (0, 0)))
        wblk = wpad.reshape(B, nb + nwb, WIN_QBLK, 2, G_N, DH_N)
        bidx = jnp.arange(nb)[:, None] + jnp.arange(nwb + 1)[None, :]
        kw = wblk[:, bidx].reshape(B, nb, (nwb + 1) * WIN_QBLK, 2, G_N, DH_N)
        k_pos = ((bidx * WIN_QBLK - WINDOW)[..., None] + jnp.arange(WIN_QBLK)).reshape(nb, -1)
        qw = qg.reshape(B, nb, WIN_QBLK, G_N, J_N, DH_N)
        q_pos_w = q_pos.reshape(nb, WIN_QBLK)
        new_win = win_new[:, -min(WINDOW, L):]
    else:
        wbuf = win_past.shape[1]
        wrows = jnp.concatenate([win_past, win_new], axis=1)
        kw = wrows[:, None]
        k_pos = ((T - L - wbuf) + jnp.arange(wbuf + L))[None]
        qw = qg[:, None]
        q_pos_w = q_pos[None]
        new_win = wrows[:, -wbuf:]
    o_win = window_attend(qw, kw[..., 0, :, :], kw[..., 1, :, :], q_pos_w, k_pos).reshape(qg.shape)
    gts = jax.nn.sigmoid(gn.astype(f32)).reshape(B, L, G_N, J_N, 3)
    y_n = (gts[..., 0:1] * o_cmp + gts[..., 1:2] * o_sel + gts[..., 2:3] * o_win).astype(xn.dtype).reshape(B, L, D_N)
    merged = jax.nn.sigmoid(g_m) * pmatmul_nd(y_m, w_br_m) + jax.nn.sigmoid(g_n) * pmatmul_nd(y_n, w_br_n)
    return pmatmul_nd(merged, w_out), kv_new, new_win, new_conv, C1, n1, m1


def peer_ffn(xn, w_pq_t, pkeys, u_bf, vt_bf):
    B, L, D = xn.shape
    n = B * L
    xt = xn.reshape(n, D).T.astype(jnp.bfloat16)
    qt = pmatmul(w_pq_t, xt)
    s1, e1, s2, e2, tau = peer_route(qt, pkeys, _pick(n, (256, 128)))
    out_t = peer_dense(xt, u_bf, vt_bf, s1, e1, s2, e2, tau, _pick(n, (512, 256, 128)), PEER_SUB,
                       PEER_VMEM_LIMIT_BYTES)
    return out_t.T.reshape(B, L, D)


def kernel(x_prompt, x_sample, cache_kv, state_win_kv, state_conv, state_C, state_n, state_m, page_table,
           norm_mix, norm_ffn, norm_final, w_in, w_conv, b_conv, w_bd, b_gates, w_hnorm, w_skip,
           cmp_pe, cmp_w1, cmp_w2, w_br_m, w_br_n, w_out, w_pq, peer_keys, peer_u, peer_v):
    f32 = jnp.float32
    xp, xs = x_prompt, x_sample
    Bp = xp.shape[0]
    Bs = xs.shape[0]
    past = page_table.shape[1] * cache_kv.shape[2]
    l = 0
    mix_w = (w_in[l], w_conv[l], b_conv[l], w_bd[l], b_gates[l], w_hnorm[l], w_skip[l],
             cmp_pe[l], cmp_w1[l], cmp_w2[l], w_br_m[l], w_br_n[l], w_out[l])
    yp, kvp, winp, convp, Cp, n_p, m_p = token_mixers(
        rmsnorm(xp, norm_mix[l]), *mix_w,
        jnp.zeros((Bp, CONV_W - 1, D_M), xp.dtype), jnp.zeros((Bp, NH_M, DH_M, DH_M), f32),
        jnp.zeros((Bp, NH_M, DH_M), f32), jnp.zeros((Bp, NH_M), f32), None, None)
    kv_past = [cache_kv[l][page_table, :, s].reshape(Bs, past, G_N, DH_N) for s in range(N_KV_SETS)]
    ys, kvs, wins, convs, Cs, n_s, m_s = token_mixers(
        rmsnorm(xs, norm_mix[l]), *mix_w,
        state_conv[l], state_C[l], state_n[l], state_m[l], kv_past, state_win_kv[l])
    xp = xp + yp
    xs = xs + ys
    peer_w = (w_pq[l].T, peer_keys[l], peer_u[l].astype(jnp.bfloat16), peer_v[l].T.astype(jnp.bfloat16))
    xp = xp + peer_ffn(rmsnorm(xp, norm_ffn[l]), *peer_w)
    xs = xs + peer_ffn(rmsnorm(xs, norm_ffn[l]), *peer_w)
    y_prompt = rmsnorm(xp, norm_final)
    y_sample = rmsnorm(xs, norm_final)
    st = lambda t: t[None]
    return (y_prompt, y_sample, st(kvp), st(kvs), st(winp), st(wins), st(convp), st(convs),
            st(Cp), st(Cs), st(n_p), st(n_s), st(m_p), st(m_s))
```

```python
import functools

import jax
import jax.numpy as jnp
import numpy as np
from jax import lax
from jax.experimental import pallas as pl
from jax.experimental.pallas import tpu as pltpu

D_MODEL = 4096
DEPTH = 1
D_M = D_MODEL // 2
DH_M = 256
NH_M = D_M // DH_M
CONV_W = 4
QKV_BLOCK = 4
D_N = D_MODEL // 2
DH_N = 128
H_N = D_N // DH_N
G_N = 4
J_N = H_N // G_N
KVW = G_N * DH_N
N_KV_SETS = 4
CMP_STRIDE = 16
CMP_LEN = 2 * CMP_STRIDE
CMP_HID = 2 * DH_N
SEL_LEN = 64
N_SEL = 16
WINDOW = 512
SEL_QBLK = 16
SCALE_N = DH_N ** -0.5
PEER_HEADS = 8
N_KEYS = 128
PEER_TOPK = 16
PEER_QDIM = 256
PEER_QHALF = PEER_QDIM // 2
IN_SIZES = (D_M, D_M, D_M, NH_M, NH_M, D_N, N_KV_SETS * KVW, 2 * KVW, 3 * H_N, D_MODEL, D_MODEL)
EPS = 1e-6
NEG = -1e30
BIG = 1e9
NEG_INF = float('-inf')

LANES = 128
VMEM_LIMIT_BYTES = 48 * 1024 * 1024
BIG_VMEM_LIMIT_BYTES = 56 * 1024 * 1024
PEER_SUB = 4
PREV_ROWS = 8
bf16 = jnp.bfloat16

_IN_OFFS = np.concatenate([[0], np.cumsum(IN_SIZES)]).tolist()
Z_XM, Z_VM, Z_OM = 0, D_M, 2 * D_M
Z_Q = 3 * D_M
Z_KV = Z_Q + D_N
Z_WIN = Z_KV + N_KV_SETS * KVW
Z_GM = Z_WIN + 2 * KVW
Z_GN = Z_GM + D_MODEL
Z_COLS = Z_GN + D_MODEL
ZG_IG, ZG_FG, ZG_GN = 0, NH_M, 2 * NH_M


def _pick(n, cands):
    for c in cands:
        if n % c == 0:
            return c
    return n


def _gelu_tanh(x):
    return 0.5 * x * (1.0 + jnp.tanh(np.float32(np.sqrt(2.0 / np.pi)) * (x + np.float32(0.044715) * (x * x * x))))


def _split3(x):
    hi = x.astype(bf16)
    r = x - hi.astype(jnp.float32)
    mid = r.astype(bf16)
    lo = (r - mid.astype(jnp.float32)).astype(bf16)
    return hi, mid, lo


def _rmsnorm_kernel(x_ref, g_ref, o_ref):
    x = x_ref[...]
    o_ref[...] = (x * lax.rsqrt(jnp.mean(x * x, axis=-1, keepdims=True) + EPS) * g_ref[...]).astype(o_ref.dtype)


def rmsnorm_rows(x2, g, out_dtype):
    n, d = x2.shape
    tr = _pick(n, (256, 128))
    return pl.pallas_call(
        _rmsnorm_kernel,
        grid=(n // tr,),
        in_specs=[pl.BlockSpec((tr, d), lambda i: (i, 0)), pl.BlockSpec((1, d), lambda i: (0, 0))],
        out_specs=pl.BlockSpec((tr, d), lambda i: (i, 0)),
        out_shape=jax.ShapeDtypeStruct((n, d), out_dtype),
        compiler_params=pltpu.CompilerParams(dimension_semantics=("parallel",), vmem_limit_bytes=VMEM_LIMIT_BYTES),
        name="rmsnorm",
    )(x2, g.reshape(1, d))


def _mm_kernel(a_ref, b_ref, o_ref):
    o_ref[...] = jnp.dot(a_ref[...], b_ref[...], preferred_element_type=jnp.float32).astype(o_ref.dtype)


def pmm(a, b, out_dtype=jnp.float32):
    m, kd = a.shape
    n = b.shape[1]
    tm = _pick(m, (1024, 512, 256, 128))
    tn = _pick(n, (1024, 512, 256, 128))
    return pl.pallas_call(
        _mm_kernel,
        grid=(m // tm, n // tn),
        in_specs=[pl.BlockSpec((tm, kd), lambda i, j: (i, 0)), pl.BlockSpec((kd, tn), lambda i, j: (0, j))],
        out_specs=pl.BlockSpec((tm, tn), lambda i, j: (i, j)),
        out_shape=jax.ShapeDtypeStruct((m, n), out_dtype),
        compiler_params=pltpu.CompilerParams(dimension_semantics=("parallel", "parallel"),
                                             vmem_limit_bytes=BIG_VMEM_LIMIT_BYTES),
        name="proj",
    )(a, b)


def _merge_kernel(ym_ref, yn_ref, wm_ref, wn_ref, gm_ref, gn_ref, o_ref):
    pm = jnp.dot(ym_ref[...], wm_ref[...], preferred_element_type=jnp.float32)
    pn = jnp.dot(yn_ref[...], wn_ref[...], preferred_element_type=jnp.float32)
    o_ref[...] = (jax.nn.sigmoid(gm_ref[...]) * pm + jax.nn.sigmoid(gn_ref[...]) * pn).astype(o_ref.dtype)


def merge_branches(ym, yn, wm, wn, z2):
    m = ym.shape[0]
    n = wm.shape[1]
    tm = _pick(m, (1024, 512, 256, 128))
    tn = 512
    return pl.pallas_call(
        _merge_kernel,
        grid=(m // tm, n // tn),
        in_specs=[pl.BlockSpec((tm, D_M), lambda i, j: (i, 0)), pl.BlockSpec((tm, D_N), lambda i, j: (i, 0)),
                  pl.BlockSpec((D_M, tn), lambda i, j: (0, j)), pl.BlockSpec((D_N, tn), lambda i, j: (0, j)),
                  pl.BlockSpec((tm, tn), lambda i, j: (i, Z_GM // tn + j)),
                  pl.BlockSpec((tm, tn), lambda i, j: (i, Z_GN // tn + j))],
        out_specs=pl.BlockSpec((tm, tn), lambda i, j: (i, j)),
        out_shape=jax.ShapeDtypeStruct((m, n), bf16),
        compiler_params=pltpu.CompilerParams(dimension_semantics=("parallel", "parallel"),
                                             vmem_limit_bytes=VMEM_LIMIT_BYTES),
        name="merge_branches",
    )(ym, yn, wm, wn, z2, z2)


def _mm_res_kernel(a_ref, b_ref, r_ref, o_ref):
    o_ref[...] = r_ref[...] + jnp.dot(a_ref[...], b_ref[...], preferred_element_type=jnp.float32)


def out_proj_residual(a, b, r):
    m, kd = a.shape
    n = b.shape[1]
    tm = _pick(m, (1024, 512, 256, 128))
    tn = 512
    return pl.pallas_call(
        _mm_res_kernel,
        grid=(m // tm, n // tn),
        in_specs=[pl.BlockSpec((tm, kd), lambda i, j: (i, 0)), pl.BlockSpec((kd, tn), lambda i, j: (0, j)),
                  pl.BlockSpec((tm, tn), lambda i, j: (i, j))],
        out_specs=pl.BlockSpec((tm, tn), lambda i, j: (i, j)),
        out_shape=jax.ShapeDtypeStruct((m, n), jnp.float32),
        compiler_params=pltpu.CompilerParams(dimension_semantics=("parallel", "parallel"),
                                             vmem_limit_bytes=VMEM_LIMIT_BYTES),
        name="out_proj",
    )(a, b, r)


def _final_kernel(h_ref, pt_ref, g_ref, o_ref):
    x = h_ref[...] + pt_ref[...].T
    o_ref[...] = x * lax.rsqrt(jnp.mean(x * x, axis=-1, keepdims=True) + EPS) * g_ref[...]


def final_norm(h2, peer_t, g):
    n, d = h2.shape
    tr = _pick(n, (256, 128))
    return pl.pallas_call(
        _final_kernel,
        grid=(n // tr,),
        in_specs=[pl.BlockSpec((tr, d), lambda i: (i, 0)), pl.BlockSpec((d, tr), lambda i: (0, i)),
                  pl.BlockSpec((1, d), lambda i: (0, 0))],
        out_specs=pl.BlockSpec((tr, d), lambda i: (i, 0)),
        out_shape=jax.ShapeDtypeStruct((n, d), jnp.float32),
        compiler_params=pltpu.CompilerParams(dimension_semantics=("parallel",), vmem_limit_bytes=VMEM_LIMIT_BYTES),
        name="final_norm",
    )(h2, peer_t, g.reshape(1, d))


def _blockdiag_apply(x, w_ref, which):
    parts = [jnp.dot(x[:, hb * LANES:(hb + 1) * LANES].astype(bf16), w_ref[which, hb].astype(bf16),
                     preferred_element_type=jnp.float32) for hb in range(DH_M // LANES)]
    return jnp.concatenate(parts, axis=1)


def _mlstm_kernel(xm_ref, vm_ref, om_ref, zg_ref, prev_ref, wc_ref, bc_ref, wbd_ref, gb_ref, c0_ref, n0_ref, m0_ref,
                  hn_ref, sk_ref, y_ref, c1_ref, n1_ref, m1_ref, *, ch, l_true):
    h = pl.program_id(1)
    lp = xm_ref.shape[0]
    nchunk = lp // ch
    f32 = jnp.float32
    tri = lax.broadcasted_iota(jnp.int32, (ch, ch), 0) >= lax.broadcasted_iota(jnp.int32, (ch, ch), 1)
    tri_bf = tri.astype(bf16)
    lane_g = lax.broadcasted_iota(jnp.int32, (ch, LANES), 1)
    sub_g = lax.broadcasted_iota(jnp.int32, (LANES, ch), 0)

    def chunk(ci, carry):
        c_st, n_st, m_st = carry
        r0 = pl.multiple_of(ci * ch, ch)
        xm = xm_ref[pl.ds(r0, ch), :]
        before = xm_ref[pl.ds(pl.multiple_of(jnp.maximum(r0 - PREV_ROWS, 0), PREV_ROWS), PREV_ROWS), :]
        before = jnp.where(ci == 0, prev_ref[...], before)
        xe = jnp.concatenate([before, xm], axis=0)
        lo = PREV_ROWS - (CONV_W - 1)
        conv = bc_ref[...] + sum(xe[lo + w:lo + w + ch, :] * wc_ref[w:w + 1, :] for w in range(CONV_W))
        c = conv * jax.nn.sigmoid(conv)
        q = _blockdiag_apply(c, wbd_ref, 0)
        k = _blockdiag_apply(c, wbd_ref, 1) * (DH_M ** -0.5)
        v = _blockdiag_apply(vm_ref[pl.ds(r0, ch), :], wbd_ref, 2)
        zg = zg_ref[pl.ds(r0, ch), :] + gb_ref[...]
        live = (r0 + lax.broadcasted_iota(jnp.int32, (ch, LANES), 0)) < l_true
        a_all = jnp.where(live, zg, NEG_INF)
        lf_all = jnp.where(live, jnp.minimum(zg, 0.0) - jnp.log1p(jnp.exp(-jnp.abs(zg))), 0.0)
        b_all = jnp.zeros((ch, LANES), f32)
        for part in _split3(lf_all):
            b_all = b_all + jnp.dot(tri_bf, part, preferred_element_type=f32)
        a_col = jnp.sum(jnp.where(lane_g == ZG_IG + h, a_all, 0.0), axis=1, keepdims=True)
        b_col = jnp.sum(jnp.where(lane_g == ZG_FG + h, b_all, 0.0), axis=1, keepdims=True)
        a_row = jnp.sum(jnp.where(sub_g == ZG_IG + h, a_all.T, 0.0), axis=0, keepdims=True)
        b_row = jnp.sum(jnp.where(sub_g == ZG_FG + h, b_all.T, 0.0), axis=0, keepdims=True)
        dmat = jnp.where(tri, b_col - b_row + a_row, NEG_INF)
        inter = b_col + m_st
        m = jnp.maximum(inter, jnp.max(dmat, axis=1, keepdims=True))
        w_intra = jnp.exp(dmat - m)
        w_inter = jnp.exp(inter - m)
        qb = q.astype(bf16)
        kb = k.astype(bf16)
        vb = v.astype(bf16)
        qk = lax.dot_general(qb, kb, (((1,), (1,)), ((), ())), preferred_element_type=f32) * w_intra
        num = w_inter * jnp.dot(qb, c_st.astype(bf16), preferred_element_type=f32) \
            + jnp.dot(qk.astype(bf16), vb, preferred_element_type=f32)
        den = w_inter * jnp.sum(q * n_st, axis=1, keepdims=True) + jnp.sum(qk, axis=1, keepdims=True)
        hh = num / jnp.maximum(jnp.abs(den), jnp.exp(-m))
        m_end = m[ch - 1:ch, :]
        b_end = b_col[ch - 1:ch, :]
        w_end = jnp.exp(b_end - b_col + a_col - m_end)
        decay = jnp.exp(b_end + m_st - m_end)
        kw = k * w_end
        c_new = decay * c_st + lax.dot_general(kw.astype(bf16), vb, (((0,), (0,)), ((), ())),
                                               preferred_element_type=f32)
        n_new = decay * n_st + jnp.sum(kw, axis=0, keepdims=True)
        hg = hh * jax.nn.sigmoid(om_ref[pl.ds(r0, ch), :])
        mu = jnp.mean(hg, axis=1, keepdims=True)
        var = jnp.mean(jnp.square(hg - mu), axis=1, keepdims=True)
        y = (hg - mu) * lax.rsqrt(var + EPS) * hn_ref[...] + sk_ref[...] * c
        y_ref[pl.ds(r0, ch), :] = y.astype(y_ref.dtype)
        return c_new, n_new, m_end

    c_fin, n_fin, m_fin = lax.fori_loop(0, nchunk, chunk, (c0_ref[...], n0_ref[...], m0_ref[:, :1]))
    c1_ref[...] = c_fin
    n1_ref[...] = n_fin
    m1_ref[...] = jnp.broadcast_to(m_fin, m1_ref.shape)


def expand_blockdiag(w_bd):
    per = LANES // QKV_BLOCK
    w = w_bd.reshape(3, D_M // LANES, per, QKV_BLOCK, QKV_BLOCK)
    full = jnp.einsum('tbnio,nm->tbnimo', w, jnp.eye(per, dtype=w_bd.dtype))
    return full.reshape(3, D_M // LANES, LANES, LANES)


def mlstm_branch(z3, zg3, col_blocks, prev8, w_conv, b_conv, wbd_full, gate_bias, c0, n0, m0, w_hnorm, w_skip,
                 l_true, ch):
    b, lp, _ = z3.shape
    cx, cv, co = col_blocks
    kern = functools.partial(_mlstm_kernel, ch=ch, l_true=l_true)
    col = lambda c0_: pl.BlockSpec((None, lp, DH_M), lambda bi, h: (bi, 0, c0_ + h))
    vec = pl.BlockSpec((1, DH_M), lambda bi, h: (0, h))
    st = lambda r, c: pl.BlockSpec((None, None, r, c), lambda bi, h: (bi, h, 0, 0))
    return pl.pallas_call(
        kern,
        grid=(b, NH_M),
        in_specs=[col(cx), col(cv), col(co),
                  pl.BlockSpec((None, lp, LANES), lambda bi, h: (bi, 0, 0)),
                  pl.BlockSpec((None, PREV_ROWS, DH_M), lambda bi, h: (bi, 0, h)),
                  pl.BlockSpec((CONV_W, DH_M), lambda bi, h: (0, h)),
                  vec,
                  pl.BlockSpec((3, DH_M // LANES, LANES, LANES), lambda bi, h: (0, h, 0, 0)),
                  pl.BlockSpec((1, LANES), lambda bi, h: (0, 0)),
                  st(DH_M, DH_M), st(1, DH_M), st(1, LANES), vec, vec],
        out_specs=[pl.BlockSpec((None, lp, DH_M), lambda bi, h: (bi, 0, h)),
                   st(DH_M, DH_M), st(1, DH_M), st(1, LANES)],
        out_shape=[jax.ShapeDtypeStruct((b, lp, D_M), bf16),
                   jax.ShapeDtypeStruct((b, NH_M, DH_M, DH_M), jnp.float32),
                   jax.ShapeDtypeStruct((b, NH_M, 1, DH_M), jnp.float32),
                   jax.ShapeDtypeStruct((b, NH_M, 1, LANES), jnp.float32)],
        compiler_params=pltpu.CompilerParams(dimension_semantics=("parallel", "parallel"),
                                             vmem_limit_bytes=VMEM_LIMIT_BYTES),
        name="mlstm_branch",
    )(z3, z3, z3, zg3, prev8, w_conv, b_conv.reshape(1, D_M), wbd_full, gate_bias, c0, n0, m0,
      w_hnorm.reshape(1, D_M), w_skip.reshape(1, D_M))


def _compress_kernel(x_ref, pe_ref, w1_ref, w2_ref, o_ref):
    nchunk = x_ref.shape[0] // CMP_STRIDE
    first = jnp.zeros((nchunk, CMP_HID), jnp.float32)
    second = jnp.zeros((nchunk, CMP_HID), jnp.float32)
    for l in range(CMP_STRIDE):
        xl = x_ref[pl.ds(l, nchunk, stride=CMP_STRIDE), :]
        first += jnp.dot((xl + pe_ref[l:l + 1, :]).astype(bf16), w1_ref[l].astype(bf16),
                         preferred_element_type=jnp.float32)
        second += jnp.dot((xl + pe_ref[CMP_STRIDE + l:CMP_STRIDE + l + 1, :]).astype(bf16),
                          w1_ref[CMP_STRIDE + l].astype(bf16), preferred_element_type=jnp.float32)
    nxt = jnp.concatenate([second[1:], jnp.zeros((1, CMP_HID), jnp.float32)], axis=0)
    row = lax.broadcasted_iota(jnp.int32, (nchunk, CMP_HID), 0)
    hid = jnp.where(row < nchunk - 1, _gelu_tanh(first + nxt), 0.0)
    o_ref[...] = jnp.dot(hid.astype(bf16), w2_ref[...].astype(bf16), preferred_element_type=jnp.float32)


def compress_prompt(z3, col0, cmp_pe, cmp_w1, cmp_w2):
    b, l, _ = z3.shape
    nchunk = l // CMP_STRIDE
    return pl.pallas_call(
        _compress_kernel,
        grid=(b, 2, G_N),
        in_specs=[pl.BlockSpec((None, l, DH_N), lambda bi, s, g: (bi, 0, col0 + s * G_N + g)),
                  pl.BlockSpec((None, CMP_LEN, DH_N), lambda bi, s, g: (s, 0, 0)),
                  pl.BlockSpec((None, CMP_LEN, DH_N, CMP_HID), lambda bi, s, g: (s, 0, 0, 0)),
                  pl.BlockSpec((None, CMP_HID, DH_N), lambda bi, s, g: (s, 0, 0))],
        out_specs=pl.BlockSpec((None, None, None, nchunk, DH_N), lambda bi, s, g: (bi, s, g, 0, 0)),
        out_shape=jax.ShapeDtypeStruct((b, 2, G_N, nchunk, DH_N), jnp.float32),
        compiler_params=pltpu.CompilerParams(dimension_semantics=("parallel", "parallel", "parallel"),
                                             vmem_limit_bytes=VMEM_LIMIT_BYTES),
        name="nsa_compress",
    )(z3, cmp_pe, cmp_w1, cmp_w2)


def _softmax_pv(s, keep, v):
    s = jnp.where(keep, s, NEG)
    m = jnp.max(s, axis=-1, keepdims=True)
    e = jnp.exp(s - m)
    den = jnp.sum(e, axis=-1, keepdims=True)
    return jnp.dot(e.astype(bf16), v, preferred_element_type=jnp.float32) / den


def _nsa_prompt_kernel(q_ref, kc_ref, vc_ref, ks_ref, vs_ref, kw_ref, vw_ref, gn_ref, o_ref, *, tq):
    g = pl.program_id(1)
    qi = pl.program_id(2)
    t = ks_ref.shape[0]
    n_c = kc_ref.shape[0]
    n_s = t // SEL_LEN
    qs = [q_ref[:, j * DH_N:(j + 1) * DH_N].astype(bf16) for j in range(J_N)]
    qpos_c = qi * tq + lax.broadcasted_iota(jnp.int32, (tq, n_c), 0)
    cidx = lax.broadcasted_iota(jnp.int32, (tq, n_c), 1)
    cmask = cidx * CMP_STRIDE + CMP_LEN <= qpos_c + 1
    kc = kc_ref[...].astype(bf16)
    vc = vc_ref[...].astype(bf16)
    o_cmp = []
    p_sum = jnp.zeros((tq, n_c), jnp.float32)
    for j in range(J_N):
        s = lax.dot_general(qs[j], kc, (((1,), (1,)), ((), ())), preferred_element_type=jnp.float32) * SCALE_N
        s = jnp.where(cmask, s, NEG)
        m = jnp.max(s, axis=-1, keepdims=True)
        e = jnp.where(cmask, jnp.exp(s - m), 0.0)
        den = jnp.sum(e, axis=-1, keepdims=True)
        p = e / jnp.where(den > 0.0, den, 1.0)
        p_sum = p_sum + p
        o_cmp.append(jnp.dot(p.astype(bf16), vc, preferred_element_type=jnp.float32))
    ci = lax.broadcasted_iota(jnp.int32, (n_c, LANES), 0)
    si = lax.broadcasted_iota(jnp.int32, (n_c, LANES), 1)
    cover = jnp.logical_and(ci * CMP_STRIDE < si * SEL_LEN + SEL_LEN,
                            ci * CMP_STRIDE + CMP_LEN > si * SEL_LEN).astype(bf16)
    imp = jnp.zeros((tq, LANES), jnp.float32)
    for part in _split3(p_sum):
        imp = imp + jnp.dot(part, cover, preferred_element_type=jnp.float32)
    blk = lax.broadcasted_iota(jnp.int32, (tq, LANES), 1)
    cur = (qi * tq + lax.broadcasted_iota(jnp.int32, (tq, LANES), 0)) // SEL_LEN
    valid = blk <= cur
    forced = jnp.logical_or(blk == 0, blk >= cur - 1)
    score = jnp.where(valid, imp + jnp.where(forced, BIG, 0.0), -BIG)
    score = jnp.where(blk < n_s, score, NEG_INF)
    rank = jnp.zeros((tq, LANES), jnp.int32)
    for s2 in range(n_s):
        col = score[:, s2:s2 + 1]
        ahead = jnp.logical_or(col > score, jnp.logical_and(col == score, blk > s2))
        rank = rank + ahead.astype(jnp.int32)
    sel = (rank < N_SEL).astype(bf16)
    expand = (lax.broadcasted_iota(jnp.int32, (LANES, t), 1) // SEL_LEN
              == lax.broadcasted_iota(jnp.int32, (LANES, t), 0)).astype(bf16)
    picked = jnp.dot(sel, expand, preferred_element_type=jnp.float32)
    qpos = qi * tq + lax.broadcasted_iota(jnp.int32, (tq, t), 0)
    kpos = lax.broadcasted_iota(jnp.int32, (tq, t), 1)
    keep_sel = jnp.logical_and(picked > 0.5, kpos <= qpos)
    ks = ks_ref[...].astype(bf16)
    vs = vs_ref[...].astype(bf16)
    span = WINDOW + tq
    w0 = pl.multiple_of(jnp.clip(qi * tq - WINDOW, 0, t - span), tq)
    kw = kw_ref[pl.ds(w0, span), :].astype(bf16)
    vw = vw_ref[pl.ds(w0, span), :].astype(bf16)
    qpos_w = qi * tq + lax.broadcasted_iota(jnp.int32, (tq, span), 0)
    kpos_w = w0 + lax.broadcasted_iota(jnp.int32, (tq, span), 1)
    keep_win = jnp.logical_and(kpos_w <= qpos_w, kpos_w > qpos_w - WINDOW)
    gates = jax.nn.sigmoid(gn_ref[...])
    gl = lax.broadcasted_iota(jnp.int32, gates.shape, 1)
    outs = []
    for j in range(J_N):
        s = lax.dot_general(qs[j], ks, (((1,), (1,)), ((), ())), preferred_element_type=jnp.float32) * SCALE_N
        o_sel = _softmax_pv(s, keep_sel, vs)
        s = lax.dot_general(qs[j], kw, (((1,), (1,)), ((), ())), preferred_element_type=jnp.float32) * SCALE_N
        o_win = _softmax_pv(s, keep_win, vw)
        acc = jnp.zeros((tq, DH_N), jnp.float32)
        for which, o in enumerate((o_cmp[j], o_sel, o_win)):
            col = jnp.sum(jnp.where(gl == ZG_GN + (g * J_N + j) * 3 + which, gates, 0.0), axis=-1, keepdims=True)
            acc = acc + col * o
        outs.append(acc)
    o_ref[...] = jnp.concatenate(outs, axis=1).astype(o_ref.dtype)


def nsa_prompt(z3, zg3, kcv, tq=128):
    b, l, _ = z3.shape
    assert l >= WINDOW + tq and l % tq == 0
    nq = l // tq
    n_c = kcv.shape[3]
    full = lambda c0: pl.BlockSpec((None, l, DH_N), lambda bi, g, qi: (bi, 0, c0 // DH_N + g))
    return pl.pallas_call(
        functools.partial(_nsa_prompt_kernel, tq=tq),
        grid=(b, G_N, nq),
        in_specs=[pl.BlockSpec((None, tq, J_N * DH_N), lambda bi, g, qi: (bi, qi, Z_Q // (J_N * DH_N) + g)),
                  pl.BlockSpec((None, None, None, n_c, DH_N), lambda bi, g, qi: (bi, 0, g, 0, 0)),
                  pl.BlockSpec((None, None, None, n_c, DH_N), lambda bi, g, qi: (bi, 1, g, 0, 0)),
                  full(Z_KV + 2 * KVW), full(Z_KV + 3 * KVW), full(Z_WIN), full(Z_WIN + KVW),
                  pl.BlockSpec((None, tq, LANES), lambda bi, g, qi: (bi, qi, 0))],
        out_specs=pl.BlockSpec((None, tq, J_N * DH_N), lambda bi, g, qi: (bi, qi, g)),
        out_shape=jax.ShapeDtypeStruct((b, l, D_N), bf16),
        compiler_params=pltpu.CompilerParams(dimension_semantics=("parallel", "parallel", "arbitrary"),
                                             vmem_limit_bytes=VMEM_LIMIT_BYTES),
        name="nsa_prompt",
    )(z3, kcv, kcv, z3, z3, z3, z3, zg3)


def compress_blocks(rows, pe, w1, w2):
    B, T = rows.shape[:2]
    tc = -(-T // CMP_STRIDE)
    rows = jnp.pad(rows, ((0, 0), (0, tc * CMP_STRIDE - T), (0, 0), (0, 0)))
    chunks = rows.reshape(B, tc, CMP_STRIDE, G_N, DH_N)
    first = jnp.einsum('bclgd,ldh->bcgh', chunks[:, :-1] + pe[:CMP_STRIDE, None, :], w1[:CMP_STRIDE])
    second = jnp.einsum('bclgd,ldh->bcgh', chunks[:, 1:] + pe[CMP_STRIDE:, None, :], w1[CMP_STRIDE:])
    return jnp.einsum('bcgh,hd->bcgd', jax.nn.gelu(first + second), w2)


def cmp_sel_attend(qg, rows, q_pos, cmp_pe, cmp_w1, cmp_w2):
    B, L = qg.shape[:2]
    T = rows[0].shape[1]
    k_cmp = compress_blocks(rows[0], cmp_pe[0], cmp_w1[0], cmp_w2[0])
    v_cmp = compress_blocks(rows[1], cmp_pe[1], cmp_w1[1], cmp_w2[1])
    n_c = k_cmp.shape[1]
    cstart = jnp.arange(n_c) * CMP_STRIDE
    cmask = (cstart[None, :] + CMP_LEN <= q_pos[:, None] + 1)[None, :, None, None, :]
    s = jnp.einsum('bqgjd,bcgd->bqgjc', qg, k_cmp).astype(jnp.float32) * SCALE_N
    p_cmp = jax.nn.softmax(jnp.where(cmask, s, NEG), axis=-1) * cmask
    o_cmp = jnp.einsum('bqgjc,bcgd->bqgjd', p_cmp.astype(v_cmp.dtype), v_cmp)
    n_s = -(-T // SEL_LEN)
    sstart = jnp.arange(n_s) * SEL_LEN
    cover = ((cstart[:, None] < sstart[None, :] + SEL_LEN) & (cstart[:, None] + CMP_LEN > sstart[None, :])).astype(jnp.float32)
    imp = jnp.einsum('bqgjc,cs->bqgs', p_cmp, cover)
    blk = jnp.arange(n_s)[None, :]
    cur = (q_pos // SEL_LEN)[:, None]
    valid = blk <= cur
    forced = (blk == 0) | (blk >= cur - 1)
    score = jnp.where(valid[None, :, None, :], imp + jnp.where(forced, BIG, 0.0)[None, :, None, :], -BIG)
    n_sel = min(N_SEL, n_s)
    _, idx = lax.top_k(score, n_sel)
    pad_s = n_s * SEL_LEN - T

    def to_blocks(r):
        r = jnp.pad(r, ((0, 0), (0, pad_s), (0, 0), (0, 0)))
        return r.reshape(B, n_s, SEL_LEN, G_N, DH_N).transpose(0, 3, 1, 2, 4)

    k_sel = to_blocks(rows[2])
    v_sel = to_blocks(rows[3])
    bi = jnp.arange(B)[:, None, None, None]
    gi = jnp.arange(G_N)[None, None, :, None]

    def sel_block(args):
        qq, ii, pp = args
        kb = k_sel[bi, gi, ii]
        vb = v_sel[bi, gi, ii]
        ss = jnp.einsum('bqgjd,bqgkrd->bqgjkr', qq, kb).astype(jnp.float32) * SCALE_N
        kpos = ii[..., None] * SEL_LEN + jnp.arange(SEL_LEN)
        keep = (kpos <= pp[None, :, None, None, None])[:, :, :, None]
        ss = jnp.where(keep, ss, NEG)
        pr = jax.nn.softmax(ss.reshape(ss.shape[:4] + (-1,)), axis=-1).reshape(ss.shape)
        return jnp.einsum('bqgjkr,bqgkrd->bqgjd', pr.astype(vb.dtype), vb)

    qb = SEL_QBLK if L % SEL_QBLK == 0 else L
    nqb = L // qb
    o_sel = lax.map(sel_block, (
        jnp.moveaxis(qg.reshape((B, nqb, qb) + qg.shape[2:]), 1, 0),
        jnp.moveaxis(idx.reshape((B, nqb, qb) + idx.shape[2:]), 1, 0),
        q_pos.reshape(nqb, qb)))
    o_sel = jnp.moveaxis(o_sel, 0, 1).reshape(qg.shape)
    return o_cmp, o_sel


def window_attend(qw, kw, vw, q_pos, k_pos):
    s = jnp.einsum('bnqgjd,bnkgd->bnqgjk', qw, kw).astype(jnp.float32) * SCALE_N
    kp = k_pos[:, None, :]
    qp = q_pos[:, :, None]
    keep = (kp <= qp) & (kp > qp - WINDOW) & (kp >= 0)
    p = jax.nn.softmax(jnp.where(keep[None, :, :, None, None, :], s, NEG), axis=-1)
    return jnp.einsum('bnqgjk,bnkgd->bnqgjd', p.astype(vw.dtype), vw)


def nsa_sample(qn, kv_new, win_new, gn, kv_past, win_past, cmp_pe, cmp_w1, cmp_w2):
    B, L = qn.shape[:2]
    qg = qn.reshape(B, L, G_N, J_N, DH_N)
    rows = [jnp.concatenate([kv_past[s], kv_new[:, :, s]], axis=1) for s in range(N_KV_SETS)]
    T = rows[0].shape[1]
    q_pos = (T - L) + jnp.arange(L)
    o_cmp, o_sel = cmp_sel_attend(qg, rows, q_pos, cmp_pe, cmp_w1, cmp_w2)
    wbuf = win_past.shape[1]
    wrows = jnp.concatenate([win_past, win_new], axis=1)
    kw = wrows[:, None]
    k_pos = ((T - L - wbuf) + jnp.arange(wbuf + L))[None]
    o_win = window_attend(qg[:, None], kw[..., 0, :, :], kw[..., 1, :, :], q_pos[None], k_pos).reshape(qg.shape)
    gts = jax.nn.sigmoid(gn.astype(jnp.float32)).reshape(B, L, G_N, J_N, 3)
    y_n = (gts[..., 0:1] * o_cmp + gts[..., 1:2] * o_sel + gts[..., 2:3] * o_win).reshape(B, L, D_N)
    return y_n, wrows[:, -wbuf:]


def _extract_top(s, k):
    rows = lax.broadcasted_iota(jnp.int32, s.shape, 0)
    nrow = s.shape[0]
    work = s
    taken = jnp.zeros(s.shape, jnp.bool_)
    tops = []
    for _ in range(k):
        m = jnp.max(work, axis=0, keepdims=True)
        first = jnp.min(jnp.where(work == m, rows, nrow), axis=0, keepdims=True)
        hit = rows == first
        taken = jnp.logical_or(taken, hit)
        work = jnp.where(hit, NEG_INF, work)
        tops.append(m)
    return tops, taken


def _peer_route_kernel(qt_ref, keys_ref, s1_ref, e1_ref, s2_ref, e2_ref, tau_ref):
    halves = []
    for c in range(2):
        q = qt_ref[c * PEER_QHALF:(c + 1) * PEER_QHALF, :].astype(bf16)
        s = jnp.dot(keys_ref[c].astype(bf16), q, preferred_element_type=jnp.float32)
        tops, taken = _extract_top(s, PEER_TOPK)
        halves.append((s, tops, taken))
    (sa, ta, ma), (sb, tb, mb) = halves
    tb_col = jnp.concatenate(tb, axis=0)
    cand = jnp.concatenate([ta[a] + tb_col for a in range(PEER_TOPK)], axis=0)
    ctops, _ = _extract_top(cand, PEER_TOPK)
    cmax = ctops[0]
    z = jnp.exp(ctops[0] - cmax)
    for r in range(1, PEER_TOPK):
        z = z + jnp.exp(ctops[r] - cmax)
    tau_ref[...] = ctops[PEER_TOPK - 1]
    s1_ref[...] = jnp.where(ma, sa, NEG_INF)
    s2_ref[...] = jnp.where(mb, sb, NEG_INF)
    e1_ref[...] = jnp.where(ma, jnp.exp(sa - ta[0]), 0.0) / z
    e2_ref[...] = jnp.where(mb, jnp.exp(sb - tb[0]), 0.0)


def peer_route(qt, pkeys, tb):
    n = qt.shape[1]
    big = jax.ShapeDtypeStruct((PEER_HEADS, N_KEYS, n), jnp.float32)
    bspec = pl.BlockSpec((None, N_KEYS, tb), lambda i, h: (h, 0, i))
    return pl.pallas_call(
        _peer_route_kernel,
        grid=(n // tb, PEER_HEADS),
        in_specs=[pl.BlockSpec((2 * PEER_QHALF, tb), lambda i, h: (h, i)),
                  pl.BlockSpec((None, 2, N_KEYS, PEER_QHALF), lambda i, h: (h, 0, 0, 0))],
        out_specs=[bspec, bspec, bspec, bspec, pl.BlockSpec((None, 1, tb), lambda i, h: (h, 0, i))],
        out_shape=[big, big, big, big, jax.ShapeDtypeStruct((PEER_HEADS, 1, n), jnp.float32)],
        compiler_params=pltpu.CompilerParams(dimension_semantics=("parallel", "parallel")),
        name="peer_route",
    )(qt, pkeys)


def _peer_dense_kernel(xt_ref, u_ref, vt_ref, s1_ref, e1_ref, s2_ref, e2_ref, tau_ref, o_ref, *, sub):
    j = pl.program_id(1)

    @pl.when(j == 0)
    def _():
        o_ref[...] = jnp.zeros_like(o_ref)

    ht = jnp.dot(u_ref[...], xt_ref[...], preferred_element_type=jnp.float32)
    acts = []
    for a in range(sub):
        i1 = j * sub + a
        g = None
        for h in range(PEER_HEADS):
            s1row = s1_ref[h, pl.ds(i1, 1), :]
            e1row = e1_ref[h, pl.ds(i1, 1), :]
            c = s2_ref[h] + s1row
            t = jnp.where(c >= tau_ref[h], e2_ref[h], 0.0) * e1row
            g = t if g is None else g + t
        acts.append((_gelu_tanh(ht[a * N_KEYS:(a + 1) * N_KEYS, :]) * g).astype(bf16))
    act = jnp.concatenate(acts, axis=0) if sub > 1 else acts[0]
    o_ref[...] += jnp.dot(vt_ref[...], act, preferred_element_type=jnp.float32)


def peer_dense(xt, u_bf, vt_bf, s1, e1, s2, e2, tau, tb, sub):
    d, n = xt.shape
    e = u_bf.shape[0]
    te = sub * N_KEYS
    once = dict(pipeline_mode=pl.Buffered(1))
    rspec = pl.BlockSpec((PEER_HEADS, N_KEYS, tb), lambda i, j: (0, 0, i), **once)
    return pl.pallas_call(
        functools.partial(_peer_dense_kernel, sub=sub),
        grid=(n // tb, e // te),
        in_specs=[pl.BlockSpec((d, tb), lambda i, j: (0, i), **once),
                  pl.BlockSpec((te, d), lambda i, j: (j, 0)),
                  pl.BlockSpec((d, te), lambda i, j: (0, j)),
                  rspec, rspec, rspec, rspec,
                  pl.BlockSpec((PEER_HEADS, 1, tb), lambda i, j: (0, 0, i), **once)],
        out_specs=pl.BlockSpec((d, tb), lambda i, j: (0, i)),
        out_shape=jax.ShapeDtypeStruct((d, n), jnp.float32),
        compiler_params=pltpu.CompilerParams(dimension_semantics=("parallel", "arbitrary"),
                                             vmem_limit_bytes=BIG_VMEM_LIMIT_BYTES),
        name="peer_dense",
    )(xt, u_bf, vt_bf, s1, e1, s2, e2, tau)


def peer_ffn_t(xn_bf, w_pq_t, pkeys, u_bf, vt_bf):
    n = xn_bf.shape[0]
    xt = xn_bf.T
    qt = pmm(w_pq_t, xt)
    s1, e1, s2, e2, tau = peer_route(qt, pkeys, _pick(n, (256, 128)))
    return peer_dense(xt, u_bf, vt_bf, s1, e1, s2, e2, tau, _pick(n, (512, 256, 128)), PEER_SUB)


def _in_proj_weights(w_in):
    o = _IN_OFFS
    w_main = jnp.concatenate([w_in[:, o[0]:o[3]], w_in[:, o[5]:o[8]], w_in[:, o[9]:o[11]]], axis=1).astype(bf16)
    w_gate = jnp.concatenate([w_in[:, o[3]:o[5]], w_in[:, o[8]:o[9]],
                              jnp.zeros((w_in.shape[0], LANES - 2 * NH_M - 3 * H_N), w_in.dtype)], axis=1).astype(bf16)
    return w_main, w_gate


def kernel(x_prompt, x_sample, cache_kv, state_win_kv, state_conv, state_C, state_n, state_m, page_table,
           norm_mix, norm_ffn, norm_final, w_in, w_conv, b_conv, w_bd, b_gates, w_hnorm, w_skip,
           cmp_pe, cmp_w1, cmp_w2, w_br_m, w_br_n, w_out, w_pq, peer_keys, peer_u, peer_v):
    assert w_in.shape[0] == DEPTH == 1
    f32 = jnp.float32
    l = 0
    Bp, Lp_, _ = x_prompt.shape
    Bs, Ls, _ = x_sample.shape
    past = page_table.shape[1] * cache_kv.shape[2]
    w_main, w_gate = _in_proj_weights(w_in[l])
    wbd_full = expand_blockdiag(w_bd[l])
    gate_bias = jnp.concatenate([b_gates[l, 0], b_gates[l, 1], jnp.zeros((LANES - 2 * NH_M,), f32)])[None]
    w_br_m_bf, w_br_n_bf, w_out_bf = w_br_m[l].astype(bf16), w_br_n[l].astype(bf16), w_out[l].astype(bf16)
    w_pq_t = w_pq[l].T.astype(bf16)
    u_bf = peer_u[l].astype(bf16)
    vt_bf = peer_v[l].T.astype(bf16)

    def mixers(x, conv_prev, C0, n0, m0, nsa_fn, ch, lpad):
        B, L, _ = x.shape
        n = B * L
        x2 = x.reshape(n, D_MODEL)
        xn = rmsnorm_rows(x2, norm_mix[l], bf16)
        z2 = pmm(xn, w_main)
        zg2 = pmm(xn, w_gate)
        z3 = z2.reshape(B, L, Z_COLS)
        zg3 = zg2.reshape(B, L, LANES)
        xm = z3[:, :, Z_XM:Z_XM + D_M]
        new_conv = jnp.concatenate([conv_prev, xm], axis=1)[:, L:]
        prev8 = jnp.pad(conv_prev, ((0, 0), (PREV_ROWS - (CONV_W - 1), 0), (0, 0)))
        if lpad == L:
            zm3, zgm3, cols = z3, zg3, (Z_XM // DH_M, Z_VM // DH_M, Z_OM // DH_M)
        else:
            zm3 = jnp.pad(z3[:, :, :Z_Q], ((0, 0), (0, lpad - L), (0, 0)))
            zgm3 = jnp.pad(zg3, ((0, 0), (0, lpad - L), (0, 0)))
            cols = (Z_XM // DH_M, Z_VM // DH_M, Z_OM // DH_M)
        y_m, C1, n1, m1 = mlstm_branch(zm3, zgm3, cols, prev8, w_conv[l], b_conv[l], wbd_full, gate_bias,
                                       C0, n0[:, :, None, :], jnp.broadcast_to(m0[:, :, None, None], (B, NH_M, 1, LANES)),
                                       w_hnorm[l], w_skip[l], L, ch)
        y_m = y_m[:, :L].reshape(n, D_M)
        kv_new = z3[:, :, Z_KV:Z_KV + N_KV_SETS * KVW].reshape(B, L, N_KV_SETS, G_N, DH_N)
        win_new = z3[:, :, Z_WIN:Z_WIN + 2 * KVW].reshape(B, L, 2, G_N, DH_N)
        y_n, new_win = nsa_fn(z3, zg3, kv_new, win_new)
        merged = merge_branches(y_m, y_n.reshape(n, D_N).astype(bf16), w_br_m_bf, w_br_n_bf, z2)
        h2 = out_proj_residual(merged, w_out_bf, x2)
        return h2, kv_new, new_win, new_conv, C1, n1[:, :, 0], m1[:, :, 0, 0]

    def nsa_p(z3, zg3, kv_new, win_new):
        kcv = compress_prompt(z3, Z_KV // DH_N, cmp_pe[l], cmp_w1[l], cmp_w2[l])
        return nsa_prompt(z3, zg3, kcv), win_new[:, -min(WINDOW, z3.shape[1]):]

    def nsa_s(z3, zg3, kv_new, win_new):
        kv_past = [cache_kv[l][page_table, :, s].reshape(Bs, past, G_N, DH_N) for s in range(N_KV_SETS)]
        return nsa_sample(z3[:, :, Z_Q:Z_Q + D_N], kv_new, win_new, zg3[:, :, ZG_GN:ZG_GN + 3 * H_N],
                          kv_past, state_win_kv[l], cmp_pe[l], cmp_w1[l], cmp_w2[l])

    hp, kvp, winp, convp, Cp, n_p, m_p = mixers(
        x_prompt, jnp.zeros((Bp, CONV_W - 1, D_M), f32), jnp.zeros((Bp, NH_M, DH_M, DH_M), f32),
        jnp.zeros((Bp, NH_M, DH_M), f32), jnp.zeros((Bp, NH_M), f32), nsa_p, 256, Lp_)
    hs, kvs, wins, convs, Cs, n_s, m_s = mixers(
        x_sample, state_conv[l], state_C[l], state_n[l], state_m[l], nsa_s, 16, 16)

    def ffn_and_norm(h2):
        xn2 = rmsnorm_rows(h2, norm_ffn[l], bf16)
        peer_t = peer_ffn_t(xn2, w_pq_t, peer_keys[l], u_bf, vt_bf)
        return final_norm(h2, peer_t, norm_final)

    y_prompt = ffn_and_norm(hp).reshape(x_prompt.shape)
    y_sample = ffn_and_norm(hs).reshape(x_sample.shape)
    st = lambda t: t[None]
    return (y_prompt, y_sample, st(kvp), st(kvs), st(winp), st(wins), st(convp), st(convs),
            st(Cp), st(Cs), st(n_p), st(n_s), st(m_p), st(m_s))
```

```python
import functools

import jax
import jax.numpy as jnp
import numpy as np
from jax import lax
from jax.experimental import pallas as pl
from jax.experimental.pallas import tpu as pltpu

D_MODEL = 4096
DEPTH = 1
D_M = D_MODEL // 2
DH_M = 256
NH_M = D_M // DH_M
CONV_W = 4
QKV_BLOCK = 4
D_N = D_MODEL // 2
DH_N = 128
H_N = D_N // DH_N
G_N = 4
J_N = H_N // G_N
KVW = G_N * DH_N
N_KV_SETS = 4
CMP_STRIDE = 16
CMP_LEN = 2 * CMP_STRIDE
CMP_HID = 2 * DH_N
SEL_LEN = 64
N_SEL = 16
WINDOW = 512
SCALE_N = DH_N ** -0.5
PEER_HEADS = 8
N_KEYS = 128
PEER_TOPK = 16
PEER_QDIM = 256
PEER_QHALF = PEER_QDIM // 2
IN_SIZES = (D_M, D_M, D_M, NH_M, NH_M, D_N, N_KV_SETS * KVW, 2 * KVW, 3 * H_N, D_MODEL, D_MODEL)
EPS = 1e-6
NEG = -1e30
BIG = 1e9
NEG_INF = float('-inf')

LANES = 128
VMEM_LIMIT_BYTES = 48 * 1024 * 1024
BIG_VMEM_LIMIT_BYTES = 56 * 1024 * 1024
PEER_SUB = 4
PREV_ROWS = 8
PAGES_PER_STEP = 16
NEW_ROWS = 16
QROWS = 8
SEL_CHUNK = 512
bf16 = jnp.bfloat16

_IN_OFFS = np.concatenate([[0], np.cumsum(IN_SIZES)]).tolist()
Z_XM, Z_VM, Z_OM = 0, D_M, 2 * D_M
Z_Q = 3 * D_M
Z_KV = Z_Q + D_N
Z_WIN = Z_KV + N_KV_SETS * KVW
Z_GM = Z_WIN + 2 * KVW
Z_GN = Z_GM + D_MODEL
Z_COLS = Z_GN + D_MODEL
ZG_IG, ZG_FG, ZG_GN = 0, NH_M, 2 * NH_M


def _pick(n, cands):
    for c in cands:
        if n % c == 0:
            return c
    return n


def _gelu_tanh(x):
    return 0.5 * x * (1.0 + jnp.tanh(np.float32(np.sqrt(2.0 / np.pi)) * (x + np.float32(0.044715) * (x * x * x))))


def _split3(x):
    hi = x.astype(bf16)
    r = x - hi.astype(jnp.float32)
    mid = r.astype(bf16)
    lo = (r - mid.astype(jnp.float32)).astype(bf16)
    return hi, mid, lo


def _rmsnorm_kernel(x_ref, g_ref, o_ref):
    x = x_ref[...]
    o_ref[...] = (x * lax.rsqrt(jnp.mean(x * x, axis=-1, keepdims=True) + EPS) * g_ref[...]).astype(o_ref.dtype)


def rmsnorm_rows(x2, g, out_dtype):
    n, d = x2.shape
    tr = _pick(n, (256, 128))
    return pl.pallas_call(
        _rmsnorm_kernel,
        grid=(n // tr,),
        in_specs=[pl.BlockSpec((tr, d), lambda i: (i, 0)), pl.BlockSpec((1, d), lambda i: (0, 0))],
        out_specs=pl.BlockSpec((tr, d), lambda i: (i, 0)),
        out_shape=jax.ShapeDtypeStruct((n, d), out_dtype),
        compiler_params=pltpu.CompilerParams(dimension_semantics=("parallel",), vmem_limit_bytes=VMEM_LIMIT_BYTES),
        name="rmsnorm",
    )(x2, g.reshape(1, d))


def _mm_kernel(a_ref, b_ref, o_ref):
    o_ref[...] = jnp.dot(a_ref[...], b_ref[...], preferred_element_type=jnp.float32).astype(o_ref.dtype)


def pmm(a, b, out_dtype=jnp.float32):
    m, kd = a.shape
    n = b.shape[1]
    tm = _pick(m, (1024, 512, 256, 128))
    tn = _pick(n, (1024, 512, 256, 128))
    return pl.pallas_call(
        _mm_kernel,
        grid=(m // tm, n // tn),
        in_specs=[pl.BlockSpec((tm, kd), lambda i, j: (i, 0)), pl.BlockSpec((kd, tn), lambda i, j: (0, j))],
        out_specs=pl.BlockSpec((tm, tn), lambda i, j: (i, j)),
        out_shape=jax.ShapeDtypeStruct((m, n), out_dtype),
        compiler_params=pltpu.CompilerParams(dimension_semantics=("parallel", "parallel"),
                                             vmem_limit_bytes=BIG_VMEM_LIMIT_BYTES),
        name="proj",
    )(a, b)


def _merge_kernel(ym_ref, yn_ref, wm_ref, wn_ref, gm_ref, gn_ref, o_ref):
    pm = jnp.dot(ym_ref[...], wm_ref[...], preferred_element_type=jnp.float32)
    pn = jnp.dot(yn_ref[...], wn_ref[...], preferred_element_type=jnp.float32)
    o_ref[...] = (jax.nn.sigmoid(gm_ref[...]) * pm + jax.nn.sigmoid(gn_ref[...]) * pn).astype(o_ref.dtype)


def merge_branches(ym, yn, wm, wn, z2):
    m = ym.shape[0]
    n = wm.shape[1]
    tm = _pick(m, (1024, 512, 256, 128))
    tn = 512
    return pl.pallas_call(
        _merge_kernel,
        grid=(m // tm, n // tn),
        in_specs=[pl.BlockSpec((tm, D_M), lambda i, j: (i, 0)), pl.BlockSpec((tm, D_N), lambda i, j: (i, 0)),
                  pl.BlockSpec((D_M, tn), lambda i, j: (0, j)), pl.BlockSpec((D_N, tn), lambda i, j: (0, j)),
                  pl.BlockSpec((tm, tn), lambda i, j: (i, Z_GM // tn + j)),
                  pl.BlockSpec((tm, tn), lambda i, j: (i, Z_GN // tn + j))],
        out_specs=pl.BlockSpec((tm, tn), lambda i, j: (i, j)),
        out_shape=jax.ShapeDtypeStruct((m, n), bf16),
        compiler_params=pltpu.CompilerParams(dimension_semantics=("parallel", "parallel"),
                                             vmem_limit_bytes=VMEM_LIMIT_BYTES),
        name="merge_branches",
    )(ym, yn, wm, wn, z2, z2)


def _mm_res_kernel(a_ref, b_ref, r_ref, o_ref):
    o_ref[...] = r_ref[...] + jnp.dot(a_ref[...], b_ref[...], preferred_element_type=jnp.float32)


def out_proj_residual(a, b, r):
    m, kd = a.shape
    n = b.shape[1]
    tm = _pick(m, (1024, 512, 256, 128))
    tn = 512
    return pl.pallas_call(
        _mm_res_kernel,
        grid=(m // tm, n // tn),
        in_specs=[pl.BlockSpec((tm, kd), lambda i, j: (i, 0)), pl.BlockSpec((kd, tn), lambda i, j: (0, j)),
                  pl.BlockSpec((tm, tn), lambda i, j: (i, j))],
        out_specs=pl.BlockSpec((tm, tn), lambda i, j: (i, j)),
        out_shape=jax.ShapeDtypeStruct((m, n), jnp.float32),
        compiler_params=pltpu.CompilerParams(dimension_semantics=("parallel", "parallel"),
                                             vmem_limit_bytes=VMEM_LIMIT_BYTES),
        name="out_proj",
    )(a, b, r)


def _final_kernel(h_ref, pt_ref, g_ref, o_ref):
    x = h_ref[...] + pt_ref[...].T
    o_ref[...] = x * lax.rsqrt(jnp.mean(x * x, axis=-1, keepdims=True) + EPS) * g_ref[...]


def final_norm(h2, peer_t, g):
    n, d = h2.shape
    tr = _pick(n, (256, 128))
    return pl.pallas_call(
        _final_kernel,
        grid=(n // tr,),
        in_specs=[pl.BlockSpec((tr, d), lambda i: (i, 0)), pl.BlockSpec((d, tr), lambda i: (0, i)),
                  pl.BlockSpec((1, d), lambda i: (0, 0))],
        out_specs=pl.BlockSpec((tr, d), lambda i: (i, 0)),
        out_shape=jax.ShapeDtypeStruct((n, d), jnp.float32),
        compiler_params=pltpu.CompilerParams(dimension_semantics=("parallel",), vmem_limit_bytes=VMEM_LIMIT_BYTES),
        name="final_norm",
    )(h2, peer_t, g.reshape(1, d))


def _blockdiag_apply(x, w_ref, which):
    parts = [jnp.dot(x[:, hb * LANES:(hb + 1) * LANES].astype(bf16), w_ref[which, hb].astype(bf16),
                     preferred_element_type=jnp.float32) for hb in range(DH_M // LANES)]
    return jnp.concatenate(parts, axis=1)


def _mlstm_kernel(xm_ref, vm_ref, om_ref, zg_ref, prev_ref, wc_ref, bc_ref, wbd_ref, gb_ref, c0_ref, n0_ref, m0_ref,
                  hn_ref, sk_ref, y_ref, c1_ref, n1_ref, m1_ref, *, ch, l_true):
    h = pl.program_id(1)
    lp = xm_ref.shape[0]
    nchunk = lp // ch
    f32 = jnp.float32
    tri = lax.broadcasted_iota(jnp.int32, (ch, ch), 0) >= lax.broadcasted_iota(jnp.int32, (ch, ch), 1)
    tri_bf = tri.astype(bf16)
    lane_g = lax.broadcasted_iota(jnp.int32, (ch, LANES), 1)
    sub_g = lax.broadcasted_iota(jnp.int32, (LANES, ch), 0)

    def chunk(ci, carry):
        c_st, n_st, m_st = carry
        r0 = pl.multiple_of(ci * ch, ch)
        xm = xm_ref[pl.ds(r0, ch), :]
        before = xm_ref[pl.ds(pl.multiple_of(jnp.maximum(r0 - PREV_ROWS, 0), PREV_ROWS), PREV_ROWS), :]
        before = jnp.where(ci == 0, prev_ref[...], before)
        xe = jnp.concatenate([before, xm], axis=0)
        lo = PREV_ROWS - (CONV_W - 1)
        conv = bc_ref[...] + sum(xe[lo + w:lo + w + ch, :] * wc_ref[w:w + 1, :] for w in range(CONV_W))
        c = conv * jax.nn.sigmoid(conv)
        q = _blockdiag_apply(c, wbd_ref, 0)
        k = _blockdiag_apply(c, wbd_ref, 1) * (DH_M ** -0.5)
        v = _blockdiag_apply(vm_ref[pl.ds(r0, ch), :], wbd_ref, 2)
        zg = zg_ref[pl.ds(r0, ch), :] + gb_ref[...]
        live = (r0 + lax.broadcasted_iota(jnp.int32, (ch, LANES), 0)) < l_true
        a_all = jnp.where(live, zg, NEG_INF)
        lf_all = jnp.where(live, jnp.minimum(zg, 0.0) - jnp.log1p(jnp.exp(-jnp.abs(zg))), 0.0)
        b_all = jnp.zeros((ch, LANES), f32)
        for part in _split3(lf_all):
            b_all = b_all + jnp.dot(tri_bf, part, preferred_element_type=f32)
        a_col = jnp.sum(jnp.where(lane_g == ZG_IG + h, a_all, 0.0), axis=1, keepdims=True)
        b_col = jnp.sum(jnp.where(lane_g == ZG_FG + h, b_all, 0.0), axis=1, keepdims=True)
        a_row = jnp.sum(jnp.where(sub_g == ZG_IG + h, a_all.T, 0.0), axis=0, keepdims=True)
        b_row = jnp.sum(jnp.where(sub_g == ZG_FG + h, b_all.T, 0.0), axis=0, keepdims=True)
        dmat = jnp.where(tri, b_col - b_row + a_row, NEG_INF)
        inter = b_col + m_st
        m = jnp.maximum(inter, jnp.max(dmat, axis=1, keepdims=True))
        w_intra = jnp.exp(dmat - m)
        w_inter = jnp.exp(inter - m)
        qb = q.astype(bf16)
        kb = k.astype(bf16)
        vb = v.astype(bf16)
        qk = lax.dot_general(qb, kb, (((1,), (1,)), ((), ())), preferred_element_type=f32) * w_intra
        num = w_inter * jnp.dot(qb, c_st.astype(bf16), preferred_element_type=f32) \
            + jnp.dot(qk.astype(bf16), vb, preferred_element_type=f32)
        den = w_inter * jnp.sum(q * n_st, axis=1, keepdims=True) + jnp.sum(qk, axis=1, keepdims=True)
        hh = num / jnp.maximum(jnp.abs(den), jnp.exp(-m))
        m_end = m[ch - 1:ch, :]
        b_end = b_col[ch - 1:ch, :]
        w_end = jnp.exp(b_end - b_col + a_col - m_end)
        decay = jnp.exp(b_end + m_st - m_end)
        kw = k * w_end
        c_new = decay * c_st + lax.dot_general(kw.astype(bf16), vb, (((0,), (0,)), ((), ())),
                                               preferred_element_type=f32)
        n_new = decay * n_st + jnp.sum(kw, axis=0, keepdims=True)
        hg = hh * jax.nn.sigmoid(om_ref[pl.ds(r0, ch), :])
        mu = jnp.mean(hg, axis=1, keepdims=True)
        var = jnp.mean(jnp.square(hg - mu), axis=1, keepdims=True)
        y = (hg - mu) * lax.rsqrt(var + EPS) * hn_ref[...] + sk_ref[...] * c
        y_ref[pl.ds(r0, ch), :] = y.astype(y_ref.dtype)
        return c_new, n_new, m_end

    c_fin, n_fin, m_fin = lax.fori_loop(0, nchunk, chunk, (c0_ref[...], n0_ref[...], m0_ref[:, :1]))
    c1_ref[...] = c_fin
    n1_ref[...] = n_fin
    m1_ref[...] = jnp.broadcast_to(m_fin, m1_ref.shape)


def expand_blockdiag(w_bd):
    per = LANES // QKV_BLOCK
    w = w_bd.reshape(3, D_M // LANES, per, QKV_BLOCK, QKV_BLOCK)
    full = jnp.einsum('tbnio,nm->tbnimo', w, jnp.eye(per, dtype=w_bd.dtype))
    return full.reshape(3, D_M // LANES, LANES, LANES)


def mlstm_branch(z3, zg3, col_blocks, prev8, w_conv, b_conv, wbd_full, gate_bias, c0, n0, m0, w_hnorm, w_skip,
                 l_true, ch):
    b, lp, _ = z3.shape
    cx, cv, co = col_blocks
    kern = functools.partial(_mlstm_kernel, ch=ch, l_true=l_true)
    col = lambda c0_: pl.BlockSpec((None, lp, DH_M), lambda bi, h: (bi, 0, c0_ + h))
    vec = pl.BlockSpec((1, DH_M), lambda bi, h: (0, h))
    st = lambda r, c: pl.BlockSpec((None, None, r, c), lambda bi, h: (bi, h, 0, 0))
    return pl.pallas_call(
        kern,
        grid=(b, NH_M),
        in_specs=[col(cx), col(cv), col(co),
                  pl.BlockSpec((None, lp, LANES), lambda bi, h: (bi, 0, 0)),
                  pl.BlockSpec((None, PREV_ROWS, DH_M), lambda bi, h: (bi, 0, h)),
                  pl.BlockSpec((CONV_W, DH_M), lambda bi, h: (0, h)),
                  vec,
                  pl.BlockSpec((3, DH_M // LANES, LANES, LANES), lambda bi, h: (0, h, 0, 0)),
                  pl.BlockSpec((1, LANES), lambda bi, h: (0, 0)),
                  st(DH_M, DH_M), st(1, DH_M), st(1, LANES), vec, vec],
        out_specs=[pl.BlockSpec((None, lp, DH_M), lambda bi, h: (bi, 0, h)),
                   st(DH_M, DH_M), st(1, DH_M), st(1, LANES)],
        out_shape=[jax.ShapeDtypeStruct((b, lp, D_M), bf16),
                   jax.ShapeDtypeStruct((b, NH_M, DH_M, DH_M), jnp.float32),
                   jax.ShapeDtypeStruct((b, NH_M, 1, DH_M), jnp.float32),
                   jax.ShapeDtypeStruct((b, NH_M, 1, LANES), jnp.float32)],
        compiler_params=pltpu.CompilerParams(dimension_semantics=("parallel", "parallel"),
                                             vmem_limit_bytes=VMEM_LIMIT_BYTES),
        name="mlstm_branch",
    )(z3, z3, z3, zg3, prev8, w_conv, b_conv.reshape(1, D_M), wbd_full, gate_bias, c0, n0, m0,
      w_hnorm.reshape(1, D_M), w_skip.reshape(1, D_M))


def _compress_kernel(x_ref, pe_ref, w1_ref, w2_ref, o_ref):
    nchunk = x_ref.shape[0] // CMP_STRIDE
    first = jnp.zeros((nchunk, CMP_HID), jnp.float32)
    second = jnp.zeros((nchunk, CMP_HID), jnp.float32)
    for l in range(CMP_STRIDE):
        xl = x_ref[pl.ds(l, nchunk, stride=CMP_STRIDE), :]
        first += jnp.dot((xl + pe_ref[l:l + 1, :]).astype(bf16), w1_ref[l].astype(bf16),
                         preferred_element_type=jnp.float32)
        second += jnp.dot((xl + pe_ref[CMP_STRIDE + l:CMP_STRIDE + l + 1, :]).astype(bf16),
                          w1_ref[CMP_STRIDE + l].astype(bf16), preferred_element_type=jnp.float32)
    nxt = jnp.concatenate([second[1:], jnp.zeros((1, CMP_HID), jnp.float32)], axis=0)
    row = lax.broadcasted_iota(jnp.int32, (nchunk, CMP_HID), 0)
    hid = jnp.where(row < nchunk - 1, _gelu_tanh(first + nxt), 0.0)
    o_ref[...] = jnp.dot(hid.astype(bf16), w2_ref[...].astype(bf16), preferred_element_type=jnp.float32)


def compress_prompt(z3, col0, cmp_pe, cmp_w1, cmp_w2):
    b, l, _ = z3.shape
    nchunk = l // CMP_STRIDE
    return pl.pallas_call(
        _compress_kernel,
        grid=(b, 2, G_N),
        in_specs=[pl.BlockSpec((None, l, DH_N), lambda bi, s, g: (bi, 0, col0 + s * G_N + g)),
                  pl.BlockSpec((None, CMP_LEN, DH_N), lambda bi, s, g: (s, 0, 0)),
                  pl.BlockSpec((None, CMP_LEN, DH_N, CMP_HID), lambda bi, s, g: (s, 0, 0, 0)),
                  pl.BlockSpec((None, CMP_HID, DH_N), lambda bi, s, g: (s, 0, 0))],
        out_specs=pl.BlockSpec((None, None, None, nchunk, DH_N), lambda bi, s, g: (bi, s, g, 0, 0)),
        out_shape=jax.ShapeDtypeStruct((b, 2, G_N, nchunk, DH_N), jnp.float32),
        compiler_params=pltpu.CompilerParams(dimension_semantics=("parallel", "parallel", "parallel"),
                                             vmem_limit_bytes=VMEM_LIMIT_BYTES),
        name="nsa_compress",
    )(z3, cmp_pe, cmp_w1, cmp_w2)


def _softmax_pv(s, keep, v):
    s = jnp.where(keep, s, NEG)
    m = jnp.max(s, axis=-1, keepdims=True)
    e = jnp.exp(s - m)
    den = jnp.sum(e, axis=-1, keepdims=True)
    return jnp.dot(e.astype(bf16), v, preferred_element_type=jnp.float32) / den


def _nsa_prompt_kernel(q_ref, kc_ref, vc_ref, ks_ref, vs_ref, kw_ref, vw_ref, gn_ref, o_ref, *, tq):
    g = pl.program_id(1)
    qi = pl.program_id(2)
    f32 = jnp.float32
    t = ks_ref.shape[0]
    n_c = kc_ref.shape[0]
    n_s = t // SEL_LEN
    nq = J_N * tq
    q4 = jnp.concatenate([q_ref[:, j * DH_N:(j + 1) * DH_N] for j in range(J_N)], axis=0).astype(bf16)
    nt = (((1,), (1,)), ((), ()))

    def qpos_of(shape):
        return qi * tq + lax.broadcasted_iota(jnp.int32, shape, 0) % tq

    cidx = lax.broadcasted_iota(jnp.int32, (nq, n_c), 1)
    cmask = cidx * CMP_STRIDE + CMP_LEN <= qpos_of((nq, n_c)) + 1
    s = lax.dot_general(q4, kc_ref[...].astype(bf16), nt, preferred_element_type=f32) * SCALE_N
    s = jnp.where(cmask, s, NEG)
    m = jnp.max(s, axis=-1, keepdims=True)
    e = jnp.where(cmask, jnp.exp(s - m), 0.0)
    den = jnp.sum(e, axis=-1, keepdims=True)
    p = e / jnp.where(den > 0.0, den, 1.0)
    o_cmp = jnp.dot(p.astype(bf16), vc_ref[...].astype(bf16), preferred_element_type=f32)
    p_sum = p[0:tq]
    for j in range(1, J_N):
        p_sum = p_sum + p[j * tq:(j + 1) * tq]
    ci = lax.broadcasted_iota(jnp.int32, (n_c, LANES), 0)
    si = lax.broadcasted_iota(jnp.int32, (n_c, LANES), 1)
    cover = jnp.logical_and(ci * CMP_STRIDE < si * SEL_LEN + SEL_LEN,
                            ci * CMP_STRIDE + CMP_LEN > si * SEL_LEN).astype(bf16)
    imp = jnp.zeros((tq, LANES), f32)
    for part in _split3(p_sum):
        imp = imp + jnp.dot(part, cover, preferred_element_type=f32)
    blk = lax.broadcasted_iota(jnp.int32, (tq, LANES), 1)
    cur = (qi * tq + lax.broadcasted_iota(jnp.int32, (tq, LANES), 0)) // SEL_LEN
    valid = blk <= cur
    forced = jnp.logical_or(blk == 0, blk >= cur - 1)
    score = jnp.where(valid, imp + jnp.where(forced, BIG, 0.0), -BIG)
    score = jnp.where(blk < n_s, score, NEG_INF)
    rank = jnp.zeros((tq, LANES), jnp.int32)
    for s2 in range(n_s):
        col = score[:, s2:s2 + 1]
        ahead = jnp.logical_or(col > score, jnp.logical_and(col == score, blk > s2))
        rank = rank + ahead.astype(jnp.int32)
    sel = (rank < N_SEL).astype(bf16)

    ck = SEL_CHUNK
    s_row = lax.broadcasted_iota(jnp.int32, (LANES, ck), 0)
    k_lane = lax.broadcasted_iota(jnp.int32, (LANES, ck), 1)
    kpos1 = lax.broadcasted_iota(jnp.int32, (tq, ck), 1)
    qpos1 = qi * tq + lax.broadcasted_iota(jnp.int32, (tq, ck), 0)

    def kchunk(kb, carry):
        m_o, l_o, acc = carry
        k0 = pl.multiple_of(kb * ck, ck)
        expand = ((k0 + k_lane) // SEL_LEN == s_row).astype(bf16)
        picked = jnp.dot(sel, expand, preferred_element_type=f32)
        keep1 = jnp.logical_and(picked > 0.5, k0 + kpos1 <= qpos1)
        keep = jnp.concatenate([keep1] * J_N, axis=0)
        kk = ks_ref[pl.ds(k0, ck), :].astype(bf16)
        vv = vs_ref[pl.ds(k0, ck), :].astype(bf16)
        sc = lax.dot_general(q4, kk, nt, preferred_element_type=f32) * SCALE_N
        sc = jnp.where(keep, sc, NEG)
        m_n = jnp.maximum(m_o, jnp.max(sc, axis=-1, keepdims=True))
        alpha = jnp.exp(m_o - m_n)
        ee = jnp.where(keep, jnp.exp(sc - m_n), 0.0)
        l_n = alpha * l_o + jnp.sum(ee, axis=-1, keepdims=True)
        return m_n, l_n, alpha * acc + jnp.dot(ee.astype(bf16), vv, preferred_element_type=f32)

    nkc = (qi * tq + tq + ck - 1) // ck
    _, l_f, acc_f = lax.fori_loop(0, nkc, kchunk, (jnp.full((nq, 1), NEG, f32), jnp.zeros((nq, 1), f32),
                                                   jnp.zeros((nq, DH_N), f32)))
    o_sel = acc_f / l_f

    span = WINDOW + tq
    w0 = pl.multiple_of(jnp.clip(qi * tq - WINDOW, 0, t - span), tq)
    kw = kw_ref[pl.ds(w0, span), :].astype(bf16)
    vw = vw_ref[pl.ds(w0, span), :].astype(bf16)
    qpos_w = qpos_of((nq, span))
    kpos_w = w0 + lax.broadcasted_iota(jnp.int32, (nq, span), 1)
    keep_win = jnp.logical_and(kpos_w <= qpos_w, kpos_w > qpos_w - WINDOW)
    sw = lax.dot_general(q4, kw, nt, preferred_element_type=f32) * SCALE_N
    o_win = _softmax_pv(sw, keep_win, vw)

    gates = jax.nn.sigmoid(gn_ref[...])
    gl = lax.broadcasted_iota(jnp.int32, gates.shape, 1)
    outs = []
    for j in range(J_N):
        acc = jnp.zeros((tq, DH_N), f32)
        for which, o in enumerate((o_cmp, o_sel, o_win)):
            col = jnp.sum(jnp.where(gl == ZG_GN + (g * J_N + j) * 3 + which, gates, 0.0), axis=-1, keepdims=True)
            acc = acc + col * o[j * tq:(j + 1) * tq]
        outs.append(acc)
    o_ref[...] = jnp.concatenate(outs, axis=1).astype(o_ref.dtype)


def nsa_prompt(z3, zg3, kcv, tq=128):
    b, l, _ = z3.shape
    assert l >= WINDOW + tq and l % tq == 0 and l % SEL_CHUNK == 0
    nq = l // tq
    n_c = kcv.shape[3]
    full = lambda c0: pl.BlockSpec((None, l, DH_N), lambda bi, g, qi: (bi, 0, c0 // DH_N + g))
    return pl.pallas_call(
        functools.partial(_nsa_prompt_kernel, tq=tq),
        grid=(b, G_N, nq),
        in_specs=[pl.BlockSpec((None, tq, J_N * DH_N), lambda bi, g, qi: (bi, qi, Z_Q // (J_N * DH_N) + g)),
                  pl.BlockSpec((None, None, None, n_c, DH_N), lambda bi, g, qi: (bi, 0, g, 0, 0)),
                  pl.BlockSpec((None, None, None, n_c, DH_N), lambda bi, g, qi: (bi, 1, g, 0, 0)),
                  full(Z_KV + 2 * KVW), full(Z_KV + 3 * KVW), full(Z_WIN), full(Z_WIN + KVW),
                  pl.BlockSpec((None, tq, LANES), lambda bi, g, qi: (bi, qi, 0))],
        out_specs=pl.BlockSpec((None, tq, J_N * DH_N), lambda bi, g, qi: (bi, qi, g)),
        out_shape=jax.ShapeDtypeStruct((b, l, D_N), bf16),
        compiler_params=pltpu.CompilerParams(dimension_semantics=("parallel", "parallel", "arbitrary"),
                                             vmem_limit_bytes=VMEM_LIMIT_BYTES),
        name="nsa_prompt",
    )(z3, kcv, kcv, z3, z3, z3, z3, zg3)


def _paged_compress_kernel(pt_ref, *refs, page, nstep):
    pages = refs[:PAGES_PER_STEP]
    new_ref, pe_ref, w1_ref, w2_ref, o_ref, carry_ref, rows_ref = refs[PAGES_PER_STEP:]
    p = pl.program_id(1)
    nch = PAGES_PER_STEP * page // CMP_STRIDE
    row = lax.broadcasted_iota(jnp.int32, (nch, CMP_HID), 0)

    @pl.when(p < nstep)
    def _():
        for r in range(PAGES_PER_STEP):
            for sg in range(2 * G_N):
                rows_ref[sg, r * page:(r + 1) * page, :] = pages[r][:, sg * DH_N:(sg + 1) * DH_N]
        for s in range(2):
            first = jnp.zeros((G_N * nch, CMP_HID), jnp.float32)
            second = jnp.zeros((G_N * nch, CMP_HID), jnp.float32)
            for l in range(CMP_STRIDE):
                xl = jnp.concatenate([rows_ref[s * G_N + g, pl.ds(l, nch, stride=CMP_STRIDE), :] for g in range(G_N)],
                                     axis=0)
                first += jnp.dot((xl + pe_ref[s, l:l + 1, :]).astype(bf16), w1_ref[s, l],
                                 preferred_element_type=jnp.float32)
                second += jnp.dot((xl + pe_ref[s, CMP_STRIDE + l:CMP_STRIDE + l + 1, :]).astype(bf16),
                                  w1_ref[s, CMP_STRIDE + l], preferred_element_type=jnp.float32)
            for g in range(G_N):
                f_g = first[g * nch:(g + 1) * nch]
                s_g = second[g * nch:(g + 1) * nch]
                prev = carry_ref[s * G_N + g, 0:1, :]
                shifted = jnp.where(row == 0, prev, pltpu.roll(f_g, 1, axis=0))
                carry_ref[s * G_N + g, 0:1, :] = f_g[nch - 1:nch, :]
                hid = _gelu_tanh(shifted + s_g)
                hid = jnp.where(jnp.logical_and(row == 0, p == 0), 0.0, hid)
                o_ref[s, g] = jnp.dot(hid.astype(bf16), w2_ref[s].astype(bf16), preferred_element_type=jnp.float32)

    @pl.when(p == nstep)
    def _():
        for s in range(2):
            for g in range(G_N):
                c0 = (s * G_N + g) * DH_N
                sec = jnp.zeros((QROWS, CMP_HID), jnp.float32)
                for l in range(CMP_STRIDE):
                    x = jnp.broadcast_to(new_ref[l:l + 1, c0:c0 + DH_N] + pe_ref[s, CMP_STRIDE + l:CMP_STRIDE + l + 1, :],
                                         (QROWS, DH_N))
                    sec += jnp.dot(x.astype(bf16), w1_ref[s, CMP_STRIDE + l], preferred_element_type=jnp.float32)
                hid = _gelu_tanh(carry_ref[s * G_N + g, 0:1, :] + sec)
                blk = jnp.dot(hid.astype(bf16), w2_ref[s].astype(bf16), preferred_element_type=jnp.float32)
                rows = lax.broadcasted_iota(jnp.int32, (nch, DH_N), 0)
                o_ref[s, g] = jnp.where(rows == 0, jnp.broadcast_to(blk[0:1, :], (nch, DH_N)), 0.0)


def paged_compress(page_table, cache2, new_c, cmp_pe, w1_bf, cmp_w2):
    b, npages = page_table.shape
    page = cache2.shape[1]
    assert page % CMP_STRIDE == 0 and npages % PAGES_PER_STEP == 0 and new_c.shape[1] == NEW_ROWS == CMP_STRIDE
    nstep = npages // PAGES_PER_STEP
    nch = PAGES_PER_STEP * page // CMP_STRIDE

    def page_spec(r):
        return pl.BlockSpec((None, page, 2 * KVW),
                            lambda bi, p, pt: (pt[bi, jnp.minimum(p * PAGES_PER_STEP + r, npages - 1)], 0, 0))

    grid_spec = pltpu.PrefetchScalarGridSpec(
        num_scalar_prefetch=1,
        grid=(b, nstep + 1),
        in_specs=[page_spec(r) for r in range(PAGES_PER_STEP)] + [
            pl.BlockSpec((None, NEW_ROWS, 2 * KVW), lambda bi, p, pt: (bi, 0, 0)),
            pl.BlockSpec((2, CMP_LEN, DH_N), lambda bi, p, pt: (0, 0, 0)),
            pl.BlockSpec((2, CMP_LEN, DH_N, CMP_HID), lambda bi, p, pt: (0, 0, 0, 0)),
            pl.BlockSpec((2, CMP_HID, DH_N), lambda bi, p, pt: (0, 0, 0))],
        out_specs=pl.BlockSpec((None, 2, G_N, nch, DH_N), lambda bi, p, pt: (bi, 0, 0, p, 0)),
        scratch_shapes=[pltpu.VMEM((2 * G_N, QROWS, CMP_HID), jnp.float32),
                        pltpu.VMEM((2 * G_N, PAGES_PER_STEP * page, DH_N), jnp.float32)])
    return pl.pallas_call(
        functools.partial(_paged_compress_kernel, page=page, nstep=nstep),
        grid_spec=grid_spec,
        out_shape=jax.ShapeDtypeStruct((b, 2, G_N, (nstep + 1) * nch, DH_N), jnp.float32),
        compiler_params=pltpu.CompilerParams(dimension_semantics=("parallel", "arbitrary"),
                                             vmem_limit_bytes=VMEM_LIMIT_BYTES),
        name="nsa_paged_compress",
    )(page_table, *([cache2] * PAGES_PER_STEP), new_c, cmp_pe, w1_bf, cmp_w2)


def _paged_attend_kernel(pt_ref, *refs, page, nstep, l_true):
    pages = refs[:PAGES_PER_STEP]
    (q_ref, kcv_ref, new_ref, winp_ref, wnew_ref, zg_ref, o_ref,
     sel_ref, ocmp_ref, m_ref, l_ref, acc_ref) = refs[PAGES_PER_STEP:]
    p = pl.program_id(1)
    past = nstep * PAGES_PER_STEP * page
    tk = PAGES_PER_STEP * page
    n_r = kcv_ref.shape[2]
    n_s = -(-(past + l_true) // SEL_LEN)
    sl = 2 * LANES
    assert n_s <= sl
    nq = J_N * QROWS
    f32 = jnp.float32
    nt = (((1,), (1,)), ((), ()))

    def q_of(g):
        return jnp.concatenate([q_ref[:, (g * J_N + j) * DH_N:(g * J_N + j + 1) * DH_N] for j in range(J_N)],
                               axis=0).astype(bf16)

    def pos_of(shape):
        return lax.broadcasted_iota(jnp.int32, shape, 0) % QROWS

    def online(g, s, keep, v):
        s = jnp.where(keep, s, NEG)
        m_old = m_ref[g]
        m_new = jnp.maximum(m_old, jnp.max(s, axis=-1, keepdims=True))
        alpha = jnp.exp(m_old - m_new)
        e = jnp.where(keep, jnp.exp(s - m_new), 0.0)
        l_ref[g] = alpha * l_ref[g] + jnp.sum(e, axis=-1, keepdims=True)
        acc_ref[g] = alpha * acc_ref[g] + jnp.dot(e.astype(bf16), v, preferred_element_type=f32)
        m_ref[g] = m_new

    @pl.when(p == 0)
    def _():
        r_i = lax.broadcasted_iota(jnp.int32, (nq, n_r), 1)
        qpos = past + pos_of((nq, n_r))
        cmask = jnp.logical_and(r_i >= 1, (r_i - 1) * CMP_STRIDE + CMP_LEN <= qpos + 1)
        ci = lax.broadcasted_iota(jnp.int32, (n_r, sl), 0) - 1
        si = lax.broadcasted_iota(jnp.int32, (n_r, sl), 1)
        cover = jnp.logical_and(jnp.logical_and(ci >= 0, ci * CMP_STRIDE < si * SEL_LEN + SEL_LEN),
                                ci * CMP_STRIDE + CMP_LEN > si * SEL_LEN).astype(bf16)
        blk = lax.broadcasted_iota(jnp.int32, (QROWS, sl), 1)
        cur = (past + lax.broadcasted_iota(jnp.int32, (QROWS, sl), 0)) // SEL_LEN
        valid = blk <= cur
        forced = jnp.logical_or(blk == 0, blk >= cur - 1)
        s_src = lax.broadcasted_iota(jnp.int32, (sl, sl), 0)
        s_dst = lax.broadcasted_iota(jnp.int32, (sl, sl), 1)
        for g in range(G_N):
            q = q_of(g)
            kc = kcv_ref[0, g].astype(bf16)
            vc = kcv_ref[1, g].astype(bf16)
            s = lax.dot_general(q, kc, nt, preferred_element_type=f32) * SCALE_N
            s = jnp.where(cmask, s, NEG)
            m = jnp.max(s, axis=-1, keepdims=True)
            e = jnp.where(cmask, jnp.exp(s - m), 0.0)
            den = jnp.sum(e, axis=-1, keepdims=True)
            pr = e / jnp.where(den > 0.0, den, 1.0)
            ocmp_ref[g] = jnp.dot(pr.astype(bf16), vc, preferred_element_type=f32)
            p_sum = pr[0:QROWS]
            for j in range(1, J_N):
                p_sum = p_sum + pr[j * QROWS:(j + 1) * QROWS]
            imp = jnp.zeros((QROWS, sl), f32)
            for part in _split3(p_sum):
                imp = imp + jnp.dot(part, cover, preferred_element_type=f32)
            score = jnp.where(valid, imp + jnp.where(forced, BIG, 0.0), -BIG)
            score = jnp.where(blk < n_s, score, NEG_INF)
            score_t = score.T
            sel_rows = []
            for qi in range(QROWS):
                col = score_t[:, qi:qi + 1]
                rw = score[qi:qi + 1, :]
                ahead = jnp.logical_or(col > rw, jnp.logical_and(col == rw, s_src < s_dst))
                rank = jnp.sum(ahead.astype(jnp.int32), axis=0, keepdims=True)
                sel_rows.append((rank < N_SEL).astype(f32))
            sel_ref[g] = jnp.concatenate(sel_rows, axis=0)
            m_ref[g] = jnp.full((nq, 1), NEG, f32)
            l_ref[g] = jnp.zeros((nq, 1), f32)
            acc_ref[g] = jnp.zeros((nq, DH_N), f32)

    @pl.when(p < nstep)
    def _():
        k0 = p * tk
        expand = ((k0 + lax.broadcasted_iota(jnp.int32, (sl, tk), 1)) // SEL_LEN
                  == lax.broadcasted_iota(jnp.int32, (sl, tk), 0)).astype(bf16)
        for g in range(G_N):
            picked = jnp.dot(sel_ref[g].astype(bf16), expand, preferred_element_type=f32)
            keep = jnp.concatenate([picked] * J_N, axis=0) > 0.5
            kk = jnp.concatenate([pages[r][:, g * DH_N:(g + 1) * DH_N] for r in range(PAGES_PER_STEP)],
                                 axis=0).astype(bf16)
            vv = jnp.concatenate([pages[r][:, KVW + g * DH_N:KVW + (g + 1) * DH_N] for r in range(PAGES_PER_STEP)],
                                 axis=0).astype(bf16)
            s = lax.dot_general(q_of(g), kk, nt, preferred_element_type=f32) * SCALE_N
            online(g, s, keep, vv)

    @pl.when(p == nstep)
    def _():
        gates = jax.nn.sigmoid(zg_ref[...])
        gl = lax.broadcasted_iota(jnp.int32, gates.shape, 1)
        pos_n = pos_of((nq, NEW_ROWS))
        l_n = lax.broadcasted_iota(jnp.int32, (nq, NEW_ROWS), 1)
        keep_new = jnp.logical_and(l_n <= pos_n, l_n < l_true)
        wbuf = winp_ref.shape[0]
        pos_w = pos_of((nq, wbuf + NEW_ROWS))
        i_w = lax.broadcasted_iota(jnp.int32, (nq, wbuf + NEW_ROWS), 1)
        keep_win = jnp.logical_or(
            jnp.logical_and(i_w < wbuf, i_w - wbuf > pos_w - WINDOW),
            jnp.logical_and(i_w >= wbuf, jnp.logical_and(i_w - wbuf <= pos_w, i_w - wbuf < l_true)))
        for g in range(G_N):
            q = q_of(g)
            kn = new_ref[:, g * DH_N:(g + 1) * DH_N].astype(bf16)
            vn = new_ref[:, KVW + g * DH_N:KVW + (g + 1) * DH_N].astype(bf16)
            s = lax.dot_general(q, kn, nt, preferred_element_type=f32) * SCALE_N
            online(g, s, keep_new, vn)
            o_sel = acc_ref[g] / l_ref[g]
            kw = jnp.concatenate([winp_ref[:, g * DH_N:(g + 1) * DH_N], wnew_ref[:, g * DH_N:(g + 1) * DH_N]],
                                 axis=0).astype(bf16)
            vw = jnp.concatenate([winp_ref[:, KVW + g * DH_N:KVW + (g + 1) * DH_N],
                                  wnew_ref[:, KVW + g * DH_N:KVW + (g + 1) * DH_N]], axis=0).astype(bf16)
            s = lax.dot_general(q, kw, nt, preferred_element_type=f32) * SCALE_N
            o_win = _softmax_pv(s, keep_win, vw)
            o_cmp = ocmp_ref[g]
            for j in range(J_N):
                acc = jnp.zeros((QROWS, DH_N), f32)
                for which, o in enumerate((o_cmp, o_sel, o_win)):
                    col = jnp.sum(jnp.where(gl == ZG_GN + (g * J_N + j) * 3 + which, gates, 0.0), axis=-1, keepdims=True)
                    acc = acc + col * o[j * QROWS:(j + 1) * QROWS]
                o_ref[:, (g * J_N + j) * DH_N:(g * J_N + j + 1) * DH_N] = acc.astype(o_ref.dtype)


def paged_attend(page_table, cache2, q8, kcv, new_s, winp, wnew, zg8, l_true):
    b, npages = page_table.shape
    page = cache2.shape[1]
    nstep = npages // PAGES_PER_STEP
    assert winp.shape[1] == WINDOW and l_true <= QROWS and npages % PAGES_PER_STEP == 0
    n_r = kcv.shape[3]
    nq = J_N * QROWS

    def page_spec(r):
        return pl.BlockSpec((None, page, 2 * KVW),
                            lambda bi, p, pt: (pt[bi, jnp.minimum(p * PAGES_PER_STEP + r, npages - 1)], 0, 1))

    per_b = lambda rows, cols: pl.BlockSpec((None, rows, cols), lambda bi, p, pt: (bi, 0, 0))
    grid_spec = pltpu.PrefetchScalarGridSpec(
        num_scalar_prefetch=1,
        grid=(b, nstep + 1),
        in_specs=[page_spec(r) for r in range(PAGES_PER_STEP)] + [
            per_b(QROWS, D_N),
            pl.BlockSpec((None, 2, G_N, n_r, DH_N), lambda bi, p, pt: (bi, 0, 0, 0, 0)),
            per_b(NEW_ROWS, 2 * KVW), per_b(WINDOW, 2 * KVW), per_b(NEW_ROWS, 2 * KVW), per_b(QROWS, LANES)],
        out_specs=per_b(QROWS, D_N),
        scratch_shapes=[pltpu.VMEM((G_N, QROWS, 2 * LANES), jnp.float32),
                        pltpu.VMEM((G_N, nq, DH_N), jnp.float32),
                        pltpu.VMEM((G_N, nq, 1), jnp.float32),
                        pltpu.VMEM((G_N, nq, 1), jnp.float32),
                        pltpu.VMEM((G_N, nq, DH_N), jnp.float32)])
    return pl.pallas_call(
        functools.partial(_paged_attend_kernel, page=page, nstep=nstep, l_true=l_true),
        grid_spec=grid_spec,
        out_shape=jax.ShapeDtypeStruct((b, QROWS, D_N), jnp.float32),
        compiler_params=pltpu.CompilerParams(dimension_semantics=("parallel", "arbitrary"),
                                             vmem_limit_bytes=VMEM_LIMIT_BYTES),
        name="nsa_paged_attend",
    )(page_table, *([cache2] * PAGES_PER_STEP), q8, kcv, new_s, winp, wnew, zg8)


def _extract_top(s, k):
    rows = lax.broadcasted_iota(jnp.int32, s.shape, 0)
    nrow = s.shape[0]
    work = s
    taken = jnp.zeros(s.shape, jnp.bool_)
    tops = []
    for _ in range(k):
        m = jnp.max(work, axis=0, keepdims=True)
        first = jnp.min(jnp.where(work == m, rows, nrow), axis=0, keepdims=True)
        hit = rows == first
        taken = jnp.logical_or(taken, hit)
        work = jnp.where(hit, NEG_INF, work)
        tops.append(m)
    return tops, taken


def _peer_route_kernel(qt_ref, keys_ref, s1_ref, e1_ref, s2_ref, e2_ref, tau_ref):
    halves = []
    for c in range(2):
        q = qt_ref[c * PEER_QHALF:(c + 1) * PEER_QHALF, :].astype(bf16)
        s = jnp.dot(keys_ref[c].astype(bf16), q, preferred_element_type=jnp.float32)
        tops, taken = _extract_top(s, PEER_TOPK)
        halves.append((s, tops, taken))
    (sa, ta, ma), (sb, tb, mb) = halves
    assert PEER_TOPK == 16
    ta_col = jnp.concatenate(ta, axis=0)
    tb_col = jnp.concatenate(tb, axis=0)
    a_idx = lax.broadcasted_iota(jnp.int32, (8, ta_col.shape[1]), 0)
    cand = jnp.concatenate(
        [ta[0] + tb_col, ta[1] + tb_col, ta[2] + tb_col[0:8], ta[3] + tb_col[0:8]]
        + [jnp.where(a_idx >= 4, ta_col[0:8] + tb[b], NEG_INF) for b in range(3)]
        + [ta_col[8:16] + tb[0]], axis=0)
    ctops, _ = _extract_top(cand, PEER_TOPK)
    cmax = ctops[0]
    z = jnp.exp(ctops[0] - cmax)
    for r in range(1, PEER_TOPK):
        z = z + jnp.exp(ctops[r] - cmax)
    tau_ref[...] = ctops[PEER_TOPK - 1]
    s1_ref[...] = jnp.where(ma, sa, NEG_INF)
    s2_ref[...] = jnp.where(mb, sb, NEG_INF)
    e1_ref[...] = jnp.where(ma, jnp.exp(sa - ta[0]), 0.0) / z
    e2_ref[...] = jnp.where(mb, jnp.exp(sb - tb[0]), 0.0)


def peer_route(qt, pkeys, tb):
    n = qt.shape[1]
    big = jax.ShapeDtypeStruct((PEER_HEADS, N_KEYS, n), jnp.float32)
    bspec = pl.BlockSpec((None, N_KEYS, tb), lambda i, h: (h, 0, i))
    return pl.pallas_call(
        _peer_route_kernel,
        grid=(n // tb, PEER_HEADS),
        in_specs=[pl.BlockSpec((2 * PEER_QHALF, tb), lambda i, h: (h, i)),
                  pl.BlockSpec((None, 2, N_KEYS, PEER_QHALF), lambda i, h: (h, 0, 0, 0))],
        out_specs=[bspec, bspec, bspec, bspec, pl.BlockSpec((None, 1, tb), lambda i, h: (h, 0, i))],
        out_shape=[big, big, big, big, jax.ShapeDtypeStruct((PEER_HEADS, 1, n), jnp.float32)],
        compiler_params=pltpu.CompilerParams(dimension_semantics=("parallel", "parallel")),
        name="peer_route",
    )(qt, pkeys)


def _peer_dense_kernel(xt_ref, u_ref, vt_ref, s1_ref, e1_ref, s2_ref, e2_ref, tau_ref, o_ref, *, sub):
    j = pl.program_id(1)

    @pl.when(j == 0)
    def _():
        o_ref[...] = jnp.zeros_like(o_ref)

    ht = jnp.dot(u_ref[...], xt_ref[...], preferred_element_type=jnp.float32)
    acts = []
    for a in range(sub):
        i1 = j * sub + a
        g = None
        for h in range(PEER_HEADS):
            s1row = s1_ref[h, pl.ds(i1, 1), :]
            e1row = e1_ref[h, pl.ds(i1, 1), :]
            c = s2_ref[h] + s1row
            t = jnp.where(c >= tau_ref[h], e2_ref[h], 0.0) * e1row
            g = t if g is None else g + t
        acts.append((_gelu_tanh(ht[a * N_KEYS:(a + 1) * N_KEYS, :]) * g).astype(bf16))
    act = jnp.concatenate(acts, axis=0) if sub > 1 else acts[0]
    o_ref[...] += jnp.dot(vt_ref[...], act, preferred_element_type=jnp.float32)


def peer_dense(xt, u_bf, vt_bf, s1, e1, s2, e2, tau, tb, sub):
    d, n = xt.shape
    e = u_bf.shape[0]
    te = sub * N_KEYS
    once = dict(pipeline_mode=pl.Buffered(1))
    rspec = pl.BlockSpec((PEER_HEADS, N_KEYS, tb), lambda i, j: (0, 0, i), **once)
    return pl.pallas_call(
        functools.partial(_peer_dense_kernel, sub=sub),
        grid=(n // tb, e // te),
        in_specs=[pl.BlockSpec((d, tb), lambda i, j: (0, i), **once),
                  pl.BlockSpec((te, d), lambda i, j: (j, 0)),
                  pl.BlockSpec((d, te), lambda i, j: (0, j)),
                  rspec, rspec, rspec, rspec,
                  pl.BlockSpec((PEER_HEADS, 1, tb), lambda i, j: (0, 0, i), **once)],
        out_specs=pl.BlockSpec((d, tb), lambda i, j: (0, i)),
        out_shape=jax.ShapeDtypeStruct((d, n), jnp.float32),
        compiler_params=pltpu.CompilerParams(dimension_semantics=("parallel", "arbitrary"),
                                             vmem_limit_bytes=BIG_VMEM_LIMIT_BYTES),
        name="peer_dense",
    )(xt, u_bf, vt_bf, s1, e1, s2, e2, tau)


def peer_ffn_t(xn_bf, w_pq_t, pkeys, u_bf, vt_bf):
    n = xn_bf.shape[0]
    xt = xn_bf.T
    qt = pmm(w_pq_t, xt)
    s1, e1, s2, e2, tau = peer_route(qt, pkeys, _pick(n, (256, 128)))
    return peer_dense(xt, u_bf, vt_bf, s1, e1, s2, e2, tau, _pick(n, (512, 256, 128)), PEER_SUB)


def _in_proj_weights(w_in):
    o = _IN_OFFS
    w_main = jnp.concatenate([w_in[:, o[0]:o[3]], w_in[:, o[5]:o[8]], w_in[:, o[9]:o[11]]], axis=1).astype(bf16)
    w_gate = jnp.concatenate([w_in[:, o[3]:o[5]], w_in[:, o[8]:o[9]],
                              jnp.zeros((w_in.shape[0], LANES - 2 * NH_M - 3 * H_N), w_in.dtype)], axis=1).astype(bf16)
    return w_main, w_gate


def kernel(x_prompt, x_sample, cache_kv, state_win_kv, state_conv, state_C, state_n, state_m, page_table,
           norm_mix, norm_ffn, norm_final, w_in, w_conv, b_conv, w_bd, b_gates, w_hnorm, w_skip,
           cmp_pe, cmp_w1, cmp_w2, w_br_m, w_br_n, w_out, w_pq, peer_keys, peer_u, peer_v):
    assert w_in.shape[0] == DEPTH == 1
    f32 = jnp.float32
    l = 0
    Bp, Lp_, _ = x_prompt.shape
    Bs, Ls, _ = x_sample.shape
    w_main, w_gate = _in_proj_weights(w_in[l])
    wbd_full = expand_blockdiag(w_bd[l])
    gate_bias = jnp.concatenate([b_gates[l, 0], b_gates[l, 1], jnp.zeros((LANES - 2 * NH_M,), f32)])[None]
    w_br_m_bf, w_br_n_bf, w_out_bf = w_br_m[l].astype(bf16), w_br_n[l].astype(bf16), w_out[l].astype(bf16)
    w_pq_t = w_pq[l].T.astype(bf16)
    u_bf = peer_u[l].astype(bf16)
    vt_bf = peer_v[l].T.astype(bf16)

    def mixers(x, conv_prev, C0, n0, m0, nsa_fn, ch, lpad):
        B, L, _ = x.shape
        n = B * L
        x2 = x.reshape(n, D_MODEL)
        xn = rmsnorm_rows(x2, norm_mix[l], bf16)
        z2 = pmm(xn, w_main)
        zg2 = pmm(xn, w_gate)
        z3 = z2.reshape(B, L, Z_COLS)
        zg3 = zg2.reshape(B, L, LANES)
        xm = z3[:, :, Z_XM:Z_XM + D_M]
        new_conv = jnp.concatenate([conv_prev, xm], axis=1)[:, L:]
        prev8 = jnp.pad(conv_prev, ((0, 0), (PREV_ROWS - (CONV_W - 1), 0), (0, 0)))
        if lpad == L:
            zm3, zgm3, cols = z3, zg3, (Z_XM // DH_M, Z_VM // DH_M, Z_OM // DH_M)
        else:
            zm3 = jnp.pad(z3[:, :, :Z_Q], ((0, 0), (0, lpad - L), (0, 0)))
            zgm3 = jnp.pad(zg3, ((0, 0), (0, lpad - L), (0, 0)))
            cols = (Z_XM // DH_M, Z_VM // DH_M, Z_OM // DH_M)
        y_m, C1, n1, m1 = mlstm_branch(zm3, zgm3, cols, prev8, w_conv[l], b_conv[l], wbd_full, gate_bias,
                                       C0, n0[:, :, None, :], jnp.broadcast_to(m0[:, :, None, None], (B, NH_M, 1, LANES)),
                                       w_hnorm[l], w_skip[l], L, ch)
        y_m = y_m[:, :L].reshape(n, D_M)
        kv_new = z3[:, :, Z_KV:Z_KV + N_KV_SETS * KVW].reshape(B, L, N_KV_SETS, G_N, DH_N)
        win_new = z3[:, :, Z_WIN:Z_WIN + 2 * KVW].reshape(B, L, 2, G_N, DH_N)
        y_n, new_win = nsa_fn(z3, zg3, kv_new, win_new)
        merged = merge_branches(y_m, y_n.reshape(n, D_N).astype(bf16), w_br_m_bf, w_br_n_bf, z2)
        h2 = out_proj_residual(merged, w_out_bf, x2)
        return h2, kv_new, new_win, new_conv, C1, n1[:, :, 0], m1[:, :, 0, 0]

    def nsa_p(z3, zg3, kv_new, win_new):
        kcv = compress_prompt(z3, Z_KV // DH_N, cmp_pe[l], cmp_w1[l], cmp_w2[l])
        return nsa_prompt(z3, zg3, kcv), win_new[:, -min(WINDOW, z3.shape[1]):]

    def nsa_s(z3, zg3, kv_new, win_new):
        L = z3.shape[1]
        padr = lambda t, rows: jnp.pad(t, ((0, 0), (0, rows - t.shape[1]), (0, 0)))
        cache2 = cache_kv[l].reshape(cache_kv.shape[1], cache_kv.shape[2], N_KV_SETS * KVW)
        kcv = paged_compress(page_table, cache2, padr(z3[:, :, Z_KV:Z_KV + 2 * KVW], NEW_ROWS),
                             cmp_pe[l], cmp_w1[l].astype(bf16), cmp_w2[l])
        y8 = paged_attend(page_table, cache2, padr(z3[:, :, Z_Q:Z_Q + D_N], QROWS), kcv,
                          padr(z3[:, :, Z_KV + 2 * KVW:Z_KV + 4 * KVW], NEW_ROWS),
                          state_win_kv[l].reshape(Bs, -1, 2 * KVW), padr(z3[:, :, Z_WIN:Z_WIN + 2 * KVW], NEW_ROWS),
                          padr(zg3, QROWS), L)
        return y8[:, :L], jnp.concatenate([state_win_kv[l][:, L:], win_new], axis=1)

    hp, kvp, winp, convp, Cp, n_p, m_p = mixers(
        x_prompt, jnp.zeros((Bp, CONV_W - 1, D_M), f32), jnp.zeros((Bp, NH_M, DH_M, DH_M), f32),
        jnp.zeros((Bp, NH_M, DH_M), f32), jnp.zeros((Bp, NH_M), f32), nsa_p, 256, Lp_)
    hs, kvs, wins, convs, Cs, n_s, m_s = mixers(
        x_sample, state_conv[l], state_C[l], state_n[l], state_m[l], nsa_s, 16, 16)

    def ffn_and_norm(h2):
        xn2 = rmsnorm_rows(h2, norm_ffn[l], bf16)
        peer_t = peer_ffn_t(xn2, w_pq_t, peer_keys[l], u_bf, vt_bf)
        return final_norm(h2, peer_t, norm_final)

    y_prompt = ffn_and_norm(hp).reshape(x_prompt.shape)
    y_sample = ffn_and_norm(hs).reshape(x_sample.shape)
    st = lambda t: t[None]
    return (y_prompt, y_sample, st(kvp), st(kvs), st(winp), st(wins), st(convp), st(convs),
            st(Cp), st(Cs), st(n_p), st(n_s), st(m_p), st(m_s))
```

```python
import functools

import jax
import jax.numpy as jnp
import numpy as np
from jax import lax
from jax.experimental import pallas as pl
from jax.experimental.pallas import tpu as pltpu

D_MODEL = 4096
DEPTH = 1
D_M = D_MODEL // 2
DH_M = 256
NH_M = D_M // DH_M
CONV_W = 4
QKV_BLOCK = 4
D_N = D_MODEL // 2
DH_N = 128
H_N = D_N // DH_N
G_N = 4
J_N = H_N // G_N
KVW = G_N * DH_N
N_KV_SETS = 4
CMP_STRIDE = 16
CMP_LEN = 2 * CMP_STRIDE
CMP_HID = 2 * DH_N
SEL_LEN = 64
N_SEL = 16
WINDOW = 512
SCALE_N = DH_N ** -0.5
PEER_HEADS = 8
N_KEYS = 128
PEER_TOPK = 16
PEER_QDIM = 256
PEER_QHALF = PEER_QDIM // 2
IN_SIZES = (D_M, D_M, D_M, NH_M, NH_M, D_N, N_KV_SETS * KVW, 2 * KVW, 3 * H_N, D_MODEL, D_MODEL)
EPS = 1e-6
NEG = -1e30
BIG = 1e9
NEG_INF = float('-inf')

LANES = 128
VMEM_LIMIT_BYTES = 48 * 1024 * 1024
BIG_VMEM_LIMIT_BYTES = 56 * 1024 * 1024
PEER_SUB = 4
PREV_ROWS = 8
PAGES_PER_STEP = 16
NEW_ROWS = 16
QROWS = 8
SEL_CHUNK = 512
GATE_ROWS = 64
bf16 = jnp.bfloat16

_IN_OFFS = np.concatenate([[0], np.cumsum(IN_SIZES)]).tolist()
Z_XM, Z_VM, Z_OM = 0, D_M, 2 * D_M
Z_Q = 3 * D_M
Z_KV = Z_Q + D_N
Z_WIN = Z_KV + N_KV_SETS * KVW
Z_GM = Z_WIN + 2 * KVW
Z_GN = Z_GM + D_MODEL
Z_COLS = Z_GN + D_MODEL
ZG_IG, ZG_FG, ZG_GN = 0, NH_M, 2 * NH_M


def _pick(n, cands):
    for c in cands:
        if n % c == 0:
            return c
    return n


def _gelu_tanh(x):
    return 0.5 * x * (1.0 + jnp.tanh(np.float32(np.sqrt(2.0 / np.pi)) * (x + np.float32(0.044715) * (x * x * x))))


def _split3(x):
    hi = x.astype(bf16)
    r = x - hi.astype(jnp.float32)
    mid = r.astype(bf16)
    lo = (r - mid.astype(jnp.float32)).astype(bf16)
    return hi, mid, lo


def _rmsnorm_kernel(x_ref, g_ref, o_ref):
    x = x_ref[...]
    o_ref[...] = (x * lax.rsqrt(jnp.mean(x * x, axis=-1, keepdims=True) + EPS) * g_ref[...]).astype(o_ref.dtype)


def rmsnorm_rows(x2, g, out_dtype):
    n, d = x2.shape
    tr = _pick(n, (256, 128))
    return pl.pallas_call(
        _rmsnorm_kernel,
        grid=(n // tr,),
        in_specs=[pl.BlockSpec((tr, d), lambda i: (i, 0)), pl.BlockSpec((1, d), lambda i: (0, 0))],
        out_specs=pl.BlockSpec((tr, d), lambda i: (i, 0)),
        out_shape=jax.ShapeDtypeStruct((n, d), out_dtype),
        compiler_params=pltpu.CompilerParams(dimension_semantics=("parallel",), vmem_limit_bytes=VMEM_LIMIT_BYTES),
        name="rmsnorm",
    )(x2, g.reshape(1, d))


def _mm_kernel(a_ref, b_ref, o_ref):
    o_ref[...] = jnp.dot(a_ref[...], b_ref[...], preferred_element_type=jnp.float32).astype(o_ref.dtype)


def pmm(a, b, out_dtype=jnp.float32):
    m, kd = a.shape
    n = b.shape[1]
    tm = _pick(m, (1024, 512, 256, 128))
    tn = _pick(n, (1024, 512, 256, 128))
    return pl.pallas_call(
        _mm_kernel,
        grid=(m // tm, n // tn),
        in_specs=[pl.BlockSpec((tm, kd), lambda i, j: (i, 0)), pl.BlockSpec((kd, tn), lambda i, j: (0, j))],
        out_specs=pl.BlockSpec((tm, tn), lambda i, j: (i, j)),
        out_shape=jax.ShapeDtypeStruct((m, n), out_dtype),
        compiler_params=pltpu.CompilerParams(dimension_semantics=("parallel", "parallel"),
                                             vmem_limit_bytes=BIG_VMEM_LIMIT_BYTES),
        name="proj",
    )(a, b)


def _merge_kernel(ym_ref, yn_ref, wm_ref, wn_ref, gm_ref, gn_ref, o_ref):
    pm = jnp.dot(ym_ref[...], wm_ref[...], preferred_element_type=jnp.float32)
    pn = jnp.dot(yn_ref[...], wn_ref[...], preferred_element_type=jnp.float32)
    o_ref[...] = (jax.nn.sigmoid(gm_ref[...]) * pm + jax.nn.sigmoid(gn_ref[...]) * pn).astype(o_ref.dtype)


def merge_branches(ym, yn, wm, wn, z2):
    m = ym.shape[0]
    n = wm.shape[1]
    tm = _pick(m, (1024, 512, 256, 128))
    tn = 512
    return pl.pallas_call(
        _merge_kernel,
        grid=(m // tm, n // tn),
        in_specs=[pl.BlockSpec((tm, D_M), lambda i, j: (i, 0)), pl.BlockSpec((tm, D_N), lambda i, j: (i, 0)),
                  pl.BlockSpec((D_M, tn), lambda i, j: (0, j)), pl.BlockSpec((D_N, tn), lambda i, j: (0, j)),
                  pl.BlockSpec((tm, tn), lambda i, j: (i, Z_GM // tn + j)),
                  pl.BlockSpec((tm, tn), lambda i, j: (i, Z_GN // tn + j))],
        out_specs=pl.BlockSpec((tm, tn), lambda i, j: (i, j)),
        out_shape=jax.ShapeDtypeStruct((m, n), bf16),
        compiler_params=pltpu.CompilerParams(dimension_semantics=("parallel", "parallel"),
                                             vmem_limit_bytes=VMEM_LIMIT_BYTES),
        name="merge_branches",
    )(ym, yn, wm, wn, z2, z2)


def _mm_res_kernel(a_ref, b_ref, r_ref, o_ref):
    o_ref[...] = r_ref[...] + jnp.dot(a_ref[...], b_ref[...], preferred_element_type=jnp.float32)


def out_proj_residual(a, b, r):
    m, kd = a.shape
    n = b.shape[1]
    tm = _pick(m, (1024, 512, 256, 128))
    tn = 512
    return pl.pallas_call(
        _mm_res_kernel,
        grid=(m // tm, n // tn),
        in_specs=[pl.BlockSpec((tm, kd), lambda i, j: (i, 0)), pl.BlockSpec((kd, tn), lambda i, j: (0, j)),
                  pl.BlockSpec((tm, tn), lambda i, j: (i, j))],
        out_specs=pl.BlockSpec((tm, tn), lambda i, j: (i, j)),
        out_shape=jax.ShapeDtypeStruct((m, n), jnp.float32),
        compiler_params=pltpu.CompilerParams(dimension_semantics=("parallel", "parallel"),
                                             vmem_limit_bytes=VMEM_LIMIT_BYTES),
        name="out_proj",
    )(a, b, r)


def _final_kernel(h_ref, pt_ref, g_ref, o_ref):
    x = h_ref[...] + pt_ref[...].T
    o_ref[...] = x * lax.rsqrt(jnp.mean(x * x, axis=-1, keepdims=True) + EPS) * g_ref[...]


def final_norm(h2, peer_t, g):
    n, d = h2.shape
    tr = _pick(n, (256, 128))
    return pl.pallas_call(
        _final_kernel,
        grid=(n // tr,),
        in_specs=[pl.BlockSpec((tr, d), lambda i: (i, 0)), pl.BlockSpec((d, tr), lambda i: (0, i)),
                  pl.BlockSpec((1, d), lambda i: (0, 0))],
        out_specs=pl.BlockSpec((tr, d), lambda i: (i, 0)),
        out_shape=jax.ShapeDtypeStruct((n, d), jnp.float32),
        compiler_params=pltpu.CompilerParams(dimension_semantics=("parallel",), vmem_limit_bytes=VMEM_LIMIT_BYTES),
        name="final_norm",
    )(h2, peer_t, g.reshape(1, d))


def _blockdiag_apply(x, w_ref, which):
    parts = [jnp.dot(x[:, hb * LANES:(hb + 1) * LANES].astype(bf16), w_ref[which, hb].astype(bf16),
                     preferred_element_type=jnp.float32) for hb in range(DH_M // LANES)]
    return jnp.concatenate(parts, axis=1)


def _mlstm_kernel(xm_ref, vm_ref, om_ref, zg_ref, prev_ref, wc_ref, bc_ref, wbd_ref, gb_ref, c0_ref, n0_ref, m0_ref,
                  hn_ref, sk_ref, y_ref, c1_ref, n1_ref, m1_ref, *, ch, l_true):
    h = pl.program_id(1)
    lp = xm_ref.shape[0]
    nchunk = lp // ch
    f32 = jnp.float32
    tri = lax.broadcasted_iota(jnp.int32, (ch, ch), 0) >= lax.broadcasted_iota(jnp.int32, (ch, ch), 1)
    tri_bf = tri.astype(bf16)
    lane_g = lax.broadcasted_iota(jnp.int32, (ch, LANES), 1)
    sub_g = lax.broadcasted_iota(jnp.int32, (LANES, ch), 0)

    def chunk(ci, carry):
        c_st, n_st, m_st = carry
        r0 = pl.multiple_of(ci * ch, ch)
        xm = xm_ref[pl.ds(r0, ch), :]
        before = xm_ref[pl.ds(pl.multiple_of(jnp.maximum(r0 - PREV_ROWS, 0), PREV_ROWS), PREV_ROWS), :]
        before = jnp.where(ci == 0, prev_ref[...], before)
        xe = jnp.concatenate([before, xm], axis=0)
        lo = PREV_ROWS - (CONV_W - 1)
        conv = bc_ref[...] + sum(xe[lo + w:lo + w + ch, :] * wc_ref[w:w + 1, :] for w in range(CONV_W))
        c = conv * jax.nn.sigmoid(conv)
        q = _blockdiag_apply(c, wbd_ref, 0)
        k = _blockdiag_apply(c, wbd_ref, 1) * (DH_M ** -0.5)
        v = _blockdiag_apply(vm_ref[pl.ds(r0, ch), :], wbd_ref, 2)
        zg = zg_ref[pl.ds(r0, ch), :] + gb_ref[...]
        live = (r0 + lax.broadcasted_iota(jnp.int32, (ch, LANES), 0)) < l_true
        a_all = jnp.where(live, zg, NEG_INF)
        lf_all = jnp.where(live, jnp.minimum(zg, 0.0) - jnp.log1p(jnp.exp(-jnp.abs(zg))), 0.0)
        b_all = jnp.zeros((ch, LANES), f32)
        for part in _split3(lf_all):
            b_all = b_all + jnp.dot(tri_bf, part, preferred_element_type=f32)
        a_col = jnp.sum(jnp.where(lane_g == ZG_IG + h, a_all, 0.0), axis=1, keepdims=True)
        b_col = jnp.sum(jnp.where(lane_g == ZG_FG + h, b_all, 0.0), axis=1, keepdims=True)
        a_row = jnp.sum(jnp.where(sub_g == ZG_IG + h, a_all.T, 0.0), axis=0, keepdims=True)
        b_row = jnp.sum(jnp.where(sub_g == ZG_FG + h, b_all.T, 0.0), axis=0, keepdims=True)
        dmat = jnp.where(tri, b_col - b_row + a_row, NEG_INF)
        inter = b_col + m_st
        m = jnp.maximum(inter, jnp.max(dmat, axis=1, keepdims=True))
        w_intra = jnp.exp(dmat - m)
        w_inter = jnp.exp(inter - m)
        qb = q.astype(bf16)
        kb = k.astype(bf16)
        vb = v.astype(bf16)
        qk = lax.dot_general(qb, kb, (((1,), (1,)), ((), ())), preferred_element_type=f32) * w_intra
        num = w_inter * jnp.dot(qb, c_st.astype(bf16), preferred_element_type=f32) \
            + jnp.dot(qk.astype(bf16), vb, preferred_element_type=f32)
        den = w_inter * jnp.sum(q * n_st, axis=1, keepdims=True) + jnp.sum(qk, axis=1, keepdims=True)
        hh = num / jnp.maximum(jnp.abs(den), jnp.exp(-m))
        m_end = m[ch - 1:ch, :]
        b_end = b_col[ch - 1:ch, :]
        w_end = jnp.exp(b_end - b_col + a_col - m_end)
        decay = jnp.exp(b_end + m_st - m_end)
        kw = k * w_end
        c_new = decay * c_st + lax.dot_general(kw.astype(bf16), vb, (((0,), (0,)), ((), ())),
                                               preferred_element_type=f32)
        n_new = decay * n_st + jnp.sum(kw, axis=0, keepdims=True)
        hg = hh * jax.nn.sigmoid(om_ref[pl.ds(r0, ch), :])
        mu = jnp.mean(hg, axis=1, keepdims=True)
        var = jnp.mean(jnp.square(hg - mu), axis=1, keepdims=True)
        y = (hg - mu) * lax.rsqrt(var + EPS) * hn_ref[...] + sk_ref[...] * c
        y_ref[pl.ds(r0, ch), :] = y.astype(y_ref.dtype)
        return c_new, n_new, m_end

    c_fin, n_fin, m_fin = lax.fori_loop(0, nchunk, chunk, (c0_ref[...], n0_ref[...], m0_ref[:, :1]))
    c1_ref[...] = c_fin
    n1_ref[...] = n_fin
    m1_ref[...] = jnp.broadcast_to(m_fin, m1_ref.shape)


def expand_blockdiag(w_bd):
    per = LANES // QKV_BLOCK
    w = w_bd.reshape(3, D_M // LANES, per, QKV_BLOCK, QKV_BLOCK)
    full = jnp.einsum('tbnio,nm->tbnimo', w, jnp.eye(per, dtype=w_bd.dtype))
    return full.reshape(3, D_M // LANES, LANES, LANES)


def mlstm_branch(z3, zg3, col_blocks, prev8, w_conv, b_conv, wbd_full, gate_bias, c0, n0, m0, w_hnorm, w_skip,
                 l_true, ch):
    b, lp, _ = z3.shape
    cx, cv, co = col_blocks
    kern = functools.partial(_mlstm_kernel, ch=ch, l_true=l_true)
    col = lambda c0_: pl.BlockSpec((None, lp, DH_M), lambda bi, h: (bi, 0, c0_ + h))
    vec = pl.BlockSpec((1, DH_M), lambda bi, h: (0, h))
    st = lambda r, c: pl.BlockSpec((None, None, r, c), lambda bi, h: (bi, h, 0, 0))
    return pl.pallas_call(
        kern,
        grid=(b, NH_M),
        in_specs=[col(cx), col(cv), col(co),
                  pl.BlockSpec((None, lp, LANES), lambda bi, h: (bi, 0, 0)),
                  pl.BlockSpec((None, PREV_ROWS, DH_M), lambda bi, h: (bi, 0, h)),
                  pl.BlockSpec((CONV_W, DH_M), lambda bi, h: (0, h)),
                  vec,
                  pl.BlockSpec((3, DH_M // LANES, LANES, LANES), lambda bi, h: (0, h, 0, 0)),
                  pl.BlockSpec((1, LANES), lambda bi, h: (0, 0)),
                  st(DH_M, DH_M), st(1, DH_M), st(1, LANES), vec, vec],
        out_specs=[pl.BlockSpec((None, lp, DH_M), lambda bi, h: (bi, 0, h)),
                   st(DH_M, DH_M), st(1, DH_M), st(1, LANES)],
        out_shape=[jax.ShapeDtypeStruct((b, lp, D_M), bf16),
                   jax.ShapeDtypeStruct((b, NH_M, DH_M, DH_M), jnp.float32),
                   jax.ShapeDtypeStruct((b, NH_M, 1, DH_M), jnp.float32),
                   jax.ShapeDtypeStruct((b, NH_M, 1, LANES), jnp.float32)],
        compiler_params=pltpu.CompilerParams(dimension_semantics=("parallel", "parallel"),
                                             vmem_limit_bytes=VMEM_LIMIT_BYTES),
        name="mlstm_branch",
    )(z3, z3, z3, zg3, prev8, w_conv, b_conv.reshape(1, D_M), wbd_full, gate_bias, c0, n0, m0,
      w_hnorm.reshape(1, D_M), w_skip.reshape(1, D_M))


def _compress_kernel(x_ref, pe_ref, w1_ref, w2_ref, o_ref):
    nchunk = x_ref.shape[0] // CMP_STRIDE
    first = jnp.zeros((nchunk, CMP_HID), jnp.float32)
    second = jnp.zeros((nchunk, CMP_HID), jnp.float32)
    for l in range(CMP_STRIDE):
        xl = x_ref[pl.ds(l, nchunk, stride=CMP_STRIDE), :]
        first += jnp.dot((xl + pe_ref[l:l + 1, :]).astype(bf16), w1_ref[l].astype(bf16),
                         preferred_element_type=jnp.float32)
        second += jnp.dot((xl + pe_ref[CMP_STRIDE + l:CMP_STRIDE + l + 1, :]).astype(bf16),
                          w1_ref[CMP_STRIDE + l].astype(bf16), preferred_element_type=jnp.float32)
    nxt = jnp.concatenate([second[1:], jnp.zeros((1, CMP_HID), jnp.float32)], axis=0)
    row = lax.broadcasted_iota(jnp.int32, (nchunk, CMP_HID), 0)
    hid = jnp.where(row < nchunk - 1, _gelu_tanh(first + nxt), 0.0)
    o_ref[...] = jnp.dot(hid.astype(bf16), w2_ref[...].astype(bf16), preferred_element_type=jnp.float32)


def compress_prompt(z3, col0, cmp_pe, cmp_w1, cmp_w2):
    b, l, _ = z3.shape
    nchunk = l // CMP_STRIDE
    return pl.pallas_call(
        _compress_kernel,
        grid=(b, 2, G_N),
        in_specs=[pl.BlockSpec((None, l, DH_N), lambda bi, s, g: (bi, 0, col0 + s * G_N + g)),
                  pl.BlockSpec((None, CMP_LEN, DH_N), lambda bi, s, g: (s, 0, 0)),
                  pl.BlockSpec((None, CMP_LEN, DH_N, CMP_HID), lambda bi, s, g: (s, 0, 0, 0)),
                  pl.BlockSpec((None, CMP_HID, DH_N), lambda bi, s, g: (s, 0, 0))],
        out_specs=pl.BlockSpec((None, None, None, nchunk, DH_N), lambda bi, s, g: (bi, s, g, 0, 0)),
        out_shape=jax.ShapeDtypeStruct((b, 2, G_N, nchunk, DH_N), jnp.float32),
        compiler_params=pltpu.CompilerParams(dimension_semantics=("parallel", "parallel", "parallel"),
                                             vmem_limit_bytes=VMEM_LIMIT_BYTES),
        name="nsa_compress",
    )(z3, cmp_pe, cmp_w1, cmp_w2)


def _softmax_pv(s, keep, v):
    s = jnp.where(keep, s, NEG)
    m = jnp.max(s, axis=-1, keepdims=True)
    e = jnp.exp(s - m)
    den = jnp.sum(e, axis=-1, keepdims=True)
    return jnp.dot(e.astype(bf16), v, preferred_element_type=jnp.float32) / den


def _nsa_prompt_kernel(q_ref, kc_ref, vc_ref, ks_ref, vs_ref, kw_ref, vw_ref, gn_ref, o_ref, *, tq):
    g = pl.program_id(1)
    qi = pl.program_id(2)
    f32 = jnp.float32
    t = ks_ref.shape[0]
    n_c = kc_ref.shape[0]
    n_s = t // SEL_LEN
    nq = J_N * tq
    q4 = jnp.concatenate([q_ref[:, j * DH_N:(j + 1) * DH_N] for j in range(J_N)], axis=0).astype(bf16)
    nt = (((1,), (1,)), ((), ()))

    def qpos_of(shape):
        return qi * tq + lax.broadcasted_iota(jnp.int32, shape, 0) % tq

    cidx = lax.broadcasted_iota(jnp.int32, (nq, n_c), 1)
    cmask = cidx * CMP_STRIDE + CMP_LEN <= qpos_of((nq, n_c)) + 1
    s = lax.dot_general(q4, kc_ref[...].astype(bf16), nt, preferred_element_type=f32) * SCALE_N
    s = jnp.where(cmask, s, NEG)
    m = jnp.max(s, axis=-1, keepdims=True)
    e = jnp.where(cmask, jnp.exp(s - m), 0.0)
    den = jnp.sum(e, axis=-1, keepdims=True)
    p = e / jnp.where(den > 0.0, den, 1.0)
    o_cmp = jnp.dot(p.astype(bf16), vc_ref[...].astype(bf16), preferred_element_type=f32)
    p_sum = p[0:tq]
    for j in range(1, J_N):
        p_sum = p_sum + p[j * tq:(j + 1) * tq]
    ci = lax.broadcasted_iota(jnp.int32, (n_c, LANES), 0)
    si = lax.broadcasted_iota(jnp.int32, (n_c, LANES), 1)
    cover = jnp.logical_and(ci * CMP_STRIDE < si * SEL_LEN + SEL_LEN,
                            ci * CMP_STRIDE + CMP_LEN > si * SEL_LEN).astype(bf16)
    imp = jnp.zeros((tq, LANES), f32)
    for part in _split3(p_sum):
        imp = imp + jnp.dot(part, cover, preferred_element_type=f32)
    blk = lax.broadcasted_iota(jnp.int32, (tq, LANES), 1)
    cur = (qi * tq + lax.broadcasted_iota(jnp.int32, (tq, LANES), 0)) // SEL_LEN
    valid = blk <= cur
    forced = jnp.logical_or(blk == 0, blk >= cur - 1)
    score = jnp.where(valid, imp + jnp.where(forced, BIG, 0.0), -BIG)
    score = jnp.where(blk < n_s, score, NEG_INF)
    rank = jnp.zeros((tq, LANES), jnp.int32)
    for s2 in range(n_s):
        col = score[:, s2:s2 + 1]
        ahead = jnp.logical_or(col > score, jnp.logical_and(col == score, blk > s2))
        rank = rank + ahead.astype(jnp.int32)
    sel = (rank < N_SEL).astype(bf16)

    ck = SEL_CHUNK
    s_row = lax.broadcasted_iota(jnp.int32, (LANES, ck), 0)
    k_lane = lax.broadcasted_iota(jnp.int32, (LANES, ck), 1)
    kpos1 = lax.broadcasted_iota(jnp.int32, (tq, ck), 1)
    qpos1 = qi * tq + lax.broadcasted_iota(jnp.int32, (tq, ck), 0)

    def kchunk(kb, carry):
        m_o, l_o, acc = carry
        k0 = pl.multiple_of(kb * ck, ck)
        expand = ((k0 + k_lane) // SEL_LEN == s_row).astype(bf16)
        picked = jnp.dot(sel, expand, preferred_element_type=f32)
        keep1 = jnp.logical_and(picked > 0.5, k0 + kpos1 <= qpos1)
        keep = jnp.concatenate([keep1] * J_N, axis=0)
        kk = ks_ref[pl.ds(k0, ck), :].astype(bf16)
        vv = vs_ref[pl.ds(k0, ck), :].astype(bf16)
        sc = lax.dot_general(q4, kk, nt, preferred_element_type=f32) * SCALE_N
        sc = jnp.where(keep, sc, NEG)
        m_n = jnp.maximum(m_o, jnp.max(sc, axis=-1, keepdims=True))
        alpha = jnp.exp(m_o - m_n)
        ee = jnp.where(keep, jnp.exp(sc - m_n), 0.0)
        l_n = alpha * l_o + jnp.sum(ee, axis=-1, keepdims=True)
        return m_n, l_n, alpha * acc + jnp.dot(ee.astype(bf16), vv, preferred_element_type=f32)

    nkc = (qi * tq + tq + ck - 1) // ck
    _, l_f, acc_f = lax.fori_loop(0, nkc, kchunk, (jnp.full((nq, 1), NEG, f32), jnp.zeros((nq, 1), f32),
                                                   jnp.zeros((nq, DH_N), f32)))
    o_sel = acc_f / l_f

    span = WINDOW + tq
    w0 = pl.multiple_of(jnp.clip(qi * tq - WINDOW, 0, t - span), tq)
    kw = kw_ref[pl.ds(w0, span), :].astype(bf16)
    vw = vw_ref[pl.ds(w0, span), :].astype(bf16)
    qpos_w = qpos_of((nq, span))
    kpos_w = w0 + lax.broadcasted_iota(jnp.int32, (nq, span), 1)
    keep_win = jnp.logical_and(kpos_w <= qpos_w, kpos_w > qpos_w - WINDOW)
    sw = lax.dot_general(q4, kw, nt, preferred_element_type=f32) * SCALE_N
    o_win = _softmax_pv(sw, keep_win, vw)

    gates = jax.nn.sigmoid(gn_ref[...])
    gl = lax.broadcasted_iota(jnp.int32, gates.shape, 1)
    outs = []
    for j in range(J_N):
        acc = jnp.zeros((tq, DH_N), f32)
        for which, o in enumerate((o_cmp, o_sel, o_win)):
            col = jnp.sum(jnp.where(gl == ZG_GN + (g * J_N + j) * 3 + which, gates, 0.0), axis=-1, keepdims=True)
            acc = acc + col * o[j * tq:(j + 1) * tq]
        outs.append(acc)
    o_ref[...] = jnp.concatenate(outs, axis=1).astype(o_ref.dtype)


def nsa_prompt(z3, zg3, kcv, tq=128):
    b, l, _ = z3.shape
    assert l >= WINDOW + tq and l % tq == 0 and l % SEL_CHUNK == 0
    nq = l // tq
    n_c = kcv.shape[3]
    full = lambda c0: pl.BlockSpec((None, l, DH_N), lambda bi, g, qi: (bi, 0, c0 // DH_N + g))
    return pl.pallas_call(
        functools.partial(_nsa_prompt_kernel, tq=tq),
        grid=(b, G_N, nq),
        in_specs=[pl.BlockSpec((None, tq, J_N * DH_N), lambda bi, g, qi: (bi, qi, Z_Q // (J_N * DH_N) + g)),
                  pl.BlockSpec((None, None, None, n_c, DH_N), lambda bi, g, qi: (bi, 0, g, 0, 0)),
                  pl.BlockSpec((None, None, None, n_c, DH_N), lambda bi, g, qi: (bi, 1, g, 0, 0)),
                  full(Z_KV + 2 * KVW), full(Z_KV + 3 * KVW), full(Z_WIN), full(Z_WIN + KVW),
                  pl.BlockSpec((None, tq, LANES), lambda bi, g, qi: (bi, qi, 0))],
        out_specs=pl.BlockSpec((None, tq, J_N * DH_N), lambda bi, g, qi: (bi, qi, g)),
        out_shape=jax.ShapeDtypeStruct((b, l, D_N), bf16),
        compiler_params=pltpu.CompilerParams(dimension_semantics=("parallel", "parallel", "arbitrary"),
                                             vmem_limit_bytes=VMEM_LIMIT_BYTES),
        name="nsa_prompt",
    )(z3, kcv, kcv, z3, z3, z3, z3, zg3)


def _paged_compress_kernel(pt_ref, *refs, page, nstep):
    pages = refs[:PAGES_PER_STEP]
    new_ref, pe_ref, w1_ref, w2_ref, o_ref, carry_ref, rows_ref = refs[PAGES_PER_STEP:]
    p = pl.program_id(1)
    nch = PAGES_PER_STEP * page // CMP_STRIDE
    row = lax.broadcasted_iota(jnp.int32, (nch, CMP_HID), 0)

    @pl.when(p < nstep)
    def _():
        for r in range(PAGES_PER_STEP):
            for sg in range(2 * G_N):
                rows_ref[sg, r * page:(r + 1) * page, :] = pages[r][:, sg * DH_N:(sg + 1) * DH_N]
        for s in range(2):
            first = jnp.zeros((G_N * nch, CMP_HID), jnp.float32)
            second = jnp.zeros((G_N * nch, CMP_HID), jnp.float32)
            for k in range(CMP_STRIDE // 2):
                xs = [jnp.concatenate([rows_ref[s * G_N + g, pl.ds(l, nch, stride=CMP_STRIDE), :] for g in range(G_N)],
                                      axis=0) for l in (2 * k, 2 * k + 1)]
                lhs = lambda off: jnp.concatenate(
                    [(xs[i] + pe_ref[s, off + 2 * k + i:off + 2 * k + i + 1, :]).astype(bf16) for i in range(2)], axis=1)
                first += jnp.dot(lhs(0), w1_ref[s, k], preferred_element_type=jnp.float32)
                second += jnp.dot(lhs(CMP_STRIDE), w1_ref[s, CMP_STRIDE // 2 + k], preferred_element_type=jnp.float32)
            for g in range(G_N):
                f_g = first[g * nch:(g + 1) * nch]
                s_g = second[g * nch:(g + 1) * nch]
                prev = carry_ref[s * G_N + g, 0:1, :]
                shifted = jnp.where(row == 0, prev, pltpu.roll(f_g, 1, axis=0))
                carry_ref[s * G_N + g, 0:1, :] = f_g[nch - 1:nch, :]
                hid = _gelu_tanh(shifted + s_g)
                hid = jnp.where(jnp.logical_and(row == 0, p == 0), 0.0, hid)
                o_ref[s, g] = jnp.dot(hid.astype(bf16), w2_ref[s].astype(bf16), preferred_element_type=jnp.float32)

    @pl.when(p == nstep)
    def _():
        for s in range(2):
            for g in range(G_N):
                c0 = (s * G_N + g) * DH_N
                sec = jnp.zeros((QROWS, CMP_HID), jnp.float32)
                for l in range(CMP_STRIDE):
                    x = jnp.broadcast_to(new_ref[l:l + 1, c0:c0 + DH_N] + pe_ref[s, CMP_STRIDE + l:CMP_STRIDE + l + 1, :],
                                         (QROWS, DH_N))
                    w_l = w1_ref[s, (CMP_STRIDE + l) // 2, (l % 2) * DH_N:(l % 2 + 1) * DH_N, :]
                    sec += jnp.dot(x.astype(bf16), w_l, preferred_element_type=jnp.float32)
                hid = _gelu_tanh(carry_ref[s * G_N + g, 0:1, :] + sec)
                blk = jnp.dot(hid.astype(bf16), w2_ref[s].astype(bf16), preferred_element_type=jnp.float32)
                rows = lax.broadcasted_iota(jnp.int32, (nch, DH_N), 0)
                o_ref[s, g] = jnp.where(rows == 0, jnp.broadcast_to(blk[0:1, :], (nch, DH_N)), 0.0)


def paged_compress(page_table, cache2, new_c, cmp_pe, w1_bf, cmp_w2):
    b, npages = page_table.shape
    page = cache2.shape[1]
    assert page % CMP_STRIDE == 0 and npages % PAGES_PER_STEP == 0 and new_c.shape[1] == NEW_ROWS == CMP_STRIDE
    nstep = npages // PAGES_PER_STEP
    nch = PAGES_PER_STEP * page // CMP_STRIDE

    def page_spec(r):
        return pl.BlockSpec((None, page, 2 * KVW),
                            lambda bi, p, pt: (pt[bi, jnp.minimum(p * PAGES_PER_STEP + r, npages - 1)], 0, 0))

    grid_spec = pltpu.PrefetchScalarGridSpec(
        num_scalar_prefetch=1,
        grid=(b, nstep + 1),
        in_specs=[page_spec(r) for r in range(PAGES_PER_STEP)] + [
            pl.BlockSpec((None, NEW_ROWS, 2 * KVW), lambda bi, p, pt: (bi, 0, 0)),
            pl.BlockSpec((2, CMP_LEN, DH_N), lambda bi, p, pt: (0, 0, 0)),
            pl.BlockSpec((2, CMP_LEN // 2, 2 * DH_N, CMP_HID), lambda bi, p, pt: (0, 0, 0, 0)),
            pl.BlockSpec((2, CMP_HID, DH_N), lambda bi, p, pt: (0, 0, 0))],
        out_specs=pl.BlockSpec((None, 2, G_N, nch, DH_N), lambda bi, p, pt: (bi, 0, 0, p, 0)),
        scratch_shapes=[pltpu.VMEM((2 * G_N, QROWS, CMP_HID), jnp.float32),
                        pltpu.VMEM((2 * G_N, PAGES_PER_STEP * page, DH_N), jnp.float32)])
    return pl.pallas_call(
        functools.partial(_paged_compress_kernel, page=page, nstep=nstep),
        grid_spec=grid_spec,
        out_shape=jax.ShapeDtypeStruct((b, 2, G_N, (nstep + 1) * nch, DH_N), jnp.float32),
        compiler_params=pltpu.CompilerParams(dimension_semantics=("parallel", "arbitrary"),
                                             vmem_limit_bytes=VMEM_LIMIT_BYTES),
        name="nsa_paged_compress",
    )(page_table, *([cache2] * PAGES_PER_STEP), new_c, cmp_pe, w1_bf, cmp_w2)


def _paged_attend_kernel(pt_ref, *refs, page, nstep, l_true):
    pages = refs[:PAGES_PER_STEP]
    (q_ref, kcv_ref, new_ref, winp_ref, wnew_ref, zg_ref, o_ref,
     sel_ref, ocmp_ref, m_ref, l_ref, acc_ref) = refs[PAGES_PER_STEP:]
    p = pl.program_id(1)
    past = nstep * PAGES_PER_STEP * page
    tk = PAGES_PER_STEP * page
    n_r = kcv_ref.shape[2]
    n_s = -(-(past + l_true) // SEL_LEN)
    sl = 2 * LANES
    assert n_s <= sl
    nq = J_N * QROWS
    f32 = jnp.float32
    nt = (((1,), (1,)), ((), ()))

    def q_of(g):
        return jnp.concatenate([q_ref[:, (g * J_N + j) * DH_N:(g * J_N + j + 1) * DH_N] for j in range(J_N)],
                               axis=0).astype(bf16)

    def pos_of(shape):
        return lax.broadcasted_iota(jnp.int32, shape, 0) % QROWS

    def online(g, s, keep, v):
        s = jnp.where(keep, s, NEG)
        m_old = m_ref[g]
        m_new = jnp.maximum(m_old, jnp.max(s, axis=-1, keepdims=True))
        alpha = jnp.exp(m_old - m_new)
        e = jnp.where(keep, jnp.exp(s - m_new), 0.0)
        l_ref[g] = alpha * l_ref[g] + jnp.sum(e, axis=-1, keepdims=True)
        acc_ref[g] = alpha * acc_ref[g] + jnp.dot(e.astype(bf16), v, preferred_element_type=f32)
        m_ref[g] = m_new

    @pl.when(p == 0)
    def _():
        r_i = lax.broadcasted_iota(jnp.int32, (nq, n_r), 1)
        qpos = past + pos_of((nq, n_r))
        cmask = jnp.logical_and(r_i >= 1, (r_i - 1) * CMP_STRIDE + CMP_LEN <= qpos + 1)
        ci = lax.broadcasted_iota(jnp.int32, (n_r, sl), 0) - 1
        si = lax.broadcasted_iota(jnp.int32, (n_r, sl), 1)
        cover = jnp.logical_and(jnp.logical_and(ci >= 0, ci * CMP_STRIDE < si * SEL_LEN + SEL_LEN),
                                ci * CMP_STRIDE + CMP_LEN > si * SEL_LEN).astype(bf16)
        blk = lax.broadcasted_iota(jnp.int32, (QROWS, sl), 1)
        cur = (past + lax.broadcasted_iota(jnp.int32, (QROWS, sl), 0)) // SEL_LEN
        valid = blk <= cur
        forced = jnp.logical_or(blk == 0, blk >= cur - 1)
        s_src = lax.broadcasted_iota(jnp.int32, (sl, sl), 0)
        s_dst = lax.broadcasted_iota(jnp.int32, (sl, sl), 1)
        for g in range(G_N):
            q = q_of(g)
            kc = kcv_ref[0, g].astype(bf16)
            vc = kcv_ref[1, g].astype(bf16)
            s = lax.dot_general(q, kc, nt, preferred_element_type=f32) * SCALE_N
            s = jnp.where(cmask, s, NEG)
            m = jnp.max(s, axis=-1, keepdims=True)
            e = jnp.where(cmask, jnp.exp(s - m), 0.0)
            den = jnp.sum(e, axis=-1, keepdims=True)
            pr = e / jnp.where(den > 0.0, den, 1.0)
            ocmp_ref[g] = jnp.dot(pr.astype(bf16), vc, preferred_element_type=f32)
            p_sum = pr[0:QROWS]
            for j in range(1, J_N):
                p_sum = p_sum + pr[j * QROWS:(j + 1) * QROWS]
            imp = jnp.zeros((QROWS, sl), f32)
            for part in _split3(p_sum):
                imp = imp + jnp.dot(part, cover, preferred_element_type=f32)
            score = jnp.where(valid, imp + jnp.where(forced, BIG, 0.0), -BIG)
            score = jnp.where(blk < n_s, score, NEG_INF)
            score_t = score.T
            sel_rows = []
            for qi in range(QROWS):
                col = score_t[:, qi:qi + 1]
                rw = score[qi:qi + 1, :]
                ahead = jnp.logical_or(col > rw, jnp.logical_and(col == rw, s_src < s_dst))
                rank = jnp.sum(ahead.astype(jnp.int32), axis=0, keepdims=True)
                sel_rows.append((rank < N_SEL).astype(f32))
            sel_ref[g] = jnp.concatenate(sel_rows, axis=0)
            m_ref[g] = jnp.full((nq, 1), NEG, f32)
            l_ref[g] = jnp.zeros((nq, 1), f32)
            acc_ref[g] = jnp.zeros((nq, DH_N), f32)

    @pl.when(p < nstep)
    def _():
        k0 = p * tk
        expand = ((k0 + lax.broadcasted_iota(jnp.int32, (sl, tk), 1)) // SEL_LEN
                  == lax.broadcasted_iota(jnp.int32, (sl, tk), 0)).astype(bf16)
        for g in range(G_N):
            picked = jnp.dot(sel_ref[g].astype(bf16), expand, preferred_element_type=f32)
            keep = jnp.concatenate([picked] * J_N, axis=0) > 0.5
            kk = jnp.concatenate([pages[r][:, g * DH_N:(g + 1) * DH_N] for r in range(PAGES_PER_STEP)],
                                 axis=0).astype(bf16)
            vv = jnp.concatenate([pages[r][:, KVW + g * DH_N:KVW + (g + 1) * DH_N] for r in range(PAGES_PER_STEP)],
                                 axis=0).astype(bf16)
            s = lax.dot_general(q_of(g), kk, nt, preferred_element_type=f32) * SCALE_N
            online(g, s, keep, vv)

    @pl.when(p == nstep)
    def _():
        gates = jax.nn.sigmoid(zg_ref[...])
        gl = lax.broadcasted_iota(jnp.int32, gates.shape, 1)
        pos_n = pos_of((nq, NEW_ROWS))
        l_n = lax.broadcasted_iota(jnp.int32, (nq, NEW_ROWS), 1)
        keep_new = jnp.logical_and(l_n <= pos_n, l_n < l_true)
        wbuf = winp_ref.shape[0]
        pos_w = pos_of((nq, wbuf + NEW_ROWS))
        i_w = lax.broadcasted_iota(jnp.int32, (nq, wbuf + NEW_ROWS), 1)
        keep_win = jnp.logical_or(
            jnp.logical_and(i_w < wbuf, i_w - wbuf > pos_w - WINDOW),
            jnp.logical_and(i_w >= wbuf, jnp.logical_and(i_w - wbuf <= pos_w, i_w - wbuf < l_true)))
        for g in range(G_N):
            q = q_of(g)
            kn = new_ref[:, g * DH_N:(g + 1) * DH_N].astype(bf16)
            vn = new_ref[:, KVW + g * DH_N:KVW + (g + 1) * DH_N].astype(bf16)
            s = lax.dot_general(q, kn, nt, preferred_element_type=f32) * SCALE_N
            online(g, s, keep_new, vn)
            o_sel = acc_ref[g] / l_ref[g]
            kw = jnp.concatenate([winp_ref[:, g * DH_N:(g + 1) * DH_N], wnew_ref[:, g * DH_N:(g + 1) * DH_N]],
                                 axis=0).astype(bf16)
            vw = jnp.concatenate([winp_ref[:, KVW + g * DH_N:KVW + (g + 1) * DH_N],
                                  wnew_ref[:, KVW + g * DH_N:KVW + (g + 1) * DH_N]], axis=0).astype(bf16)
            s = lax.dot_general(q, kw, nt, preferred_element_type=f32) * SCALE_N
            o_win = _softmax_pv(s, keep_win, vw)
            o_cmp = ocmp_ref[g]
            for j in range(J_N):
                acc = jnp.zeros((QROWS, DH_N), f32)
                for which, o in enumerate((o_cmp, o_sel, o_win)):
                    col = jnp.sum(jnp.where(gl == ZG_GN + (g * J_N + j) * 3 + which, gates, 0.0), axis=-1, keepdims=True)
                    acc = acc + col * o[j * QROWS:(j + 1) * QROWS]
                o_ref[:, (g * J_N + j) * DH_N:(g * J_N + j + 1) * DH_N] = acc.astype(o_ref.dtype)


def paged_attend(page_table, cache2, q8, kcv, new_s, winp, wnew, zg8, l_true):
    b, npages = page_table.shape
    page = cache2.shape[1]
    nstep = npages // PAGES_PER_STEP
    assert winp.shape[1] == WINDOW and l_true <= QROWS and npages % PAGES_PER_STEP == 0
    n_r = kcv.shape[3]
    nq = J_N * QROWS

    def page_spec(r):
        return pl.BlockSpec((None, page, 2 * KVW),
                            lambda bi, p, pt: (pt[bi, jnp.minimum(p * PAGES_PER_STEP + r, npages - 1)], 0, 1))

    per_b = lambda rows, cols: pl.BlockSpec((None, rows, cols), lambda bi, p, pt: (bi, 0, 0))
    grid_spec = pltpu.PrefetchScalarGridSpec(
        num_scalar_prefetch=1,
        grid=(b, nstep + 1),
        in_specs=[page_spec(r) for r in range(PAGES_PER_STEP)] + [
            per_b(QROWS, D_N),
            pl.BlockSpec((None, 2, G_N, n_r, DH_N), lambda bi, p, pt: (bi, 0, 0, 0, 0)),
            per_b(NEW_ROWS, 2 * KVW), per_b(WINDOW, 2 * KVW), per_b(NEW_ROWS, 2 * KVW), per_b(QROWS, LANES)],
        out_specs=per_b(QROWS, D_N),
        scratch_shapes=[pltpu.VMEM((G_N, QROWS, 2 * LANES), jnp.float32),
                        pltpu.VMEM((G_N, nq, DH_N), jnp.float32),
                        pltpu.VMEM((G_N, nq, 1), jnp.float32),
                        pltpu.VMEM((G_N, nq, 1), jnp.float32),
                        pltpu.VMEM((G_N, nq, DH_N), jnp.float32)])
    return pl.pallas_call(
        functools.partial(_paged_attend_kernel, page=page, nstep=nstep, l_true=l_true),
        grid_spec=grid_spec,
        out_shape=jax.ShapeDtypeStruct((b, QROWS, D_N), jnp.float32),
        compiler_params=pltpu.CompilerParams(dimension_semantics=("parallel", "arbitrary"),
                                             vmem_limit_bytes=VMEM_LIMIT_BYTES),
        name="nsa_paged_attend",
    )(page_table, *([cache2] * PAGES_PER_STEP), q8, kcv, new_s, winp, wnew, zg8)


def _extract_top(s, k):
    rows = lax.broadcasted_iota(jnp.int32, s.shape, 0)
    nrow = s.shape[0]
    work = s
    taken = jnp.zeros(s.shape, jnp.bool_)
    tops = []
    for _ in range(k):
        m = jnp.max(work, axis=0, keepdims=True)
        first = jnp.min(jnp.where(work == m, rows, nrow), axis=0, keepdims=True)
        hit = rows == first
        taken = jnp.logical_or(taken, hit)
        work = jnp.where(hit, NEG_INF, work)
        tops.append(m)
    return tops, taken


def _peer_route_kernel(qt_ref, keys_ref, s1_ref, e1_ref, s2_ref, e2_ref, tau_ref):
    halves = []
    for c in range(2):
        q = qt_ref[c * PEER_QHALF:(c + 1) * PEER_QHALF, :].astype(bf16)
        s = jnp.dot(keys_ref[c].astype(bf16), q, preferred_element_type=jnp.float32)
        tops, taken = _extract_top(s, PEER_TOPK)
        halves.append((s, tops, taken))
    (sa, ta, ma), (sb, tb, mb) = halves
    assert PEER_TOPK == 16
    ta_col = jnp.concatenate(ta, axis=0)
    tb_col = jnp.concatenate(tb, axis=0)
    a_idx = lax.broadcasted_iota(jnp.int32, (8, ta_col.shape[1]), 0)
    cand = jnp.concatenate(
        [ta[0] + tb_col, ta[1] + tb_col, ta[2] + tb_col[0:8], ta[3] + tb_col[0:8]]
        + [jnp.where(a_idx >= 4, ta_col[0:8] + tb[b], NEG_INF) for b in range(3)]
        + [ta_col[8:16] + tb[0]], axis=0)
    ctops, _ = _extract_top(cand, PEER_TOPK)
    cmax = ctops[0]
    z = jnp.exp(ctops[0] - cmax)
    for r in range(1, PEER_TOPK):
        z = z + jnp.exp(ctops[r] - cmax)
    tau_ref[...] = ctops[PEER_TOPK - 1]
    s1_ref[...] = jnp.where(ma, sa, NEG_INF)
    s2_ref[...] = jnp.where(mb, sb, NEG_INF)
    e1_ref[...] = jnp.where(ma, jnp.exp(sa - ta[0]), 0.0) / z
    e2_ref[...] = jnp.where(mb, jnp.exp(sb - tb[0]), 0.0)


def peer_route(qt, pkeys, tb):
    n = qt.shape[1]
    big = jax.ShapeDtypeStruct((PEER_HEADS, N_KEYS, n), jnp.float32)
    bspec = pl.BlockSpec((None, N_KEYS, tb), lambda i, h: (h, 0, i))
    return pl.pallas_call(
        _peer_route_kernel,
        grid=(n // tb, PEER_HEADS),
        in_specs=[pl.BlockSpec((2 * PEER_QHALF, tb), lambda i, h: (h, i)),
                  pl.BlockSpec((None, 2, N_KEYS, PEER_QHALF), lambda i, h: (h, 0, 0, 0))],
        out_specs=[bspec, bspec, bspec, bspec, pl.BlockSpec((None, 1, tb), lambda i, h: (h, 0, i))],
        out_shape=[big, big, big, big, jax.ShapeDtypeStruct((PEER_HEADS, 1, n), jnp.float32)],
        compiler_params=pltpu.CompilerParams(dimension_semantics=("parallel", "parallel")),
        name="peer_route",
    )(qt, pkeys)


def _peer_dense_kernel(xt_ref, u_ref, vt_ref, s1_ref, e1_ref, s2_ref, e2_ref, tau_ref, o_ref, gcur_ref, gnext_ref,
                       row_ref, *, sub):
    j = pl.program_id(1)
    nj = pl.num_programs(1)

    def gate_tile(jt):
        for a in range(sub):
            for h in range(PEER_HEADS):
                row_ref[0, a * PEER_HEADS + h:a * PEER_HEADS + h + 1, :] = s1_ref[h, pl.ds(jt * sub + a, 1), :]
                row_ref[1, a * PEER_HEADS + h:a * PEER_HEADS + h + 1, :] = e1_ref[h, pl.ds(jt * sub + a, 1), :]
        for st in range(xt_ref.shape[1] // LANES):
            ln = slice(st * LANES, (st + 1) * LANES)
            for kh in range(N_KEYS // GATE_ROWS):
                ks = slice(kh * GATE_ROWS, (kh + 1) * GATE_ROWS)
                g = [None] * sub
                for h in range(PEER_HEADS):
                    s2 = s2_ref[h, ks, ln]
                    e2 = e2_ref[h, ks, ln]
                    tau = tau_ref[h, :, ln]
                    for a in range(sub):
                        r = a * PEER_HEADS + h
                        t = jnp.where(s2 + row_ref[0, r:r + 1, ln] >= tau, e2, 0.0) * row_ref[1, r:r + 1, ln]
                        g[a] = t if g[a] is None else g[a] + t
                for a in range(sub):
                    gnext_ref[a * N_KEYS + kh * GATE_ROWS:a * N_KEYS + (kh + 1) * GATE_ROWS, ln] = g[a]

    @pl.when(j == 0)
    def _():
        o_ref[...] = jnp.zeros_like(o_ref)
        gate_tile(0)

    gcur_ref[...] = gnext_ref[...]
    gate_tile(jnp.minimum(j + 1, nj - 1))
    ht = jnp.dot(u_ref[...], xt_ref[...], preferred_element_type=jnp.float32)
    act = (_gelu_tanh(ht) * gcur_ref[...]).astype(bf16)
    o_ref[...] += jnp.dot(vt_ref[...], act, preferred_element_type=jnp.float32)


def peer_dense(xt, u_bf, vt_bf, s1, e1, s2, e2, tau, tb, sub):
    d, n = xt.shape
    e = u_bf.shape[0]
    te = sub * N_KEYS
    once = dict(pipeline_mode=pl.Buffered(1))
    rspec = pl.BlockSpec((PEER_HEADS, N_KEYS, tb), lambda i, j: (0, 0, i), **once)
    return pl.pallas_call(
        functools.partial(_peer_dense_kernel, sub=sub),
        grid=(n // tb, e // te),
        in_specs=[pl.BlockSpec((d, tb), lambda i, j: (0, i), **once),
                  pl.BlockSpec((te, d), lambda i, j: (j, 0)),
                  pl.BlockSpec((d, te), lambda i, j: (0, j)),
                  rspec, rspec, rspec, rspec,
                  pl.BlockSpec((PEER_HEADS, 1, tb), lambda i, j: (0, 0, i), **once)],
        out_specs=pl.BlockSpec((d, tb), lambda i, j: (0, i)),
        out_shape=jax.ShapeDtypeStruct((d, n), jnp.float32),
        scratch_shapes=[pltpu.VMEM((te, tb), jnp.float32), pltpu.VMEM((te, tb), jnp.float32),
                        pltpu.VMEM((2, sub * PEER_HEADS, tb), jnp.float32)],
        compiler_params=pltpu.CompilerParams(dimension_semantics=("parallel", "arbitrary"),
                                             vmem_limit_bytes=BIG_VMEM_LIMIT_BYTES),
        name="peer_dense",
    )(xt, u_bf, vt_bf, s1, e1, s2, e2, tau)


def peer_ffn_t(xn_bf, w_pq_t, pkeys, u_bf, vt_bf):
    n = xn_bf.shape[0]
    xt = xn_bf.T
    qt = pmm(w_pq_t, xt)
    s1, e1, s2, e2, tau = peer_route(qt, pkeys, _pick(n, (256, 128)))
    return peer_dense(xt, u_bf, vt_bf, s1, e1, s2, e2, tau, _pick(n, (512, 256, 128)), PEER_SUB)


def _in_proj_weights(w_in):
    o = _IN_OFFS
    w_main = jnp.concatenate([w_in[:, o[0]:o[3]], w_in[:, o[5]:o[8]], w_in[:, o[9]:o[11]]], axis=1).astype(bf16)
    w_gate = jnp.concatenate([w_in[:, o[3]:o[5]], w_in[:, o[8]:o[9]],
                              jnp.zeros((w_in.shape[0], LANES - 2 * NH_M - 3 * H_N), w_in.dtype)], axis=1).astype(bf16)
    return w_main, w_gate


def kernel(x_prompt, x_sample, cache_kv, state_win_kv, state_conv, state_C, state_n, state_m, page_table,
           norm_mix, norm_ffn, norm_final, w_in, w_conv, b_conv, w_bd, b_gates, w_hnorm, w_skip,
           cmp_pe, cmp_w1, cmp_w2, w_br_m, w_br_n, w_out, w_pq, peer_keys, peer_u, peer_v):
    assert w_in.shape[0] == DEPTH == 1
    f32 = jnp.float32
    l = 0
    Bp, Lp_, _ = x_prompt.shape
    Bs, Ls, _ = x_sample.shape
    w_main, w_gate = _in_proj_weights(w_in[l])
    wbd_full = expand_blockdiag(w_bd[l])
    gate_bias = jnp.concatenate([b_gates[l, 0], b_gates[l, 1], jnp.zeros((LANES - 2 * NH_M,), f32)])[None]
    w_br_m_bf, w_br_n_bf, w_out_bf = w_br_m[l].astype(bf16), w_br_n[l].astype(bf16), w_out[l].astype(bf16)
    w_pq_t = w_pq[l].T.astype(bf16)
    u_bf = peer_u[l].astype(bf16)
    vt_bf = peer_v[l].T.astype(bf16)

    def mixers(x, conv_prev, C0, n0, m0, nsa_fn, ch, lpad):
        B, L, _ = x.shape
        n = B * L
        x2 = x.reshape(n, D_MODEL)
        xn = rmsnorm_rows(x2, norm_mix[l], bf16)
        z2 = pmm(xn, w_main)
        zg2 = pmm(xn, w_gate)
        z3 = z2.reshape(B, L, Z_COLS)
        zg3 = zg2.reshape(B, L, LANES)
        xm = z3[:, :, Z_XM:Z_XM + D_M]
        new_conv = jnp.concatenate([conv_prev, xm], axis=1)[:, L:]
        prev8 = jnp.pad(conv_prev, ((0, 0), (PREV_ROWS - (CONV_W - 1), 0), (0, 0)))
        if lpad == L:
            zm3, zgm3, cols = z3, zg3, (Z_XM // DH_M, Z_VM // DH_M, Z_OM // DH_M)
        else:
            zm3 = jnp.pad(z3[:, :, :Z_Q], ((0, 0), (0, lpad - L), (0, 0)))
            zgm3 = jnp.pad(zg3, ((0, 0), (0, lpad - L), (0, 0)))
            cols = (Z_XM // DH_M, Z_VM // DH_M, Z_OM // DH_M)
        y_m, C1, n1, m1 = mlstm_branch(zm3, zgm3, cols, prev8, w_conv[l], b_conv[l], wbd_full, gate_bias,
                                       C0, n0[:, :, None, :], jnp.broadcast_to(m0[:, :, None, None], (B, NH_M, 1, LANES)),
                                       w_hnorm[l], w_skip[l], L, ch)
        y_m = y_m[:, :L].reshape(n, D_M)
        kv_new = z3[:, :, Z_KV:Z_KV + N_KV_SETS * KVW].reshape(B, L, N_KV_SETS, G_N, DH_N)
        win_new = z3[:, :, Z_WIN:Z_WIN + 2 * KVW].reshape(B, L, 2, G_N, DH_N)
        y_n, new_win = nsa_fn(z3, zg3, kv_new, win_new)
        merged = merge_branches(y_m, y_n.reshape(n, D_N).astype(bf16), w_br_m_bf, w_br_n_bf, z2)
        h2 = out_proj_residual(merged, w_out_bf, x2)
        return h2, kv_new, new_win, new_conv, C1, n1[:, :, 0], m1[:, :, 0, 0]

    def nsa_p(z3, zg3, kv_new, win_new):
        kcv = compress_prompt(z3, Z_KV // DH_N, cmp_pe[l], cmp_w1[l], cmp_w2[l])
        return nsa_prompt(z3, zg3, kcv), win_new[:, -min(WINDOW, z3.shape[1]):]

    def nsa_s(z3, zg3, kv_new, win_new):
        L = z3.shape[1]
        padr = lambda t, rows: jnp.pad(t, ((0, 0), (0, rows - t.shape[1]), (0, 0)))
        cache2 = cache_kv.reshape(DEPTH * cache_kv.shape[1], cache_kv.shape[2], N_KV_SETS * KVW)
        win_state = state_win_kv.reshape(Bs, state_win_kv.shape[2], 2 * KVW)
        kcv = paged_compress(page_table, cache2, padr(z3[:, :, Z_KV:Z_KV + 2 * KVW], NEW_ROWS),
                             cmp_pe[l], cmp_w1[l].astype(bf16).reshape(2, CMP_LEN // 2, 2 * DH_N, CMP_HID), cmp_w2[l])
        y8 = paged_attend(page_table, cache2, padr(z3[:, :, Z_Q:Z_Q + D_N], QROWS), kcv,
                          padr(z3[:, :, Z_KV + 2 * KVW:Z_KV + 4 * KVW], NEW_ROWS),
                          win_state, padr(z3[:, :, Z_WIN:Z_WIN + 2 * KVW], NEW_ROWS), padr(zg3, QROWS), L)
        new_win = jnp.concatenate([win_state[:, L:], z3[:, :, Z_WIN:Z_WIN + 2 * KVW]], axis=1)
        return y8[:, :L], new_win.reshape(Bs, -1, 2, G_N, DH_N)

    hp, kvp, winp, convp, Cp, n_p, m_p = mixers(
        x_prompt, jnp.zeros((Bp, CONV_W - 1, D_M), f32), jnp.zeros((Bp, NH_M, DH_M, DH_M), f32),
        jnp.zeros((Bp, NH_M, DH_M), f32), jnp.zeros((Bp, NH_M), f32), nsa_p, 256, Lp_)
    hs, kvs, wins, convs, Cs, n_s, m_s = mixers(
        x_sample, state_conv.reshape(state_conv.shape[1:]), state_C.reshape(state_C.shape[1:]),
        state_n.reshape(state_n.shape[1:]), state_m.reshape(state_m.shape[1:]), nsa_s, 16, 16)

    def ffn_and_norm(h2):
        xn2 = rmsnorm_rows(h2, norm_ffn[l], bf16)
        peer_t = peer_ffn_t(xn2, w_pq_t, peer_keys[l], u_bf, vt_bf)
        return final_norm(h2, peer_t, norm_final)

    y_prompt = ffn_and_norm(hp).reshape(x_prompt.shape)
    y_sample = ffn_and_norm(hs).reshape(x_sample.shape)
    st = lambda t: t[None]
    return (y_prompt, y_sample, st(kvp), st(kvs), st(winp), st(wins), st(convp), st(convs),
            st(Cp), st(Cs), st(n_p), st(n_s), st(m_p), st(m_s))
```

```python
import functools

import jax
import jax.numpy as jnp
import numpy as np
from jax import lax
from jax.experimental import pallas as pl
from jax.experimental.pallas import tpu as pltpu

D_MODEL = 4096
DEPTH = 1
D_M = D_MODEL // 2
DH_M = 256
NH_M = D_M // DH_M
CONV_W = 4
QKV_BLOCK = 4
D_N = D_MODEL // 2
DH_N = 128
H_N = D_N // DH_N
G_N = 4
J_N = H_N // G_N
KVW = G_N * DH_N
N_KV_SETS = 4
CMP_STRIDE = 16
CMP_LEN = 2 * CMP_STRIDE
CMP_HID = 2 * DH_N
SEL_LEN = 64
N_SEL = 16
WINDOW = 512
SCALE_N = DH_N ** -0.5
PEER_HEADS = 8
N_KEYS = 128
PEER_TOPK = 16
PEER_QDIM = 256
PEER_QHALF = PEER_QDIM // 2
IN_SIZES = (D_M, D_M, D_M, NH_M, NH_M, D_N, N_KV_SETS * KVW, 2 * KVW, 3 * H_N, D_MODEL, D_MODEL)
EPS = 1e-6
NEG = -1e30
BIG = 1e9
NEG_INF = float('-inf')

LANES = 128
VMEM_LIMIT_BYTES = 48 * 1024 * 1024
BIG_VMEM_LIMIT_BYTES = 56 * 1024 * 1024
PEER_SUB = 4
PREV_ROWS = 8
PAGES_PER_STEP = 16
NEW_ROWS = 16
QROWS = 8
SEL_CHUNK = 512
bf16 = jnp.bfloat16

_IN_OFFS = np.concatenate([[0], np.cumsum(IN_SIZES)]).tolist()
Z_XM, Z_VM, Z_OM = 0, D_M, 2 * D_M
Z_Q = 3 * D_M
Z_KV = Z_Q + D_N
Z_WIN = Z_KV + N_KV_SETS * KVW
Z_GM = Z_WIN + 2 * KVW
Z_GN = Z_GM + D_MODEL
Z_COLS = Z_GN + D_MODEL
ZG_IG, ZG_FG, ZG_GN = 0, NH_M, 2 * NH_M


def _pick(n, cands):
    for c in cands:
        if n % c == 0:
            return c
    return n


def _gelu_tanh(x):
    return 0.5 * x * (1.0 + jnp.tanh(np.float32(np.sqrt(2.0 / np.pi)) * (x + np.float32(0.044715) * (x * x * x))))


def _split3(x):
    hi = x.astype(bf16)
    r = x - hi.astype(jnp.float32)
    mid = r.astype(bf16)
    lo = (r - mid.astype(jnp.float32)).astype(bf16)
    return hi, mid, lo


def _rmsnorm_kernel(x_ref, g_ref, o_ref):
    x = x_ref[...]
    o_ref[...] = (x * lax.rsqrt(jnp.mean(x * x, axis=-1, keepdims=True) + EPS) * g_ref[...]).astype(o_ref.dtype)


def rmsnorm_rows(x2, g, out_dtype):
    n, d = x2.shape
    tr = _pick(n, (256, 128))
    return pl.pallas_call(
        _rmsnorm_kernel,
        grid=(n // tr,),
        in_specs=[pl.BlockSpec((tr, d), lambda i: (i, 0)), pl.BlockSpec((1, d), lambda i: (0, 0))],
        out_specs=pl.BlockSpec((tr, d), lambda i: (i, 0)),
        out_shape=jax.ShapeDtypeStruct((n, d), out_dtype),
        compiler_params=pltpu.CompilerParams(dimension_semantics=("parallel",), vmem_limit_bytes=VMEM_LIMIT_BYTES),
        name="rmsnorm",
    )(x2, g.reshape(1, d))


def _rmsnorm_t_kernel(x_ref, g_ref, o_ref):
    x = x_ref[...]
    o_ref[...] = (x * lax.rsqrt(jnp.mean(x * x, axis=-1, keepdims=True) + EPS) * g_ref[...]).T.astype(o_ref.dtype)


def rmsnorm_rows_t(x2, g, out_dtype):
    n, d = x2.shape
    tr = _pick(n, (256, 128))
    return pl.pallas_call(
        _rmsnorm_t_kernel,
        grid=(n // tr,),
        in_specs=[pl.BlockSpec((tr, d), lambda i: (i, 0)), pl.BlockSpec((1, d), lambda i: (0, 0))],
        out_specs=pl.BlockSpec((d, tr), lambda i: (0, i)),
        out_shape=jax.ShapeDtypeStruct((d, n), out_dtype),
        compiler_params=pltpu.CompilerParams(dimension_semantics=("parallel",), vmem_limit_bytes=VMEM_LIMIT_BYTES),
        name="rmsnorm_t",
    )(x2, g.reshape(1, d))


def _mm_kernel(a_ref, b_ref, o_ref):
    o_ref[...] = jnp.dot(a_ref[...], b_ref[...], preferred_element_type=jnp.float32).astype(o_ref.dtype)


def pmm(a, b, out_dtype=jnp.float32):
    m, kd = a.shape
    n = b.shape[1]
    tm = _pick(m, (1024, 512, 256, 128))
    tn = _pick(n, (1024, 512, 256, 128))
    return pl.pallas_call(
        _mm_kernel,
        grid=(m // tm, n // tn),
        in_specs=[pl.BlockSpec((tm, kd), lambda i, j: (i, 0)), pl.BlockSpec((kd, tn), lambda i, j: (0, j))],
        out_specs=pl.BlockSpec((tm, tn), lambda i, j: (i, j)),
        out_shape=jax.ShapeDtypeStruct((m, n), out_dtype),
        compiler_params=pltpu.CompilerParams(dimension_semantics=("parallel", "parallel"),
                                             vmem_limit_bytes=BIG_VMEM_LIMIT_BYTES),
        name="proj",
    )(a, b)


def _merge_kernel(ym_ref, yn_ref, wm_ref, wn_ref, gm_ref, gn_ref, o_ref):
    pm = jnp.dot(ym_ref[...], wm_ref[...], preferred_element_type=jnp.float32)
    pn = jnp.dot(yn_ref[...], wn_ref[...], preferred_element_type=jnp.float32)
    o_ref[...] = (jax.nn.sigmoid(gm_ref[...]) * pm + jax.nn.sigmoid(gn_ref[...]) * pn).astype(o_ref.dtype)


def merge_branches(ym, yn, wm, wn, z2):
    m = ym.shape[0]
    n = wm.shape[1]
    tm = _pick(m, (1024, 512, 256, 128))
    tn = 512
    return pl.pallas_call(
        _merge_kernel,
        grid=(m // tm, n // tn),
        in_specs=[pl.BlockSpec((tm, D_M), lambda i, j: (i, 0)), pl.BlockSpec((tm, D_N), lambda i, j: (i, 0)),
                  pl.BlockSpec((D_M, tn), lambda i, j: (0, j)), pl.BlockSpec((D_N, tn), lambda i, j: (0, j)),
                  pl.BlockSpec((tm, tn), lambda i, j: (i, Z_GM // tn + j)),
                  pl.BlockSpec((tm, tn), lambda i, j: (i, Z_GN // tn + j))],
        out_specs=pl.BlockSpec((tm, tn), lambda i, j: (i, j)),
        out_shape=jax.ShapeDtypeStruct((m, n), bf16),
        compiler_params=pltpu.CompilerParams(dimension_semantics=("parallel", "parallel"),
                                             vmem_limit_bytes=VMEM_LIMIT_BYTES),
        name="merge_branches",
    )(ym, yn, wm, wn, z2, z2)


def _mm_res_kernel(a_ref, b_ref, r_ref, o_ref):
    o_ref[...] = r_ref[...] + jnp.dot(a_ref[...], b_ref[...], preferred_element_type=jnp.float32)


def out_proj_residual(a, b, r):
    m, kd = a.shape
    n = b.shape[1]
    tm = _pick(m, (1024, 512, 256, 128))
    tn = 512
    return pl.pallas_call(
        _mm_res_kernel,
        grid=(m // tm, n // tn),
        in_specs=[pl.BlockSpec((tm, kd), lambda i, j: (i, 0)), pl.BlockSpec((kd, tn), lambda i, j: (0, j)),
                  pl.BlockSpec((tm, tn), lambda i, j: (i, j))],
        out_specs=pl.BlockSpec((tm, tn), lambda i, j: (i, j)),
        out_shape=jax.ShapeDtypeStruct((m, n), jnp.float32),
        compiler_params=pltpu.CompilerParams(dimension_semantics=("parallel", "parallel"),
                                             vmem_limit_bytes=VMEM_LIMIT_BYTES),
        name="out_proj",
    )(a, b, r)


def _final_kernel(h_ref, pt_ref, g_ref, o_ref):
    x = h_ref[...] + pt_ref[...].T
    o_ref[...] = x * lax.rsqrt(jnp.mean(x * x, axis=-1, keepdims=True) + EPS) * g_ref[...]


def final_norm(h2, peer_t, g):
    n, d = h2.shape
    tr = _pick(n, (256, 128))
    return pl.pallas_call(
        _final_kernel,
        grid=(n // tr,),
        in_specs=[pl.BlockSpec((tr, d), lambda i: (i, 0)), pl.BlockSpec((d, tr), lambda i: (0, i)),
                  pl.BlockSpec((1, d), lambda i: (0, 0))],
        out_specs=pl.BlockSpec((tr, d), lambda i: (i, 0)),
        out_shape=jax.ShapeDtypeStruct((n, d), jnp.float32),
        compiler_params=pltpu.CompilerParams(dimension_semantics=("parallel",), vmem_limit_bytes=VMEM_LIMIT_BYTES),
        name="final_norm",
    )(h2, peer_t, g.reshape(1, d))


def _blockdiag_apply(x, w_ref, which):
    parts = [jnp.dot(x[:, hb * LANES:(hb + 1) * LANES].astype(bf16), w_ref[which, hb].astype(bf16),
                     preferred_element_type=jnp.float32) for hb in range(DH_M // LANES)]
    return jnp.concatenate(parts, axis=1)


def _mlstm_kernel(xm_ref, vm_ref, om_ref, zg_ref, prev_ref, wc_ref, bc_ref, wbd_ref, gb_ref, c0_ref, n0_ref, m0_ref,
                  hn_ref, sk_ref, y_ref, c1_ref, n1_ref, m1_ref, *, ch, l_true):
    h = pl.program_id(1)
    lp = xm_ref.shape[0]
    nchunk = lp // ch
    f32 = jnp.float32
    tri = lax.broadcasted_iota(jnp.int32, (ch, ch), 0) >= lax.broadcasted_iota(jnp.int32, (ch, ch), 1)
    tri_bf = tri.astype(bf16)
    lane_g = lax.broadcasted_iota(jnp.int32, (ch, LANES), 1)
    sub_g = lax.broadcasted_iota(jnp.int32, (LANES, ch), 0)

    def chunk(ci, carry):
        c_st, n_st, m_st = carry
        r0 = pl.multiple_of(ci * ch, ch)
        xm = xm_ref[pl.ds(r0, ch), :]
        before = xm_ref[pl.ds(pl.multiple_of(jnp.maximum(r0 - PREV_ROWS, 0), PREV_ROWS), PREV_ROWS), :]
        before = jnp.where(ci == 0, prev_ref[...], before)
        xe = jnp.concatenate([before, xm], axis=0)
        lo = PREV_ROWS - (CONV_W - 1)
        conv = bc_ref[...] + sum(xe[lo + w:lo + w + ch, :] * wc_ref[w:w + 1, :] for w in range(CONV_W))
        c = conv * jax.nn.sigmoid(conv)
        q = _blockdiag_apply(c, wbd_ref, 0)
        k = _blockdiag_apply(c, wbd_ref, 1) * (DH_M ** -0.5)
        v = _blockdiag_apply(vm_ref[pl.ds(r0, ch), :], wbd_ref, 2)
        zg = zg_ref[pl.ds(r0, ch), :] + gb_ref[...]
        live = (r0 + lax.broadcasted_iota(jnp.int32, (ch, LANES), 0)) < l_true
        a_all = jnp.where(live, zg, NEG_INF)
        lf_all = jnp.where(live, jnp.minimum(zg, 0.0) - jnp.log1p(jnp.exp(-jnp.abs(zg))), 0.0)
        b_all = jnp.zeros((ch, LANES), f32)
        for part in _split3(lf_all):
            b_all = b_all + jnp.dot(tri_bf, part, preferred_element_type=f32)
        a_col = jnp.sum(jnp.where(lane_g == ZG_IG + h, a_all, 0.0), axis=1, keepdims=True)
        b_col = jnp.sum(jnp.where(lane_g == ZG_FG + h, b_all, 0.0), axis=1, keepdims=True)
        a_row = jnp.sum(jnp.where(sub_g == ZG_IG + h, a_all.T, 0.0), axis=0, keepdims=True)
        b_row = jnp.sum(jnp.where(sub_g == ZG_FG + h, b_all.T, 0.0), axis=0, keepdims=True)
        dmat = jnp.where(tri, b_col - b_row + a_row, NEG_INF)
        inter = b_col + m_st
        m = jnp.maximum(inter, jnp.max(dmat, axis=1, keepdims=True))
        w_intra = jnp.exp(dmat - m)
        w_inter = jnp.exp(inter - m)
        qb = q.astype(bf16)
        kb = k.astype(bf16)
        vb = v.astype(bf16)
        qk = lax.dot_general(qb, kb, (((1,), (1,)), ((), ())), preferred_element_type=f32) * w_intra
        num = w_inter * jnp.dot(qb, c_st.astype(bf16), preferred_element_type=f32) \
            + jnp.dot(qk.astype(bf16), vb, preferred_element_type=f32)
        den = w_inter * jnp.sum(q * n_st, axis=1, keepdims=True) + jnp.sum(qk, axis=1, keepdims=True)
        hh = num / jnp.maximum(jnp.abs(den), jnp.exp(-m))
        m_end = m[ch - 1:ch, :]
        b_end = b_col[ch - 1:ch, :]
        w_end = jnp.exp(b_end - b_col + a_col - m_end)
        decay = jnp.exp(b_end + m_st - m_end)
        kw = k * w_end
        c_new = decay * c_st + lax.dot_general(kw.astype(bf16), vb, (((0,), (0,)), ((), ())),
                                               preferred_element_type=f32)
        n_new = decay * n_st + jnp.sum(kw, axis=0, keepdims=True)
        hg = hh * jax.nn.sigmoid(om_ref[pl.ds(r0, ch), :])
        mu = jnp.mean(hg, axis=1, keepdims=True)
        var = jnp.mean(jnp.square(hg - mu), axis=1, keepdims=True)
        y = (hg - mu) * lax.rsqrt(var + EPS) * hn_ref[...] + sk_ref[...] * c
        y_ref[pl.ds(r0, ch), :] = y.astype(y_ref.dtype)
        return c_new, n_new, m_end

    c_fin, n_fin, m_fin = lax.fori_loop(0, nchunk, chunk, (c0_ref[...], n0_ref[...], m0_ref[:, :1]))
    c1_ref[...] = c_fin
    n1_ref[...] = n_fin
    m1_ref[...] = jnp.broadcast_to(m_fin, m1_ref.shape)


def expand_blockdiag(w_bd):
    per = LANES // QKV_BLOCK
    w = w_bd.reshape(3, D_M // LANES, per, QKV_BLOCK, QKV_BLOCK)
    full = jnp.einsum('tbnio,nm->tbnimo', w, jnp.eye(per, dtype=w_bd.dtype))
    return full.reshape(3, D_M // LANES, LANES, LANES)


def mlstm_branch(z3, zg3, col_blocks, prev8, w_conv, b_conv, wbd_full, gate_bias, c0, n0, m0, w_hnorm, w_skip,
                 l_true, ch):
    b, lp, _ = z3.shape
    cx, cv, co = col_blocks
    kern = functools.partial(_mlstm_kernel, ch=ch, l_true=l_true)
    col = lambda c0_: pl.BlockSpec((None, lp, DH_M), lambda bi, h: (bi, 0, c0_ + h))
    vec = pl.BlockSpec((1, DH_M), lambda bi, h: (0, h))
    st = lambda r, c: pl.BlockSpec((None, None, r, c), lambda bi, h: (bi, h, 0, 0))
    return pl.pallas_call(
        kern,
        grid=(b, NH_M),
        in_specs=[col(cx), col(cv), col(co),
                  pl.BlockSpec((None, lp, LANES), lambda bi, h: (bi, 0, 0)),
                  pl.BlockSpec((None, PREV_ROWS, DH_M), lambda bi, h: (bi, 0, h)),
                  pl.BlockSpec((CONV_W, DH_M), lambda bi, h: (0, h)),
                  vec,
                  pl.BlockSpec((3, DH_M // LANES, LANES, LANES), lambda bi, h: (0, h, 0, 0)),
                  pl.BlockSpec((1, LANES), lambda bi, h: (0, 0)),
                  st(DH_M, DH_M), st(1, DH_M), st(1, LANES), vec, vec],
        out_specs=[pl.BlockSpec((None, lp, DH_M), lambda bi, h: (bi, 0, h)),
                   st(DH_M, DH_M), st(1, DH_M), st(1, LANES)],
        out_shape=[jax.ShapeDtypeStruct((b, lp, D_M), bf16),
                   jax.ShapeDtypeStruct((b, NH_M, DH_M, DH_M), jnp.float32),
                   jax.ShapeDtypeStruct((b, NH_M, 1, DH_M), jnp.float32),
                   jax.ShapeDtypeStruct((b, NH_M, 1, LANES), jnp.float32)],
        compiler_params=pltpu.CompilerParams(dimension_semantics=("parallel", "parallel"),
                                             vmem_limit_bytes=VMEM_LIMIT_BYTES),
        name="mlstm_branch",
    )(z3, z3, z3, zg3, prev8, w_conv, b_conv.reshape(1, D_M), wbd_full, gate_bias, c0, n0, m0,
      w_hnorm.reshape(1, D_M), w_skip.reshape(1, D_M))


def _compress_kernel(x_ref, pe_ref, w1_ref, w2_ref, o_ref):
    nchunk = x_ref.shape[0] // CMP_STRIDE
    first = jnp.zeros((nchunk, CMP_HID), jnp.float32)
    second = jnp.zeros((nchunk, CMP_HID), jnp.float32)
    for l in range(CMP_STRIDE):
        xl = x_ref[pl.ds(l, nchunk, stride=CMP_STRIDE), :]
        first += jnp.dot((xl + pe_ref[l:l + 1, :]).astype(bf16), w1_ref[l].astype(bf16),
                         preferred_element_type=jnp.float32)
        second += jnp.dot((xl + pe_ref[CMP_STRIDE + l:CMP_STRIDE + l + 1, :]).astype(bf16),
                          w1_ref[CMP_STRIDE + l].astype(bf16), preferred_element_type=jnp.float32)
    nxt = jnp.concatenate([second[1:], jnp.zeros((1, CMP_HID), jnp.float32)], axis=0)
    row = lax.broadcasted_iota(jnp.int32, (nchunk, CMP_HID), 0)
    hid = jnp.where(row < nchunk - 1, _gelu_tanh(first + nxt), 0.0)
    o_ref[...] = jnp.dot(hid.astype(bf16), w2_ref[...].astype(bf16), preferred_element_type=jnp.float32)


def compress_prompt(z3, col0, cmp_pe, cmp_w1, cmp_w2):
    b, l, _ = z3.shape
    nchunk = l // CMP_STRIDE
    return pl.pallas_call(
        _compress_kernel,
        grid=(b, 2, G_N),
        in_specs=[pl.BlockSpec((None, l, DH_N), lambda bi, s, g: (bi, 0, col0 + s * G_N + g)),
                  pl.BlockSpec((None, CMP_LEN, DH_N), lambda bi, s, g: (s, 0, 0)),
                  pl.BlockSpec((None, CMP_LEN, DH_N, CMP_HID), lambda bi, s, g: (s, 0, 0, 0)),
                  pl.BlockSpec((None, CMP_HID, DH_N), lambda bi, s, g: (s, 0, 0))],
        out_specs=pl.BlockSpec((None, None, None, nchunk, DH_N), lambda bi, s, g: (bi, s, g, 0, 0)),
        out_shape=jax.ShapeDtypeStruct((b, 2, G_N, nchunk, DH_N), jnp.float32),
        compiler_params=pltpu.CompilerParams(dimension_semantics=("parallel", "parallel", "parallel"),
                                             vmem_limit_bytes=VMEM_LIMIT_BYTES),
        name="nsa_compress",
    )(z3, cmp_pe, cmp_w1, cmp_w2)


def _softmax_pv(s, keep, v):
    s = jnp.where(keep, s, NEG)
    m = jnp.max(s, axis=-1, keepdims=True)
    e = jnp.exp(s - m)
    den = jnp.sum(e, axis=-1, keepdims=True)
    return jnp.dot(e.astype(bf16), v, preferred_element_type=jnp.float32) / den


def _nsa_prompt_kernel(q_ref, kc_ref, vc_ref, ks_ref, vs_ref, kw_ref, vw_ref, gn_ref, o_ref, *, tq):
    g = pl.program_id(1)
    qi = pl.program_id(2)
    f32 = jnp.float32
    t = ks_ref.shape[0]
    n_c = kc_ref.shape[0]
    n_s = t // SEL_LEN
    nq = J_N * tq
    q4 = jnp.concatenate([q_ref[:, j * DH_N:(j + 1) * DH_N] for j in range(J_N)], axis=0).astype(bf16)
    nt = (((1,), (1,)), ((), ()))

    def qpos_of(shape):
        return qi * tq + lax.broadcasted_iota(jnp.int32, shape, 0) % tq

    cidx = lax.broadcasted_iota(jnp.int32, (nq, n_c), 1)
    cmask = cidx * CMP_STRIDE + CMP_LEN <= qpos_of((nq, n_c)) + 1
    s = lax.dot_general(q4, kc_ref[...].astype(bf16), nt, preferred_element_type=f32) * SCALE_N
    s = jnp.where(cmask, s, NEG)
    m = jnp.max(s, axis=-1, keepdims=True)
    e = jnp.where(cmask, jnp.exp(s - m), 0.0)
    den = jnp.sum(e, axis=-1, keepdims=True)
    p = e / jnp.where(den > 0.0, den, 1.0)
    o_cmp = jnp.dot(p.astype(bf16), vc_ref[...].astype(bf16), preferred_element_type=f32)
    p_sum = p[0:tq]
    for j in range(1, J_N):
        p_sum = p_sum + p[j * tq:(j + 1) * tq]
    ci = lax.broadcasted_iota(jnp.int32, (n_c, LANES), 0)
    si = lax.broadcasted_iota(jnp.int32, (n_c, LANES), 1)
    cover = jnp.logical_and(ci * CMP_STRIDE < si * SEL_LEN + SEL_LEN,
                            ci * CMP_STRIDE + CMP_LEN > si * SEL_LEN).astype(bf16)
    imp = jnp.zeros((tq, LANES), f32)
    for part in _split3(p_sum):
        imp = imp + jnp.dot(part, cover, preferred_element_type=f32)
    blk = lax.broadcasted_iota(jnp.int32, (tq, LANES), 1)
    cur = (qi * tq + lax.broadcasted_iota(jnp.int32, (tq, LANES), 0)) // SEL_LEN
    valid = blk <= cur
    forced = jnp.logical_or(blk == 0, blk >= cur - 1)
    score = jnp.where(valid, imp + jnp.where(forced, BIG, 0.0), -BIG)
    score = jnp.where(blk < n_s, score, NEG_INF)
    rank = jnp.zeros((tq, LANES), jnp.int32)
    for s2 in range(n_s):
        col = score[:, s2:s2 + 1]
        ahead = jnp.logical_or(col > score, jnp.logical_and(col == score, blk > s2))
        rank = rank + ahead.astype(jnp.int32)
    sel = (rank < N_SEL).astype(bf16)

    ck = SEL_CHUNK
    s_row = lax.broadcasted_iota(jnp.int32, (LANES, ck), 0)
    k_lane = lax.broadcasted_iota(jnp.int32, (LANES, ck), 1)
    kpos1 = lax.broadcasted_iota(jnp.int32, (tq, ck), 1)
    qpos1 = qi * tq + lax.broadcasted_iota(jnp.int32, (tq, ck), 0)

    def kchunk(kb, carry):
        m_o, l_o, acc = carry
        k0 = pl.multiple_of(kb * ck, ck)
        expand = ((k0 + k_lane) // SEL_LEN == s_row).astype(bf16)
        picked = jnp.dot(sel, expand, preferred_element_type=f32)
        keep1 = jnp.logical_and(picked > 0.5, k0 + kpos1 <= qpos1)
        keep = jnp.concatenate([keep1] * J_N, axis=0)
        kk = ks_ref[pl.ds(k0, ck), :].astype(bf16)
        vv = vs_ref[pl.ds(k0, ck), :].astype(bf16)
        sc = lax.dot_general(q4, kk, nt, preferred_element_type=f32) * SCALE_N
        sc = jnp.where(keep, sc, NEG)
        m_n = jnp.maximum(m_o, jnp.max(sc, axis=-1, keepdims=True))
        alpha = jnp.exp(m_o - m_n)
        ee = jnp.where(keep, jnp.exp(sc - m_n), 0.0)
        l_n = alpha * l_o + jnp.sum(ee, axis=-1, keepdims=True)
        return m_n, l_n, alpha * acc + jnp.dot(ee.astype(bf16), vv, preferred_element_type=f32)

    nkc = (qi * tq + tq + ck - 1) // ck
    _, l_f, acc_f = lax.fori_loop(0, nkc, kchunk, (jnp.full((nq, 1), NEG, f32), jnp.zeros((nq, 1), f32),
                                                   jnp.zeros((nq, DH_N), f32)))
    o_sel = acc_f / l_f

    span = WINDOW + tq
    w0 = pl.multiple_of(jnp.clip(qi * tq - WINDOW, 0, t - span), tq)
    kw = kw_ref[pl.ds(w0, span), :].astype(bf16)
    vw = vw_ref[pl.ds(w0, span), :].astype(bf16)
    qpos_w = qpos_of((nq, span))
    kpos_w = w0 + lax.broadcasted_iota(jnp.int32, (nq, span), 1)
    keep_win = jnp.logical_and(kpos_w <= qpos_w, kpos_w > qpos_w - WINDOW)
    sw = lax.dot_general(q4, kw, nt, preferred_element_type=f32) * SCALE_N
    o_win = _softmax_pv(sw, keep_win, vw)

    gates = jax.nn.sigmoid(gn_ref[...])
    gl = lax.broadcasted_iota(jnp.int32, gates.shape, 1)
    outs = []
    for j in range(J_N):
        acc = jnp.zeros((tq, DH_N), f32)
        for which, o in enumerate((o_cmp, o_sel, o_win)):
            col = jnp.sum(jnp.where(gl == ZG_GN + (g * J_N + j) * 3 + which, gates, 0.0), axis=-1, keepdims=True)
            acc = acc + col * o[j * tq:(j + 1) * tq]
        outs.append(acc)
    o_ref[...] = jnp.concatenate(outs, axis=1).astype(o_ref.dtype)


def nsa_prompt(z3, zg3, kcv, tq=128):
    b, l, _ = z3.shape
    assert l >= WINDOW + tq and l % tq == 0 and l % SEL_CHUNK == 0
    nq = l // tq
    n_c = kcv.shape[3]
    full = lambda c0: pl.BlockSpec((None, l, DH_N), lambda bi, g, qi: (bi, 0, c0 // DH_N + g))
    return pl.pallas_call(
        functools.partial(_nsa_prompt_kernel, tq=tq),
        grid=(b, G_N, nq),
        in_specs=[pl.BlockSpec((None, tq, J_N * DH_N), lambda bi, g, qi: (bi, qi, Z_Q // (J_N * DH_N) + g)),
                  pl.BlockSpec((None, None, None, n_c, DH_N), lambda bi, g, qi: (bi, 0, g, 0, 0)),
                  pl.BlockSpec((None, None, None, n_c, DH_N), lambda bi, g, qi: (bi, 1, g, 0, 0)),
                  full(Z_KV + 2 * KVW), full(Z_KV + 3 * KVW), full(Z_WIN), full(Z_WIN + KVW),
                  pl.BlockSpec((None, tq, LANES), lambda bi, g, qi: (bi, qi, 0))],
        out_specs=pl.BlockSpec((None, tq, J_N * DH_N), lambda bi, g, qi: (bi, qi, g)),
        out_shape=jax.ShapeDtypeStruct((b, l, D_N), bf16),
        compiler_params=pltpu.CompilerParams(dimension_semantics=("parallel", "parallel", "arbitrary"),
                                             vmem_limit_bytes=VMEM_LIMIT_BYTES),
        name="nsa_prompt",
    )(z3, kcv, kcv, z3, z3, z3, z3, zg3)


def _paged_compress_kernel(pt_ref, *refs, page, nstep):
    pages = refs[:PAGES_PER_STEP]
    new_ref, pe_ref, w1_ref, w2_ref, o_ref, carry_ref, rows_ref = refs[PAGES_PER_STEP:]
    p = pl.program_id(1)
    nch = PAGES_PER_STEP * page // CMP_STRIDE
    row = lax.broadcasted_iota(jnp.int32, (nch, CMP_HID), 0)

    @pl.when(p < nstep)
    def _():
        for r in range(PAGES_PER_STEP):
            for sg in range(2 * G_N):
                rows_ref[sg, r * page:(r + 1) * page, :] = pages[r][:, sg, :]
        for s in range(2):
            first = jnp.zeros((G_N * nch, CMP_HID), jnp.float32)
            second = jnp.zeros((G_N * nch, CMP_HID), jnp.float32)
            for k in range(CMP_STRIDE // 2):
                xs = [jnp.concatenate([rows_ref[s * G_N + g, pl.ds(l, nch, stride=CMP_STRIDE), :] for g in range(G_N)],
                                      axis=0) for l in (2 * k, 2 * k + 1)]
                lhs = lambda off: jnp.concatenate(
                    [(xs[i] + pe_ref[s, off + 2 * k + i:off + 2 * k + i + 1, :]).astype(bf16) for i in range(2)], axis=1)
                first += jnp.dot(lhs(0), w1_ref[s, k], preferred_element_type=jnp.float32)
                second += jnp.dot(lhs(CMP_STRIDE), w1_ref[s, CMP_STRIDE // 2 + k], preferred_element_type=jnp.float32)
            for g in range(G_N):
                f_g = first[g * nch:(g + 1) * nch]
                s_g = second[g * nch:(g + 1) * nch]
                prev = carry_ref[s * G_N + g, 0:1, :]
                shifted = jnp.where(row == 0, prev, pltpu.roll(f_g, 1, axis=0))
                carry_ref[s * G_N + g, 0:1, :] = f_g[nch - 1:nch, :]
                hid = _gelu_tanh(shifted + s_g)
                hid = jnp.where(jnp.logical_and(row == 0, p == 0), 0.0, hid)
                o_ref[s, g] = jnp.dot(hid.astype(bf16), w2_ref[s].astype(bf16), preferred_element_type=jnp.float32)

    @pl.when(p == nstep)
    def _():
        for s in range(2):
            for g in range(G_N):
                c0 = (s * G_N + g) * DH_N
                sec = jnp.zeros((QROWS, CMP_HID), jnp.float32)
                for l in range(CMP_STRIDE):
                    x = jnp.broadcast_to(new_ref[l:l + 1, c0:c0 + DH_N] + pe_ref[s, CMP_STRIDE + l:CMP_STRIDE + l + 1, :],
                                         (QROWS, DH_N))
                    w_l = w1_ref[s, (CMP_STRIDE + l) // 2, (l % 2) * DH_N:(l % 2 + 1) * DH_N, :]
                    sec += jnp.dot(x.astype(bf16), w_l, preferred_element_type=jnp.float32)
                hid = _gelu_tanh(carry_ref[s * G_N + g, 0:1, :] + sec)
                blk = jnp.dot(hid.astype(bf16), w2_ref[s].astype(bf16), preferred_element_type=jnp.float32)
                rows = lax.broadcasted_iota(jnp.int32, (nch, DH_N), 0)
                o_ref[s, g] = jnp.where(rows == 0, jnp.broadcast_to(blk[0:1, :], (nch, DH_N)), 0.0)


def paged_compress(page_table, cache3, page, new_c, cmp_pe, w1_bf, cmp_w2):
    b, npages = page_table.shape
    assert page % CMP_STRIDE == 0 and npages % PAGES_PER_STEP == 0 and new_c.shape[1] == NEW_ROWS == CMP_STRIDE
    nstep = npages // PAGES_PER_STEP
    nch = PAGES_PER_STEP * page // CMP_STRIDE

    def page_spec(r):
        return pl.BlockSpec((page, 2 * G_N, DH_N),
                            lambda bi, p, pt: (pt[bi, jnp.minimum(p * PAGES_PER_STEP + r, npages - 1)], 0, 0))

    grid_spec = pltpu.PrefetchScalarGridSpec(
        num_scalar_prefetch=1,
        grid=(b, nstep + 1),
        in_specs=[page_spec(r) for r in range(PAGES_PER_STEP)] + [
            pl.BlockSpec((None, NEW_ROWS, 2 * KVW), lambda bi, p, pt: (bi, 0, 0)),
            pl.BlockSpec((2, CMP_LEN, DH_N), lambda bi, p, pt: (0, 0, 0)),
            pl.BlockSpec((2, CMP_LEN // 2, 2 * DH_N, CMP_HID), lambda bi, p, pt: (0, 0, 0, 0)),
            pl.BlockSpec((2, CMP_HID, DH_N), lambda bi, p, pt: (0, 0, 0))],
        out_specs=pl.BlockSpec((None, 2, G_N, nch, DH_N), lambda bi, p, pt: (bi, 0, 0, p, 0)),
        scratch_shapes=[pltpu.VMEM((2 * G_N, QROWS, CMP_HID), jnp.float32),
                        pltpu.VMEM((2 * G_N, PAGES_PER_STEP * page, DH_N), jnp.float32)])
    return pl.pallas_call(
        functools.partial(_paged_compress_kernel, page=page, nstep=nstep),
        grid_spec=grid_spec,
        out_shape=jax.ShapeDtypeStruct((b, 2, G_N, (nstep + 1) * nch, DH_N), jnp.float32),
        compiler_params=pltpu.CompilerParams(dimension_semantics=("parallel", "arbitrary"),
                                             vmem_limit_bytes=VMEM_LIMIT_BYTES),
        name="nsa_paged_compress",
    )(page_table, *([cache3] * PAGES_PER_STEP), new_c, cmp_pe, w1_bf, cmp_w2)


def _paged_attend_kernel(pt_ref, *refs, page, nstep, l_true):
    pages = refs[:PAGES_PER_STEP]
    (q_ref, kcv_ref, new_ref, winp_ref, wnew_ref, zg_ref, o_ref,
     sel_ref, ocmp_ref, m_ref, l_ref, acc_ref) = refs[PAGES_PER_STEP:]
    p = pl.program_id(1)
    past = nstep * PAGES_PER_STEP * page
    tk = PAGES_PER_STEP * page
    n_r = kcv_ref.shape[2]
    n_s = -(-(past + l_true) // SEL_LEN)
    sl = 2 * LANES
    assert n_s <= sl
    nq = J_N * QROWS
    f32 = jnp.float32
    nt = (((1,), (1,)), ((), ()))

    def q_of(g):
        return jnp.concatenate([q_ref[:, (g * J_N + j) * DH_N:(g * J_N + j + 1) * DH_N] for j in range(J_N)],
                               axis=0).astype(bf16)

    def pos_of(shape):
        return lax.broadcasted_iota(jnp.int32, shape, 0) % QROWS

    def online(g, s, keep, v):
        s = jnp.where(keep, s, NEG)
        m_old = m_ref[g]
        m_new = jnp.maximum(m_old, jnp.max(s, axis=-1, keepdims=True))
        alpha = jnp.exp(m_old - m_new)
        e = jnp.where(keep, jnp.exp(s - m_new), 0.0)
        l_ref[g] = alpha * l_ref[g] + jnp.sum(e, axis=-1, keepdims=True)
        acc_ref[g] = alpha * acc_ref[g] + jnp.dot(e.astype(bf16), v, preferred_element_type=f32)
        m_ref[g] = m_new

    @pl.when(p == 0)
    def _():
        r_i = lax.broadcasted_iota(jnp.int32, (nq, n_r), 1)
        qpos = past + pos_of((nq, n_r))
        cmask = jnp.logical_and(r_i >= 1, (r_i - 1) * CMP_STRIDE + CMP_LEN <= qpos + 1)
        ci = lax.broadcasted_iota(jnp.int32, (n_r, sl), 0) - 1
        si = lax.broadcasted_iota(jnp.int32, (n_r, sl), 1)
        cover = jnp.logical_and(jnp.logical_and(ci >= 0, ci * CMP_STRIDE < si * SEL_LEN + SEL_LEN),
                                ci * CMP_STRIDE + CMP_LEN > si * SEL_LEN).astype(bf16)
        blk = lax.broadcasted_iota(jnp.int32, (QROWS, sl), 1)
        cur = (past + lax.broadcasted_iota(jnp.int32, (QROWS, sl), 0)) // SEL_LEN
        valid = blk <= cur
        forced = jnp.logical_or(blk == 0, blk >= cur - 1)
        s_src = lax.broadcasted_iota(jnp.int32, (sl, sl), 0)
        s_dst = lax.broadcasted_iota(jnp.int32, (sl, sl), 1)
        for g in range(G_N):
            q = q_of(g)
            kc = kcv_ref[0, g].astype(bf16)
            vc = kcv_ref[1, g].astype(bf16)
            s = lax.dot_general(q, kc, nt, preferred_element_type=f32) * SCALE_N
            s = jnp.where(cmask, s, NEG)
            m = jnp.max(s, axis=-1, keepdims=True)
            e = jnp.where(cmask, jnp.exp(s - m), 0.0)
            den = jnp.sum(e, axis=-1, keepdims=True)
            pr = e / jnp.where(den > 0.0, den, 1.0)
            ocmp_ref[g] = jnp.dot(pr.astype(bf16), vc, preferred_element_type=f32)
            p_sum = pr[0:QROWS]
            for j in range(1, J_N):
                p_sum = p_sum + pr[j * QROWS:(j + 1) * QROWS]
            imp = jnp.zeros((QROWS, sl), f32)
            for part in _split3(p_sum):
                imp = imp + jnp.dot(part, cover, preferred_element_type=f32)
            score = jnp.where(valid, imp + jnp.where(forced, BIG, 0.0), -BIG)
            score = jnp.where(blk < n_s, score, NEG_INF)
            score_t = score.T
            sel_rows = []
            for qi in range(QROWS):
                col = score_t[:, qi:qi + 1]
                rw = score[qi:qi + 1, :]
                ahead = jnp.logical_or(col > rw, jnp.logical_and(col == rw, s_src < s_dst))
                rank = jnp.sum(ahead.astype(jnp.int32), axis=0, keepdims=True)
                sel_rows.append((rank < N_SEL).astype(f32))
            sel_ref[g] = jnp.concatenate(sel_rows, axis=0)
            m_ref[g] = jnp.full((nq, 1), NEG, f32)
            l_ref[g] = jnp.zeros((nq, 1), f32)
            acc_ref[g] = jnp.zeros((nq, DH_N), f32)

    @pl.when(p < nstep)
    def _():
        k0 = p * tk
        expand = ((k0 + lax.broadcasted_iota(jnp.int32, (sl, tk), 1)) // SEL_LEN
                  == lax.broadcasted_iota(jnp.int32, (sl, tk), 0)).astype(bf16)
        for g in range(G_N):
            picked = jnp.dot(sel_ref[g].astype(bf16), expand, preferred_element_type=f32)
            keep = jnp.concatenate([picked] * J_N, axis=0) > 0.5
            kk = jnp.concatenate([pages[r][:, g, :] for r in range(PAGES_PER_STEP)], axis=0).astype(bf16)
            vv = jnp.concatenate([pages[r][:, G_N + g, :] for r in range(PAGES_PER_STEP)], axis=0).astype(bf16)
            s = lax.dot_general(q_of(g), kk, nt, preferred_element_type=f32) * SCALE_N
            online(g, s, keep, vv)

    @pl.when(p == nstep)
    def _():
        gates = jax.nn.sigmoid(zg_ref[...])
        gl = lax.broadcasted_iota(jnp.int32, gates.shape, 1)
        pos_n = pos_of((nq, NEW_ROWS))
        l_n = lax.broadcasted_iota(jnp.int32, (nq, NEW_ROWS), 1)
        keep_new = jnp.logical_and(l_n <= pos_n, l_n < l_true)
        wbuf = winp_ref.shape[0]
        pos_w = pos_of((nq, wbuf + NEW_ROWS))
        i_w = lax.broadcasted_iota(jnp.int32, (nq, wbuf + NEW_ROWS), 1)
        keep_win = jnp.logical_or(
            jnp.logical_and(i_w < wbuf, i_w - wbuf > pos_w - WINDOW),
            jnp.logical_and(i_w >= wbuf, jnp.logical_and(i_w - wbuf <= pos_w, i_w - wbuf < l_true)))
        for g in range(G_N):
            q = q_of(g)
            kn = new_ref[:, g * DH_N:(g + 1) * DH_N].astype(bf16)
            vn = new_ref[:, KVW + g * DH_N:KVW + (g + 1) * DH_N].astype(bf16)
            s = lax.dot_general(q, kn, nt, preferred_element_type=f32) * SCALE_N
            online(g, s, keep_new, vn)
            o_sel = acc_ref[g] / l_ref[g]
            kw = jnp.concatenate([winp_ref[:, g, :], wnew_ref[:, g * DH_N:(g + 1) * DH_N]], axis=0).astype(bf16)
            vw = jnp.concatenate([winp_ref[:, G_N + g, :], wnew_ref[:, KVW + g * DH_N:KVW + (g + 1) * DH_N]],
                                 axis=0).astype(bf16)
            s = lax.dot_general(q, kw, nt, preferred_element_type=f32) * SCALE_N
            o_win = _softmax_pv(s, keep_win, vw)
            o_cmp = ocmp_ref[g]
            for j in range(J_N):
                acc = jnp.zeros((QROWS, DH_N), f32)
                for which, o in enumerate((o_cmp, o_sel, o_win)):
                    col = jnp.sum(jnp.where(gl == ZG_GN + (g * J_N + j) * 3 + which, gates, 0.0), axis=-1, keepdims=True)
                    acc = acc + col * o[j * QROWS:(j + 1) * QROWS]
                o_ref[:, (g * J_N + j) * DH_N:(g * J_N + j + 1) * DH_N] = acc.astype(o_ref.dtype)


def paged_attend(page_table, cache3, page, q8, kcv, new_s, winp, wnew, zg8, l_true):
    b, npages = page_table.shape
    nstep = npages // PAGES_PER_STEP
    assert winp.shape[0] == b * WINDOW and l_true <= QROWS and npages % PAGES_PER_STEP == 0
    n_r = kcv.shape[3]
    nq = J_N * QROWS

    def page_spec(r):
        return pl.BlockSpec((page, 2 * G_N, DH_N),
                            lambda bi, p, pt: (pt[bi, jnp.minimum(p * PAGES_PER_STEP + r, npages - 1)], 1, 0))

    per_b = lambda rows, cols: pl.BlockSpec((None, rows, cols), lambda bi, p, pt: (bi, 0, 0))
    grid_spec = pltpu.PrefetchScalarGridSpec(
        num_scalar_prefetch=1,
        grid=(b, nstep + 1),
        in_specs=[page_spec(r) for r in range(PAGES_PER_STEP)] + [
            per_b(QROWS, D_N),
            pl.BlockSpec((None, 2, G_N, n_r, DH_N), lambda bi, p, pt: (bi, 0, 0, 0, 0)),
            per_b(NEW_ROWS, 2 * KVW), pl.BlockSpec((WINDOW, 2 * G_N, DH_N), lambda bi, p, pt: (bi, 0, 0)),
            per_b(NEW_ROWS, 2 * KVW), per_b(QROWS, LANES)],
        out_specs=per_b(QROWS, D_N),
        scratch_shapes=[pltpu.VMEM((G_N, QROWS, 2 * LANES), jnp.float32),
                        pltpu.VMEM((G_N, nq, DH_N), jnp.float32),
                        pltpu.VMEM((G_N, nq, 1), jnp.float32),
                        pltpu.VMEM((G_N, nq, 1), jnp.float32),
                        pltpu.VMEM((G_N, nq, DH_N), jnp.float32)])
    return pl.pallas_call(
        functools.partial(_paged_attend_kernel, page=page, nstep=nstep, l_true=l_true),
        grid_spec=grid_spec,
        out_shape=jax.ShapeDtypeStruct((b, QROWS, D_N), jnp.float32),
        compiler_params=pltpu.CompilerParams(dimension_semantics=("parallel", "arbitrary"),
                                             vmem_limit_bytes=VMEM_LIMIT_BYTES),
        name="nsa_paged_attend",
    )(page_table, *([cache3] * PAGES_PER_STEP), q8, kcv, new_s, winp, wnew, zg8)


def _extract_top(s, k):
    rows = lax.broadcasted_iota(jnp.int32, s.shape, 0)
    nrow = s.shape[0]
    work = s
    taken = jnp.zeros(s.shape, jnp.bool_)
    tops = []
    for _ in range(k):
        m = jnp.max(work, axis=0, keepdims=True)
        first = jnp.min(jnp.where(work == m, rows, nrow), axis=0, keepdims=True)
        hit = rows == first
        taken = jnp.logical_or(taken, hit)
        work = jnp.where(hit, NEG_INF, work)
        tops.append(m)
    return tops, taken


def _peer_route_kernel(qt_ref, keys_ref, s1_ref, e1_ref, s2_ref, e2_ref, tau_ref):
    halves = []
    for c in range(2):
        q = qt_ref[c * PEER_QHALF:(c + 1) * PEER_QHALF, :].astype(bf16)
        s = jnp.dot(keys_ref[c].astype(bf16), q, preferred_element_type=jnp.float32)
        tops, taken = _extract_top(s, PEER_TOPK)
        halves.append((s, tops, taken))
    (sa, ta, ma), (sb, tb, mb) = halves
    assert PEER_TOPK == 16
    ta_col = jnp.concatenate(ta, axis=0)
    tb_col = jnp.concatenate(tb, axis=0)
    a_idx = lax.broadcasted_iota(jnp.int32, (8, ta_col.shape[1]), 0)
    cand = jnp.concatenate(
        [ta[0] + tb_col, ta[1] + tb_col, ta[2] + tb_col[0:8], ta[3] + tb_col[0:8]]
        + [jnp.where(a_idx >= 4, ta_col[0:8] + tb[b], NEG_INF) for b in range(3)]
        + [ta_col[8:16] + tb[0]], axis=0)
    ctops, _ = _extract_top(cand, PEER_TOPK)
    cmax = ctops[0]
    z = jnp.exp(ctops[0] - cmax)
    for r in range(1, PEER_TOPK):
        z = z + jnp.exp(ctops[r] - cmax)
    tau_ref[...] = ctops[PEER_TOPK - 1]
    s1_ref[...] = jnp.where(ma, sa, NEG_INF)
    s2_ref[...] = jnp.where(mb, sb, NEG_INF)
    e1_ref[...] = jnp.where(ma, jnp.exp(sa - ta[0]), 0.0) / z
    e2_ref[...] = jnp.where(mb, jnp.exp(sb - tb[0]), 0.0)


def peer_route(qt, pkeys, tb):
    n = qt.shape[1]
    big = jax.ShapeDtypeStruct((PEER_HEADS, N_KEYS, n), jnp.float32)
    bspec = pl.BlockSpec((None, N_KEYS, tb), lambda i, h: (h, 0, i))
    return pl.pallas_call(
        _peer_route_kernel,
        grid=(n // tb, PEER_HEADS),
        in_specs=[pl.BlockSpec((2 * PEER_QHALF, tb), lambda i, h: (h, i)),
                  pl.BlockSpec((None, 2, N_KEYS, PEER_QHALF), lambda i, h: (h, 0, 0, 0))],
        out_specs=[bspec, bspec, bspec, bspec, pl.BlockSpec((None, 1, tb), lambda i, h: (h, 0, i))],
        out_shape=[big, big, big, big, jax.ShapeDtypeStruct((PEER_HEADS, 1, n), jnp.float32)],
        compiler_params=pltpu.CompilerParams(dimension_semantics=("parallel", "parallel")),
        name="peer_route",
    )(qt, pkeys)


def _peer_dense_kernel(xt_ref, u_ref, vt_ref, s1_ref, e1_ref, s2_ref, e2_ref, tau_ref, o_ref, *, sub):
    j = pl.program_id(1)

    @pl.when(j == 0)
    def _():
        o_ref[...] = jnp.zeros_like(o_ref)

    ht = jnp.dot(u_ref[...], xt_ref[...], preferred_element_type=jnp.float32)
    acts = []
    for a in range(sub):
        i1 = j * sub + a
        g = None
        for h in range(PEER_HEADS):
            s1row = s1_ref[h, pl.ds(i1, 1), :]
            e1row = e1_ref[h, pl.ds(i1, 1), :]
            c = s2_ref[h] + s1row
            t = jnp.where(c >= tau_ref[h], e2_ref[h], 0.0) * e1row
            g = t if g is None else g + t
        acts.append((_gelu_tanh(ht[a * N_KEYS:(a + 1) * N_KEYS, :]) * g).astype(bf16))
    act = jnp.concatenate(acts, axis=0) if sub > 1 else acts[0]
    o_ref[...] += jnp.dot(vt_ref[...], act, preferred_element_type=jnp.float32)


def peer_dense(xt, u_bf, vt_bf, s1, e1, s2, e2, tau, tb, sub):
    d, n = xt.shape
    e = u_bf.shape[0]
    te = sub * N_KEYS
    once = dict(pipeline_mode=pl.Buffered(1))
    rspec = pl.BlockSpec((PEER_HEADS, N_KEYS, tb), lambda i, j: (0, 0, i), **once)
    return pl.pallas_call(
        functools.partial(_peer_dense_kernel, sub=sub),
        grid=(n // tb, e // te),
        in_specs=[pl.BlockSpec((d, tb), lambda i, j: (0, i), **once),
                  pl.BlockSpec((te, d), lambda i, j: (j, 0)),
                  pl.BlockSpec((d, te), lambda i, j: (0, j)),
                  rspec, rspec, rspec, rspec,
                  pl.BlockSpec((PEER_HEADS, 1, tb), lambda i, j: (0, 0, i), **once)],
        out_specs=pl.BlockSpec((d, tb), lambda i, j: (0, i)),
        out_shape=jax.ShapeDtypeStruct((d, n), jnp.float32),
        compiler_params=pltpu.CompilerParams(dimension_semantics=("parallel", "arbitrary"),
                                             vmem_limit_bytes=BIG_VMEM_LIMIT_BYTES),
        name="peer_dense",
    )(xt, u_bf, vt_bf, s1, e1, s2, e2, tau)


def peer_ffn_t(xt, w_pq_t, pkeys, u_bf, vt_bf):
    n = xt.shape[1]
    qt = pmm(w_pq_t, xt)
    s1, e1, s2, e2, tau = peer_route(qt, pkeys, _pick(n, (256, 128)))
    return peer_dense(xt, u_bf, vt_bf, s1, e1, s2, e2, tau, _pick(n, (512, 256, 128)), PEER_SUB)


def _in_proj_weights(w_in):
    o = _IN_OFFS
    w_main = jnp.concatenate([w_in[:, o[0]:o[3]], w_in[:, o[5]:o[8]], w_in[:, o[9]:o[11]]], axis=1).astype(bf16)
    w_gate = jnp.concatenate([w_in[:, o[3]:o[5]], w_in[:, o[8]:o[9]],
                              jnp.zeros((w_in.shape[0], LANES - 2 * NH_M - 3 * H_N), w_in.dtype)], axis=1).astype(bf16)
    return w_main, w_gate


def kernel(x_prompt, x_sample, cache_kv, state_win_kv, state_conv, state_C, state_n, state_m, page_table,
           norm_mix, norm_ffn, norm_final, w_in, w_conv, b_conv, w_bd, b_gates, w_hnorm, w_skip,
           cmp_pe, cmp_w1, cmp_w2, w_br_m, w_br_n, w_out, w_pq, peer_keys, peer_u, peer_v):
    assert w_in.shape[0] == DEPTH == 1
    f32 = jnp.float32
    l = 0
    Bp, Lp_, _ = x_prompt.shape
    Bs, Ls, _ = x_sample.shape
    w_main, w_gate = _in_proj_weights(w_in[l])
    wbd_full = expand_blockdiag(w_bd[l])
    gate_bias = jnp.concatenate([b_gates[l, 0], b_gates[l, 1], jnp.zeros((LANES - 2 * NH_M,), f32)])[None]
    w_br_m_bf, w_br_n_bf, w_out_bf = w_br_m[l].astype(bf16), w_br_n[l].astype(bf16), w_out[l].astype(bf16)
    w_pq_t = w_pq[l].T.astype(bf16)
    u_bf = peer_u[l].astype(bf16)
    vt_bf = peer_v[l].T.astype(bf16)

    def mixers(x, conv_prev, C0, n0, m0, nsa_fn, ch, lpad):
        B, L, _ = x.shape
        n = B * L
        x2 = x.reshape(n, D_MODEL)
        xn = rmsnorm_rows(x2, norm_mix[l], bf16)
        z2 = pmm(xn, w_main)
        zg2 = pmm(xn, w_gate)
        z3 = z2.reshape(B, L, Z_COLS)
        zg3 = zg2.reshape(B, L, LANES)
        xm = z3[:, :, Z_XM:Z_XM + D_M]
        new_conv = jnp.concatenate([conv_prev, xm], axis=1)[:, L:]
        prev8 = jnp.pad(conv_prev, ((0, 0), (PREV_ROWS - (CONV_W - 1), 0), (0, 0)))
        if lpad == L:
            zm3, zgm3, cols = z3, zg3, (Z_XM // DH_M, Z_VM // DH_M, Z_OM // DH_M)
        else:
            zm3 = jnp.pad(z3[:, :, :Z_Q], ((0, 0), (0, lpad - L), (0, 0)))
            zgm3 = jnp.pad(zg3, ((0, 0), (0, lpad - L), (0, 0)))
            cols = (Z_XM // DH_M, Z_VM // DH_M, Z_OM // DH_M)
        y_m, C1, n1, m1 = mlstm_branch(zm3, zgm3, cols, prev8, w_conv[l], b_conv[l], wbd_full, gate_bias,
                                       C0, n0[:, :, None, :], jnp.broadcast_to(m0[:, :, None, None], (B, NH_M, 1, LANES)),
                                       w_hnorm[l], w_skip[l], L, ch)
        y_m = y_m[:, :L].reshape(n, D_M)
        kv_new = z3[:, :, Z_KV:Z_KV + N_KV_SETS * KVW].reshape(B, L, N_KV_SETS, G_N, DH_N)
        win_new = z3[:, :, Z_WIN:Z_WIN + 2 * KVW].reshape(B, L, 2, G_N, DH_N)
        y_n, new_win = nsa_fn(z3, zg3, kv_new, win_new)
        merged = merge_branches(y_m, y_n.reshape(n, D_N).astype(bf16), w_br_m_bf, w_br_n_bf, z2)
        h2 = out_proj_residual(merged, w_out_bf, x2)
        return h2, kv_new, new_win, new_conv, C1, n1[:, :, 0], m1[:, :, 0, 0]

    def nsa_p(z3, zg3, kv_new, win_new):
        kcv = compress_prompt(z3, Z_KV // DH_N, cmp_pe[l], cmp_w1[l], cmp_w2[l])
        return nsa_prompt(z3, zg3, kcv), win_new[:, -min(WINDOW, z3.shape[1]):]

    def nsa_s(z3, zg3, kv_new, win_new):
        L = z3.shape[1]
        padr = lambda t, rows: jnp.pad(t, ((0, 0), (0, rows - t.shape[1]), (0, 0)))
        page = cache_kv.shape[2]
        cache3 = cache_kv.reshape(DEPTH * cache_kv.shape[1] * page, N_KV_SETS * G_N, DH_N)
        wbuf = state_win_kv.shape[2]
        win3 = state_win_kv.reshape(Bs * wbuf, 2 * G_N, DH_N)
        kcv = paged_compress(page_table, cache3, page, padr(z3[:, :, Z_KV:Z_KV + 2 * KVW], NEW_ROWS),
                             cmp_pe[l], cmp_w1[l].astype(bf16).reshape(2, CMP_LEN // 2, 2 * DH_N, CMP_HID), cmp_w2[l])
        y8 = paged_attend(page_table, cache3, page, padr(z3[:, :, Z_Q:Z_Q + D_N], QROWS), kcv,
                          padr(z3[:, :, Z_KV + 2 * KVW:Z_KV + 4 * KVW], NEW_ROWS),
                          win3, padr(z3[:, :, Z_WIN:Z_WIN + 2 * KVW], NEW_ROWS), padr(zg3, QROWS), L)
        new_win = jnp.concatenate([state_win_kv.reshape(Bs, wbuf, 2, G_N, DH_N)[:, L:], win_new], axis=1)
        return y8[:, :L], new_win

    hp, kvp, winp, convp, Cp, n_p, m_p = mixers(
        x_prompt, jnp.zeros((Bp, CONV_W - 1, D_M), f32), jnp.zeros((Bp, NH_M, DH_M, DH_M), f32),
        jnp.zeros((Bp, NH_M, DH_M), f32), jnp.zeros((Bp, NH_M), f32), nsa_p, 256, Lp_)
    hs, kvs, wins, convs, Cs, n_s, m_s = mixers(
        x_sample, state_conv.reshape(state_conv.shape[1:]), state_C.reshape(state_C.shape[1:]),
        state_n.reshape(state_n.shape[1:]), state_m.reshape(state_m.shape[1:]), nsa_s, 16, 16)

    def ffn_and_norm(h2):
        xt = rmsnorm_rows_t(h2, norm_ffn[l], bf16)
        peer_t = peer_ffn_t(xt, w_pq_t, peer_keys[l], u_bf, vt_bf)
        return final_norm(h2, peer_t, norm_final)

    y_prompt = ffn_and_norm(hp).reshape(x_prompt.shape)
    y_sample = ffn_and_norm(hs).reshape(x_sample.shape)
    st = lambda t: t[None]
    return (y_prompt, y_sample, st(kvp), st(kvs), st(winp), st(wins), st(convp), st(convs),
            st(Cp), st(Cs), st(n_p), st(n_s), st(m_p), st(m_s))
```

```python
import functools

import jax
import jax.numpy as jnp
import numpy as np
from jax import lax
from jax.experimental import pallas as pl
from jax.experimental.pallas import tpu as pltpu

D_MODEL = 4096
DEPTH = 1
D_M = D_MODEL // 2
DH_M = 256
NH_M = D_M // DH_M
CONV_W = 4
QKV_BLOCK = 4
D_N = D_MODEL // 2
DH_N = 128
H_N = D_N // DH_N
G_N = 4
J_N = H_N // G_N
KVW = G_N * DH_N
N_KV_SETS = 4
CMP_STRIDE = 16
CMP_LEN = 2 * CMP_STRIDE
CMP_HID = 2 * DH_N
SEL_LEN = 64
N_SEL = 16
WINDOW = 512
SCALE_N = DH_N ** -0.5
PEER_HEADS = 8
N_KEYS = 128
PEER_TOPK = 16
PEER_QDIM = 256
PEER_QHALF = PEER_QDIM // 2
IN_SIZES = (D_M, D_M, D_M, NH_M, NH_M, D_N, N_KV_SETS * KVW, 2 * KVW, 3 * H_N, D_MODEL, D_MODEL)
EPS = 1e-6
NEG = -1e30
BIG = 1e9
NEG_INF = float('-inf')

LANES = 128
VMEM_LIMIT_BYTES = 48 * 1024 * 1024
BIG_VMEM_LIMIT_BYTES = 56 * 1024 * 1024
PEER_SUB = 4
PREV_ROWS = 8
PAGES_PER_STEP = 16
NEW_ROWS = 16
QROWS = 8
SEL_CHUNK = 512
bf16 = jnp.bfloat16

_IN_OFFS = np.concatenate([[0], np.cumsum(IN_SIZES)]).tolist()
Z_XM, Z_VM, Z_OM = 0, D_M, 2 * D_M
Z_Q = 3 * D_M
Z_KV = Z_Q + D_N
Z_WIN = Z_KV + N_KV_SETS * KVW
Z_GM = Z_WIN + 2 * KVW
Z_GN = Z_GM + D_MODEL
Z_COLS = Z_GN + D_MODEL
ZG_IG, ZG_FG, ZG_GN = 0, NH_M, 2 * NH_M


def _pick(n, cands):
    for c in cands:
        if n % c == 0:
            return c
    return n


def _gelu_tanh(x):
    return 0.5 * x * (1.0 + jnp.tanh(np.float32(np.sqrt(2.0 / np.pi)) * (x + np.float32(0.044715) * (x * x * x))))


def _split3(x):
    hi = x.astype(bf16)
    r = x - hi.astype(jnp.float32)
    mid = r.astype(bf16)
    lo = (r - mid.astype(jnp.float32)).astype(bf16)
    return hi, mid, lo


def _rmsnorm_kernel(x_ref, g_ref, o_ref):
    x = x_ref[...]
    o_ref[...] = (x * lax.rsqrt(jnp.mean(x * x, axis=-1, keepdims=True) + EPS) * g_ref[...]).astype(o_ref.dtype)


def rmsnorm_rows(x2, g, out_dtype):
    n, d = x2.shape
    tr = _pick(n, (256, 128))
    return pl.pallas_call(
        _rmsnorm_kernel,
        grid=(n // tr,),
        in_specs=[pl.BlockSpec((tr, d), lambda i: (i, 0)), pl.BlockSpec((1, d), lambda i: (0, 0))],
        out_specs=pl.BlockSpec((tr, d), lambda i: (i, 0)),
        out_shape=jax.ShapeDtypeStruct((n, d), out_dtype),
        compiler_params=pltpu.CompilerParams(dimension_semantics=("parallel",), vmem_limit_bytes=VMEM_LIMIT_BYTES),
        name="rmsnorm",
    )(x2, g.reshape(1, d))


def _rmsnorm_t_kernel(x_ref, g_ref, o_ref):
    x = x_ref[...]
    o_ref[...] = (x * lax.rsqrt(jnp.mean(x * x, axis=-1, keepdims=True) + EPS) * g_ref[...]).T.astype(o_ref.dtype)


def rmsnorm_rows_t(x2, g, out_dtype):
    n, d = x2.shape
    tr = _pick(n, (256, 128))
    return pl.pallas_call(
        _rmsnorm_t_kernel,
        grid=(n // tr,),
        in_specs=[pl.BlockSpec((tr, d), lambda i: (i, 0)), pl.BlockSpec((1, d), lambda i: (0, 0))],
        out_specs=pl.BlockSpec((d, tr), lambda i: (0, i)),
        out_shape=jax.ShapeDtypeStruct((d, n), out_dtype),
        compiler_params=pltpu.CompilerParams(dimension_semantics=("parallel",), vmem_limit_bytes=VMEM_LIMIT_BYTES),
        name="rmsnorm_t",
    )(x2, g.reshape(1, d))


def _mm_kernel(a_ref, b_ref, o_ref):
    o_ref[...] = jnp.dot(a_ref[...], b_ref[...], preferred_element_type=jnp.float32).astype(o_ref.dtype)


def pmm(a, b, out_dtype=jnp.float32):
    m, kd = a.shape
    n = b.shape[1]
    tm = _pick(m, (1024, 512, 256, 128))
    tn = _pick(n, (1024, 512, 256, 128))
    return pl.pallas_call(
        _mm_kernel,
        grid=(m // tm, n // tn),
        in_specs=[pl.BlockSpec((tm, kd), lambda i, j: (i, 0)), pl.BlockSpec((kd, tn), lambda i, j: (0, j))],
        out_specs=pl.BlockSpec((tm, tn), lambda i, j: (i, j)),
        out_shape=jax.ShapeDtypeStruct((m, n), out_dtype),
        compiler_params=pltpu.CompilerParams(dimension_semantics=("parallel", "parallel"),
                                             vmem_limit_bytes=BIG_VMEM_LIMIT_BYTES),
        name="proj",
    )(a, b)


def _merge_kernel(ym_ref, yn_ref, wm_ref, wn_ref, gm_ref, gn_ref, o_ref):
    pm = jnp.dot(ym_ref[...], wm_ref[...], preferred_element_type=jnp.float32)
    pn = jnp.dot(yn_ref[...], wn_ref[...], preferred_element_type=jnp.float32)
    o_ref[...] = (jax.nn.sigmoid(gm_ref[...]) * pm + jax.nn.sigmoid(gn_ref[...]) * pn).astype(o_ref.dtype)


def merge_branches(ym, yn, wm, wn, z2):
    m = ym.shape[0]
    n = wm.shape[1]
    tm = _pick(m, (1024, 512, 256, 128))
    tn = 512
    return pl.pallas_call(
        _merge_kernel,
        grid=(m // tm, n // tn),
        in_specs=[pl.BlockSpec((tm, D_M), lambda i, j: (i, 0)), pl.BlockSpec((tm, D_N), lambda i, j: (i, 0)),
                  pl.BlockSpec((D_M, tn), lambda i, j: (0, j)), pl.BlockSpec((D_N, tn), lambda i, j: (0, j)),
                  pl.BlockSpec((tm, tn), lambda i, j: (i, Z_GM // tn + j)),
                  pl.BlockSpec((tm, tn), lambda i, j: (i, Z_GN // tn + j))],
        out_specs=pl.BlockSpec((tm, tn), lambda i, j: (i, j)),
        out_shape=jax.ShapeDtypeStruct((m, n), bf16),
        compiler_params=pltpu.CompilerParams(dimension_semantics=("parallel", "parallel"),
                                             vmem_limit_bytes=VMEM_LIMIT_BYTES),
        name="merge_branches",
    )(ym, yn, wm, wn, z2, z2)


def _mm_res_kernel(a_ref, b_ref, r_ref, o_ref):
    o_ref[...] = r_ref[...] + jnp.dot(a_ref[...], b_ref[...], preferred_element_type=jnp.float32)


def out_proj_residual(a, b, r):
    m, kd = a.shape
    n = b.shape[1]
    tm = _pick(m, (1024, 512, 256, 128))
    tn = 512
    return pl.pallas_call(
        _mm_res_kernel,
        grid=(m // tm, n // tn),
        in_specs=[pl.BlockSpec((tm, kd), lambda i, j: (i, 0)), pl.BlockSpec((kd, tn), lambda i, j: (0, j)),
                  pl.BlockSpec((tm, tn), lambda i, j: (i, j))],
        out_specs=pl.BlockSpec((tm, tn), lambda i, j: (i, j)),
        out_shape=jax.ShapeDtypeStruct((m, n), jnp.float32),
        compiler_params=pltpu.CompilerParams(dimension_semantics=("parallel", "parallel"),
                                             vmem_limit_bytes=VMEM_LIMIT_BYTES),
        name="out_proj",
    )(a, b, r)


def _final_kernel(h_ref, pt_ref, g_ref, o_ref):
    x = h_ref[...] + pt_ref[...].T
    o_ref[...] = x * lax.rsqrt(jnp.mean(x * x, axis=-1, keepdims=True) + EPS) * g_ref[...]


def final_norm(h2, peer_t, g):
    n, d = h2.shape
    tr = _pick(n, (256, 128))
    return pl.pallas_call(
        _final_kernel,
        grid=(n // tr,),
        in_specs=[pl.BlockSpec((tr, d), lambda i: (i, 0)), pl.BlockSpec((d, tr), lambda i: (0, i)),
                  pl.BlockSpec((1, d), lambda i: (0, 0))],
        out_specs=pl.BlockSpec((tr, d), lambda i: (i, 0)),
        out_shape=jax.ShapeDtypeStruct((n, d), jnp.float32),
        compiler_params=pltpu.CompilerParams(dimension_semantics=("parallel",), vmem_limit_bytes=VMEM_LIMIT_BYTES),
        name="final_norm",
    )(h2, peer_t, g.reshape(1, d))


def _blockdiag_apply(x, w_ref, which):
    parts = [jnp.dot(x[:, hb * LANES:(hb + 1) * LANES].astype(bf16), w_ref[which, hb].astype(bf16),
                     preferred_element_type=jnp.float32) for hb in range(DH_M // LANES)]
    return jnp.concatenate(parts, axis=1)


def _mlstm_kernel(xm_ref, vm_ref, om_ref, zg_ref, prev_ref, wc_ref, bc_ref, wbd_ref, gb_ref, c0_ref, n0_ref, m0_ref,
                  hn_ref, sk_ref, y_ref, c1_ref, n1_ref, m1_ref, *, ch, l_true):
    h = pl.program_id(1)
    lp = xm_ref.shape[0]
    nchunk = lp // ch
    f32 = jnp.float32
    tri = lax.broadcasted_iota(jnp.int32, (ch, ch), 0) >= lax.broadcasted_iota(jnp.int32, (ch, ch), 1)
    tri_bf = tri.astype(bf16)
    lane_g = lax.broadcasted_iota(jnp.int32, (ch, LANES), 1)
    sub_g = lax.broadcasted_iota(jnp.int32, (LANES, ch), 0)

    def chunk(ci, carry):
        c_st, n_st, m_st = carry
        r0 = pl.multiple_of(ci * ch, ch)
        xm = xm_ref[pl.ds(r0, ch), :]
        before = xm_ref[pl.ds(pl.multiple_of(jnp.maximum(r0 - PREV_ROWS, 0), PREV_ROWS), PREV_ROWS), :]
        before = jnp.where(ci == 0, prev_ref[...], before)
        xe = jnp.concatenate([before, xm], axis=0)
        lo = PREV_ROWS - (CONV_W - 1)
        conv = bc_ref[...] + sum(xe[lo + w:lo + w + ch, :] * wc_ref[w:w + 1, :] for w in range(CONV_W))
        c = conv * jax.nn.sigmoid(conv)
        q = _blockdiag_apply(c, wbd_ref, 0)
        k = _blockdiag_apply(c, wbd_ref, 1) * (DH_M ** -0.5)
        v = _blockdiag_apply(vm_ref[pl.ds(r0, ch), :], wbd_ref, 2)
        zg = zg_ref[pl.ds(r0, ch), :] + gb_ref[...]
        live = (r0 + lax.broadcasted_iota(jnp.int32, (ch, LANES), 0)) < l_true
        a_all = jnp.where(live, zg, NEG_INF)
        lf_all = jnp.where(live, jnp.minimum(zg, 0.0) - jnp.log1p(jnp.exp(-jnp.abs(zg))), 0.0)
        b_all = jnp.zeros((ch, LANES), f32)
        for part in _split3(lf_all):
            b_all = b_all + jnp.dot(tri_bf, part, preferred_element_type=f32)
        a_col = jnp.sum(jnp.where(lane_g == ZG_IG + h, a_all, 0.0), axis=1, keepdims=True)
        b_col = jnp.sum(jnp.where(lane_g == ZG_FG + h, b_all, 0.0), axis=1, keepdims=True)
        a_row = jnp.sum(jnp.where(sub_g == ZG_IG + h, a_all.T, 0.0), axis=0, keepdims=True)
        b_row = jnp.sum(jnp.where(sub_g == ZG_FG + h, b_all.T, 0.0), axis=0, keepdims=True)
        dmat = jnp.where(tri, b_col - b_row + a_row, NEG_INF)
        inter = b_col + m_st
        m = jnp.maximum(inter, jnp.max(dmat, axis=1, keepdims=True))
        w_intra = jnp.exp(dmat - m)
        w_inter = jnp.exp(inter - m)
        qb = q.astype(bf16)
        kb = k.astype(bf16)
        vb = v.astype(bf16)
        qk = lax.dot_general(qb, kb, (((1,), (1,)), ((), ())), preferred_element_type=f32) * w_intra
        num = w_inter * jnp.dot(qb, c_st.astype(bf16), preferred_element_type=f32) \
            + jnp.dot(qk.astype(bf16), vb, preferred_element_type=f32)
        den = w_inter * jnp.sum(q * n_st, axis=1, keepdims=True) + jnp.sum(qk, axis=1, keepdims=True)
        hh = num / jnp.maximum(jnp.abs(den), jnp.exp(-m))
        m_end = m[ch - 1:ch, :]
        b_end = b_col[ch - 1:ch, :]
        w_end = jnp.exp(b_end - b_col + a_col - m_end)
        decay = jnp.exp(b_end + m_st - m_end)
        kw = k * w_end
        c_new = decay * c_st + lax.dot_general(kw.astype(bf16), vb, (((0,), (0,)), ((), ())),
                                               preferred_element_type=f32)
        n_new = decay * n_st + jnp.sum(kw, axis=0, keepdims=True)
        hg = hh * jax.nn.sigmoid(om_ref[pl.ds(r0, ch), :])
        mu = jnp.mean(hg, axis=1, keepdims=True)
        var = jnp.mean(jnp.square(hg - mu), axis=1, keepdims=True)
        y = (hg - mu) * lax.rsqrt(var + EPS) * hn_ref[...] + sk_ref[...] * c
        y_ref[pl.ds(r0, ch), :] = y.astype(y_ref.dtype)
        return c_new, n_new, m_end

    c_fin, n_fin, m_fin = lax.fori_loop(0, nchunk, chunk, (c0_ref[...], n0_ref[...], m0_ref[:, :1]))
    c1_ref[...] = c_fin
    n1_ref[...] = n_fin
    m1_ref[...] = jnp.broadcast_to(m_fin, m1_ref.shape)


def expand_blockdiag(w_bd):
    per = LANES // QKV_BLOCK
    w = w_bd.reshape(3, D_M // LANES, per, QKV_BLOCK, QKV_BLOCK)
    full = jnp.einsum('tbnio,nm->tbnimo', w, jnp.eye(per, dtype=w_bd.dtype))
    return full.reshape(3, D_M // LANES, LANES, LANES)


def mlstm_branch(z3, zg3, col_blocks, prev8, w_conv, b_conv, wbd_full, gate_bias, c0, n0, m0, w_hnorm, w_skip,
                 l_true, ch):
    b, lp, _ = z3.shape
    cx, cv, co = col_blocks
    kern = functools.partial(_mlstm_kernel, ch=ch, l_true=l_true)
    col = lambda c0_: pl.BlockSpec((None, lp, DH_M), lambda bi, h: (bi, 0, c0_ + h))
    vec = pl.BlockSpec((1, DH_M), lambda bi, h: (0, h))
    st = lambda r, c: pl.BlockSpec((None, None, r, c), lambda bi, h: (bi, h, 0, 0))
    return pl.pallas_call(
        kern,
        grid=(b, NH_M),
        in_specs=[col(cx), col(cv), col(co),
                  pl.BlockSpec((None, lp, LANES), lambda bi, h: (bi, 0, 0)),
                  pl.BlockSpec((None, PREV_ROWS, DH_M), lambda bi, h: (bi, 0, h)),
                  pl.BlockSpec((CONV_W, DH_M), lambda bi, h: (0, h)),
                  vec,
                  pl.BlockSpec((3, DH_M // LANES, LANES, LANES), lambda bi, h: (0, h, 0, 0)),
                  pl.BlockSpec((1, LANES), lambda bi, h: (0, 0)),
                  st(DH_M, DH_M), st(1, DH_M), st(1, LANES), vec, vec],
        out_specs=[pl.BlockSpec((None, lp, DH_M), lambda bi, h: (bi, 0, h)),
                   st(DH_M, DH_M), st(1, DH_M), st(1, LANES)],
        out_shape=[jax.ShapeDtypeStruct((b, lp, D_M), bf16),
                   jax.ShapeDtypeStruct((b, NH_M, DH_M, DH_M), jnp.float32),
                   jax.ShapeDtypeStruct((b, NH_M, 1, DH_M), jnp.float32),
                   jax.ShapeDtypeStruct((b, NH_M, 1, LANES), jnp.float32)],
        compiler_params=pltpu.CompilerParams(dimension_semantics=("parallel", "parallel"),
                                             vmem_limit_bytes=VMEM_LIMIT_BYTES),
        name="mlstm_branch",
    )(z3, z3, z3, zg3, prev8, w_conv, b_conv.reshape(1, D_M), wbd_full, gate_bias, c0, n0, m0,
      w_hnorm.reshape(1, D_M), w_skip.reshape(1, D_M))


def _compress_kernel(x_ref, pe_ref, w1_ref, w2_ref, o_ref):
    nchunk = x_ref.shape[0] // CMP_STRIDE
    first = jnp.zeros((nchunk, CMP_HID), jnp.float32)
    second = jnp.zeros((nchunk, CMP_HID), jnp.float32)
    for l in range(CMP_STRIDE):
        xl = x_ref[pl.ds(l, nchunk, stride=CMP_STRIDE), :]
        first += jnp.dot((xl + pe_ref[l:l + 1, :]).astype(bf16), w1_ref[l].astype(bf16),
                         preferred_element_type=jnp.float32)
        second += jnp.dot((xl + pe_ref[CMP_STRIDE + l:CMP_STRIDE + l + 1, :]).astype(bf16),
                          w1_ref[CMP_STRIDE + l].astype(bf16), preferred_element_type=jnp.float32)
    nxt = jnp.concatenate([second[1:], jnp.zeros((1, CMP_HID), jnp.float32)], axis=0)
    row = lax.broadcasted_iota(jnp.int32, (nchunk, CMP_HID), 0)
    hid = jnp.where(row < nchunk - 1, _gelu_tanh(first + nxt), 0.0)
    o_ref[...] = jnp.dot(hid.astype(bf16), w2_ref[...].astype(bf16), preferred_element_type=jnp.float32)


def compress_prompt(z3, col0, cmp_pe, cmp_w1, cmp_w2):
    b, l, _ = z3.shape
    nchunk = l // CMP_STRIDE
    return pl.pallas_call(
        _compress_kernel,
        grid=(b, 2, G_N),
        in_specs=[pl.BlockSpec((None, l, DH_N), lambda bi, s, g: (bi, 0, col0 + s * G_N + g)),
                  pl.BlockSpec((None, CMP_LEN, DH_N), lambda bi, s, g: (s, 0, 0)),
                  pl.BlockSpec((None, CMP_LEN, DH_N, CMP_HID), lambda bi, s, g: (s, 0, 0, 0)),
                  pl.BlockSpec((None, CMP_HID, DH_N), lambda bi, s, g: (s, 0, 0))],
        out_specs=pl.BlockSpec((None, None, None, nchunk, DH_N), lambda bi, s, g: (bi, s, g, 0, 0)),
        out_shape=jax.ShapeDtypeStruct((b, 2, G_N, nchunk, DH_N), jnp.float32),
        compiler_params=pltpu.CompilerParams(dimension_semantics=("parallel", "parallel", "parallel"),
                                             vmem_limit_bytes=VMEM_LIMIT_BYTES),
        name="nsa_compress",
    )(z3, cmp_pe, cmp_w1, cmp_w2)


def _softmax_pv(s, keep, v):
    s = jnp.where(keep, s, NEG)
    m = jnp.max(s, axis=-1, keepdims=True)
    e = jnp.exp(s - m)
    den = jnp.sum(e, axis=-1, keepdims=True)
    return jnp.dot(e.astype(bf16), v, preferred_element_type=jnp.float32) / den


def _nsa_prompt_kernel(q_ref, kc_ref, vc_ref, ks_ref, vs_ref, kw_ref, vw_ref, gn_ref, o_ref, *, tq):
    g = pl.program_id(1)
    qi = pl.program_id(2)
    f32 = jnp.float32
    t = ks_ref.shape[0]
    n_c = kc_ref.shape[0]
    n_s = t // SEL_LEN
    nq = J_N * tq
    q4 = jnp.concatenate([q_ref[:, j * DH_N:(j + 1) * DH_N] for j in range(J_N)], axis=0).astype(bf16)
    nt = (((1,), (1,)), ((), ()))

    def qpos_of(shape):
        return qi * tq + lax.broadcasted_iota(jnp.int32, shape, 0) % tq

    cidx = lax.broadcasted_iota(jnp.int32, (nq, n_c), 1)
    cmask = cidx * CMP_STRIDE + CMP_LEN <= qpos_of((nq, n_c)) + 1
    s = lax.dot_general(q4, kc_ref[...].astype(bf16), nt, preferred_element_type=f32) * SCALE_N
    s = jnp.where(cmask, s, NEG)
    m = jnp.max(s, axis=-1, keepdims=True)
    e = jnp.where(cmask, jnp.exp(s - m), 0.0)
    den = jnp.sum(e, axis=-1, keepdims=True)
    p = e / jnp.where(den > 0.0, den, 1.0)
    o_cmp = jnp.dot(p.astype(bf16), vc_ref[...].astype(bf16), preferred_element_type=f32)
    p_sum = p[0:tq]
    for j in range(1, J_N):
        p_sum = p_sum + p[j * tq:(j + 1) * tq]
    ci = lax.broadcasted_iota(jnp.int32, (n_c, LANES), 0)
    si = lax.broadcasted_iota(jnp.int32, (n_c, LANES), 1)
    cover = jnp.logical_and(ci * CMP_STRIDE < si * SEL_LEN + SEL_LEN,
                            ci * CMP_STRIDE + CMP_LEN > si * SEL_LEN).astype(bf16)
    imp = jnp.zeros((tq, LANES), f32)
    for part in _split3(p_sum):
        imp = imp + jnp.dot(part, cover, preferred_element_type=f32)
    blk = lax.broadcasted_iota(jnp.int32, (tq, LANES), 1)
    cur = (qi * tq + lax.broadcasted_iota(jnp.int32, (tq, LANES), 0)) // SEL_LEN
    valid = blk <= cur
    forced = jnp.logical_or(blk == 0, blk >= cur - 1)
    score = jnp.where(valid, imp + jnp.where(forced, BIG, 0.0), -BIG)
    score = jnp.where(blk < n_s, score, NEG_INF)
    rank = jnp.zeros((tq, LANES), jnp.int32)
    for s2 in range(n_s):
        col = score[:, s2:s2 + 1]
        ahead = jnp.logical_or(col > score, jnp.logical_and(col == score, blk > s2))
        rank = rank + ahead.astype(jnp.int32)
    sel = (rank < N_SEL).astype(bf16)

    ck = SEL_CHUNK
    s_row = lax.broadcasted_iota(jnp.int32, (LANES, ck), 0)
    k_lane = lax.broadcasted_iota(jnp.int32, (LANES, ck), 1)
    kpos1 = lax.broadcasted_iota(jnp.int32, (tq, ck), 1)
    qpos1 = qi * tq + lax.broadcasted_iota(jnp.int32, (tq, ck), 0)

    def kchunk(kb, carry):
        m_o, l_o, acc = carry
        k0 = pl.multiple_of(kb * ck, ck)
        expand = ((k0 + k_lane) // SEL_LEN == s_row).astype(bf16)
        picked = jnp.dot(sel, expand, preferred_element_type=f32)
        keep1 = jnp.logical_and(picked > 0.5, k0 + kpos1 <= qpos1)
        keep = jnp.concatenate([keep1] * J_N, axis=0)
        kk = ks_ref[pl.ds(k0, ck), :].astype(bf16)
        vv = vs_ref[pl.ds(k0, ck), :].astype(bf16)
        sc = lax.dot_general(q4, kk, nt, preferred_element_type=f32) * SCALE_N
        sc = jnp.where(keep, sc, NEG)
        m_n = jnp.maximum(m_o, jnp.max(sc, axis=-1, keepdims=True))
        alpha = jnp.exp(m_o - m_n)
        ee = jnp.where(keep, jnp.exp(sc - m_n), 0.0)
        l_n = alpha * l_o + jnp.sum(ee, axis=-1, keepdims=True)
        return m_n, l_n, alpha * acc + jnp.dot(ee.astype(bf16), vv, preferred_element_type=f32)

    nkc = (qi * tq + tq + ck - 1) // ck
    _, l_f, acc_f = lax.fori_loop(0, nkc, kchunk, (jnp.full((nq, 1), NEG, f32), jnp.zeros((nq, 1), f32),
                                                   jnp.zeros((nq, DH_N), f32)))
    o_sel = acc_f / l_f

    span = WINDOW + tq
    w0 = pl.multiple_of(jnp.clip(qi * tq - WINDOW, 0, t - span), tq)
    kw = kw_ref[pl.ds(w0, span), :].astype(bf16)
    vw = vw_ref[pl.ds(w0, span), :].astype(bf16)
    qpos_w = qpos_of((nq, span))
    kpos_w = w0 + lax.broadcasted_iota(jnp.int32, (nq, span), 1)
    keep_win = jnp.logical_and(kpos_w <= qpos_w, kpos_w > qpos_w - WINDOW)
    sw = lax.dot_general(q4, kw, nt, preferred_element_type=f32) * SCALE_N
    o_win = _softmax_pv(sw, keep_win, vw)

    gates = jax.nn.sigmoid(gn_ref[...])
    gl = lax.broadcasted_iota(jnp.int32, gates.shape, 1)
    outs = []
    for j in range(J_N):
        acc = jnp.zeros((tq, DH_N), f32)
        for which, o in enumerate((o_cmp, o_sel, o_win)):
            col = jnp.sum(jnp.where(gl == ZG_GN + (g * J_N + j) * 3 + which, gates, 0.0), axis=-1, keepdims=True)
            acc = acc + col * o[j * tq:(j + 1) * tq]
        outs.append(acc)
    o_ref[...] = jnp.concatenate(outs, axis=1).astype(o_ref.dtype)


def nsa_prompt(z3, zg3, kcv, tq=128):
    b, l, _ = z3.shape
    assert l >= WINDOW + tq and l % tq == 0 and l % SEL_CHUNK == 0
    nq = l // tq
    n_c = kcv.shape[3]
    full = lambda c0: pl.BlockSpec((None, l, DH_N), lambda bi, g, qi: (bi, 0, c0 // DH_N + g))
    return pl.pallas_call(
        functools.partial(_nsa_prompt_kernel, tq=tq),
        grid=(b, G_N, nq),
        in_specs=[pl.BlockSpec((None, tq, J_N * DH_N), lambda bi, g, qi: (bi, qi, Z_Q // (J_N * DH_N) + g)),
                  pl.BlockSpec((None, None, None, n_c, DH_N), lambda bi, g, qi: (bi, 0, g, 0, 0)),
                  pl.BlockSpec((None, None, None, n_c, DH_N), lambda bi, g, qi: (bi, 1, g, 0, 0)),
                  full(Z_KV + 2 * KVW), full(Z_KV + 3 * KVW), full(Z_WIN), full(Z_WIN + KVW),
                  pl.BlockSpec((None, tq, LANES), lambda bi, g, qi: (bi, qi, 0))],
        out_specs=pl.BlockSpec((None, tq, J_N * DH_N), lambda bi, g, qi: (bi, qi, g)),
        out_shape=jax.ShapeDtypeStruct((b, l, D_N), bf16),
        compiler_params=pltpu.CompilerParams(dimension_semantics=("parallel", "parallel", "arbitrary"),
                                             vmem_limit_bytes=VMEM_LIMIT_BYTES),
        name="nsa_prompt",
    )(z3, kcv, kcv, z3, z3, z3, z3, zg3)


def _paged_compress_kernel(pt_ref, *refs, page, nstep):
    pages = refs[:PAGES_PER_STEP]
    new_ref, pe_ref, w1_ref, w2_ref, o_ref, carry_ref = refs[PAGES_PER_STEP:]
    p = pl.program_id(1)
    nsg = 2 * G_N
    cpp = page // CMP_STRIDE
    nch = PAGES_PER_STEP * cpp
    row = lax.broadcasted_iota(jnp.int32, (nch, CMP_HID), 0)
    flat = [pg.reshape(page * nsg, DH_N) for pg in pages]

    @pl.when(p < nstep)
    def _():
        for s in range(2):
            first = jnp.zeros((G_N * nch, CMP_HID), jnp.float32)
            second = jnp.zeros((G_N * nch, CMP_HID), jnp.float32)
            for k in range(CMP_STRIDE // 2):
                xs = [jnp.concatenate([flat[r][pl.ds(l * nsg + s * G_N + g, cpp, stride=CMP_STRIDE * nsg), :]
                                       for g in range(G_N) for r in range(PAGES_PER_STEP)], axis=0)
                      for l in (2 * k, 2 * k + 1)]
                lhs = lambda off: jnp.concatenate(
                    [(xs[i] + pe_ref[s, off + 2 * k + i:off + 2 * k + i + 1, :]).astype(bf16) for i in range(2)], axis=1)
                first += jnp.dot(lhs(0), w1_ref[s, k], preferred_element_type=jnp.float32)
                second += jnp.dot(lhs(CMP_STRIDE), w1_ref[s, CMP_STRIDE // 2 + k], preferred_element_type=jnp.float32)
            for g in range(G_N):
                f_g = first[g * nch:(g + 1) * nch]
                s_g = second[g * nch:(g + 1) * nch]
                prev = carry_ref[s * G_N + g, 0:1, :]
                shifted = jnp.where(row == 0, prev, pltpu.roll(f_g, 1, axis=0))
                carry_ref[s * G_N + g, 0:1, :] = f_g[nch - 1:nch, :]
                hid = _gelu_tanh(shifted + s_g)
                hid = jnp.where(jnp.logical_and(row == 0, p == 0), 0.0, hid)
                o_ref[s, g] = jnp.dot(hid.astype(bf16), w2_ref[s].astype(bf16), preferred_element_type=jnp.float32)

    @pl.when(p == nstep)
    def _():
        for s in range(2):
            for g in range(G_N):
                c0 = (s * G_N + g) * DH_N
                sec = jnp.zeros((QROWS, CMP_HID), jnp.float32)
                for l in range(CMP_STRIDE):
                    x = jnp.broadcast_to(new_ref[l:l + 1, c0:c0 + DH_N] + pe_ref[s, CMP_STRIDE + l:CMP_STRIDE + l + 1, :],
                                         (QROWS, DH_N))
                    w_l = w1_ref[s, (CMP_STRIDE + l) // 2, (l % 2) * DH_N:(l % 2 + 1) * DH_N, :]
                    sec += jnp.dot(x.astype(bf16), w_l, preferred_element_type=jnp.float32)
                hid = _gelu_tanh(carry_ref[s * G_N + g, 0:1, :] + sec)
                blk = jnp.dot(hid.astype(bf16), w2_ref[s].astype(bf16), preferred_element_type=jnp.float32)
                rows = lax.broadcasted_iota(jnp.int32, (nch, DH_N), 0)
                o_ref[s, g] = jnp.where(rows == 0, jnp.broadcast_to(blk[0:1, :], (nch, DH_N)), 0.0)


def paged_compress(page_table, cache3, page, new_c, cmp_pe, w1_bf, cmp_w2):
    b, npages = page_table.shape
    assert page % CMP_STRIDE == 0 and npages % PAGES_PER_STEP == 0 and new_c.shape[1] == NEW_ROWS == CMP_STRIDE
    nstep = npages // PAGES_PER_STEP
    nch = PAGES_PER_STEP * page // CMP_STRIDE

    def page_spec(r):
        return pl.BlockSpec((page, 2 * G_N, DH_N),
                            lambda bi, p, pt: (pt[bi, jnp.minimum(p * PAGES_PER_STEP + r, npages - 1)], 0, 0))

    grid_spec = pltpu.PrefetchScalarGridSpec(
        num_scalar_prefetch=1,
        grid=(b, nstep + 1),
        in_specs=[page_spec(r) for r in range(PAGES_PER_STEP)] + [
            pl.BlockSpec((None, NEW_ROWS, 2 * KVW), lambda bi, p, pt: (bi, 0, 0)),
            pl.BlockSpec((2, CMP_LEN, DH_N), lambda bi, p, pt: (0, 0, 0)),
            pl.BlockSpec((2, CMP_LEN // 2, 2 * DH_N, CMP_HID), lambda bi, p, pt: (0, 0, 0, 0)),
            pl.BlockSpec((2, CMP_HID, DH_N), lambda bi, p, pt: (0, 0, 0))],
        out_specs=pl.BlockSpec((None, 2, G_N, nch, DH_N), lambda bi, p, pt: (bi, 0, 0, p, 0)),
        scratch_shapes=[pltpu.VMEM((2 * G_N, QROWS, CMP_HID), jnp.float32)])
    return pl.pallas_call(
        functools.partial(_paged_compress_kernel, page=page, nstep=nstep),
        grid_spec=grid_spec,
        out_shape=jax.ShapeDtypeStruct((b, 2, G_N, (nstep + 1) * nch, DH_N), jnp.float32),
        compiler_params=pltpu.CompilerParams(dimension_semantics=("parallel", "arbitrary"),
                                             vmem_limit_bytes=VMEM_LIMIT_BYTES),
        name="nsa_paged_compress",
    )(page_table, *([cache3] * PAGES_PER_STEP), new_c, cmp_pe, w1_bf, cmp_w2)


def _paged_attend_kernel(pt_ref, *refs, page, nstep, l_true):
    pages = refs[:PAGES_PER_STEP]
    (q_ref, kcv_ref, new_ref, winp_ref, wnew_ref, zg_ref, o_ref,
     sel_ref, ocmp_ref, m_ref, l_ref, acc_ref) = refs[PAGES_PER_STEP:]
    p = pl.program_id(1)
    past = nstep * PAGES_PER_STEP * page
    tk = PAGES_PER_STEP * page
    n_r = kcv_ref.shape[2]
    n_s = -(-(past + l_true) // SEL_LEN)
    sl = 2 * LANES
    assert n_s <= sl
    nq = J_N * QROWS
    f32 = jnp.float32
    nt = (((1,), (1,)), ((), ()))
    nsg = 2 * G_N
    wbuf = winp_ref.shape[0]
    flat = [pg.reshape(page * nsg, DH_N) for pg in pages]
    win_flat = winp_ref.reshape(wbuf * nsg, DH_N)

    def q_of(g):
        return jnp.concatenate([q_ref[:, (g * J_N + j) * DH_N:(g * J_N + j + 1) * DH_N] for j in range(J_N)],
                               axis=0).astype(bf16)

    def pos_of(shape):
        return lax.broadcasted_iota(jnp.int32, shape, 0) % QROWS

    def online(g, s, keep, v):
        s = jnp.where(keep, s, NEG)
        m_old = m_ref[g]
        m_new = jnp.maximum(m_old, jnp.max(s, axis=-1, keepdims=True))
        alpha = jnp.exp(m_old - m_new)
        e = jnp.where(keep, jnp.exp(s - m_new), 0.0)
        l_ref[g] = alpha * l_ref[g] + jnp.sum(e, axis=-1, keepdims=True)
        acc_ref[g] = alpha * acc_ref[g] + jnp.dot(e.astype(bf16), v, preferred_element_type=f32)
        m_ref[g] = m_new

    @pl.when(p == 0)
    def _():
        r_i = lax.broadcasted_iota(jnp.int32, (nq, n_r), 1)
        qpos = past + pos_of((nq, n_r))
        cmask = jnp.logical_and(r_i >= 1, (r_i - 1) * CMP_STRIDE + CMP_LEN <= qpos + 1)
        ci = lax.broadcasted_iota(jnp.int32, (n_r, sl), 0) - 1
        si = lax.broadcasted_iota(jnp.int32, (n_r, sl), 1)
        cover = jnp.logical_and(jnp.logical_and(ci >= 0, ci * CMP_STRIDE < si * SEL_LEN + SEL_LEN),
                                ci * CMP_STRIDE + CMP_LEN > si * SEL_LEN).astype(bf16)
        blk = lax.broadcasted_iota(jnp.int32, (QROWS, sl), 1)
        cur = (past + lax.broadcasted_iota(jnp.int32, (QROWS, sl), 0)) // SEL_LEN
        valid = blk <= cur
        forced = jnp.logical_or(blk == 0, blk >= cur - 1)
        s_src = lax.broadcasted_iota(jnp.int32, (sl, sl), 0)
        s_dst = lax.broadcasted_iota(jnp.int32, (sl, sl), 1)
        for g in range(G_N):
            q = q_of(g)
            kc = kcv_ref[0, g].astype(bf16)
            vc = kcv_ref[1, g].astype(bf16)
            s = lax.dot_general(q, kc, nt, preferred_element_type=f32) * SCALE_N
            s = jnp.where(cmask, s, NEG)
            m = jnp.max(s, axis=-1, keepdims=True)
            e = jnp.where(cmask, jnp.exp(s - m), 0.0)
            den = jnp.sum(e, axis=-1, keepdims=True)
            pr = e / jnp.where(den > 0.0, den, 1.0)
            ocmp_ref[g] = jnp.dot(pr.astype(bf16), vc, preferred_element_type=f32)
            p_sum = pr[0:QROWS]
            for j in range(1, J_N):
                p_sum = p_sum + pr[j * QROWS:(j + 1) * QROWS]
            imp = jnp.zeros((QROWS, sl), f32)
            for part in _split3(p_sum):
                imp = imp + jnp.dot(part, cover, preferred_element_type=f32)
            score = jnp.where(valid, imp + jnp.where(forced, BIG, 0.0), -BIG)
            score = jnp.where(blk < n_s, score, NEG_INF)
            score_t = score.T
            sel_rows = []
            for qi in range(QROWS):
                col = score_t[:, qi:qi + 1]
                rw = score[qi:qi + 1, :]
                ahead = jnp.logical_or(col > rw, jnp.logical_and(col == rw, s_src < s_dst))
                rank = jnp.sum(ahead.astype(jnp.int32), axis=0, keepdims=True)
                sel_rows.append((rank < N_SEL).astype(f32))
            sel_ref[g] = jnp.concatenate(sel_rows, axis=0)
            m_ref[g] = jnp.full((nq, 1), NEG, f32)
            l_ref[g] = jnp.zeros((nq, 1), f32)
            acc_ref[g] = jnp.zeros((nq, DH_N), f32)

    @pl.when(p < nstep)
    def _():
        k0 = p * tk
        expand = ((k0 + lax.broadcasted_iota(jnp.int32, (sl, tk), 1)) // SEL_LEN
                  == lax.broadcasted_iota(jnp.int32, (sl, tk), 0)).astype(bf16)
        for g in range(G_N):
            picked = jnp.dot(sel_ref[g].astype(bf16), expand, preferred_element_type=f32)
            keep = jnp.concatenate([picked] * J_N, axis=0) > 0.5
            kk = jnp.concatenate([pg[pl.ds(g, page, stride=nsg), :] for pg in flat], axis=0).astype(bf16)
            vv = jnp.concatenate([pg[pl.ds(G_N + g, page, stride=nsg), :] for pg in flat], axis=0).astype(bf16)
            s = lax.dot_general(q_of(g), kk, nt, preferred_element_type=f32) * SCALE_N
            online(g, s, keep, vv)

    @pl.when(p == nstep)
    def _():
        gates = jax.nn.sigmoid(zg_ref[...])
        gl = lax.broadcasted_iota(jnp.int32, gates.shape, 1)
        pos_n = pos_of((nq, NEW_ROWS))
        l_n = lax.broadcasted_iota(jnp.int32, (nq, NEW_ROWS), 1)
        keep_new = jnp.logical_and(l_n <= pos_n, l_n < l_true)
        pos_w = pos_of((nq, wbuf + NEW_ROWS))
        i_w = lax.broadcasted_iota(jnp.int32, (nq, wbuf + NEW_ROWS), 1)
        keep_win = jnp.logical_or(
            jnp.logical_and(i_w < wbuf, i_w - wbuf > pos_w - WINDOW),
            jnp.logical_and(i_w >= wbuf, jnp.logical_and(i_w - wbuf <= pos_w, i_w - wbuf < l_true)))
        for g in range(G_N):
            q = q_of(g)
            kn = new_ref[:, g * DH_N:(g + 1) * DH_N].astype(bf16)
            vn = new_ref[:, KVW + g * DH_N:KVW + (g + 1) * DH_N].astype(bf16)
            s = lax.dot_general(q, kn, nt, preferred_element_type=f32) * SCALE_N
            online(g, s, keep_new, vn)
            o_sel = acc_ref[g] / l_ref[g]
            kw = jnp.concatenate([win_flat[pl.ds(g, wbuf, stride=nsg), :], wnew_ref[:, g * DH_N:(g + 1) * DH_N]],
                                 axis=0).astype(bf16)
            vw = jnp.concatenate([win_flat[pl.ds(G_N + g, wbuf, stride=nsg), :],
                                  wnew_ref[:, KVW + g * DH_N:KVW + (g + 1) * DH_N]], axis=0).astype(bf16)
            s = lax.dot_general(q, kw, nt, preferred_element_type=f32) * SCALE_N
            o_win = _softmax_pv(s, keep_win, vw)
            o_cmp = ocmp_ref[g]
            for j in range(J_N):
                acc = jnp.zeros((QROWS, DH_N), f32)
                for which, o in enumerate((o_cmp, o_sel, o_win)):
                    col = jnp.sum(jnp.where(gl == ZG_GN + (g * J_N + j) * 3 + which, gates, 0.0), axis=-1, keepdims=True)
                    acc = acc + col * o[j * QROWS:(j + 1) * QROWS]
                o_ref[:, (g * J_N + j) * DH_N:(g * J_N + j + 1) * DH_N] = acc.astype(o_ref.dtype)


def paged_attend(page_table, cache3, page, q8, kcv, new_s, winp, wnew, zg8, l_true):
    b, npages = page_table.shape
    nstep = npages // PAGES_PER_STEP
    assert winp.shape[0] == b * WINDOW and l_true <= QROWS and npages % PAGES_PER_STEP == 0
    n_r = kcv.shape[3]
    nq = J_N * QROWS

    def page_spec(r):
        return pl.BlockSpec((page, 2 * G_N, DH_N),
                            lambda bi, p, pt: (pt[bi, jnp.minimum(p * PAGES_PER_STEP + r, npages - 1)], 1, 0))

    per_b = lambda rows, cols: pl.BlockSpec((None, rows, cols), lambda bi, p, pt: (bi, 0, 0))
    grid_spec = pltpu.PrefetchScalarGridSpec(
        num_scalar_prefetch=1,
        grid=(b, nstep + 1),
        in_specs=[page_spec(r) for r in range(PAGES_PER_STEP)] + [
            per_b(QROWS, D_N),
            pl.BlockSpec((None, 2, G_N, n_r, DH_N), lambda bi, p, pt: (bi, 0, 0, 0, 0)),
            per_b(NEW_ROWS, 2 * KVW), pl.BlockSpec((WINDOW, 2 * G_N, DH_N), lambda bi, p, pt: (bi, 0, 0)),
            per_b(NEW_ROWS, 2 * KVW), per_b(QROWS, LANES)],
        out_specs=per_b(QROWS, D_N),
        scratch_shapes=[pltpu.VMEM((G_N, QROWS, 2 * LANES), jnp.float32),
                        pltpu.VMEM((G_N, nq, DH_N), jnp.float32),
                        pltpu.VMEM((G_N, nq, 1), jnp.float32),
                        pltpu.VMEM((G_N, nq, 1), jnp.float32),
                        pltpu.VMEM((G_N, nq, DH_N), jnp.float32)])
    return pl.pallas_call(
        functools.partial(_paged_attend_kernel, page=page, nstep=nstep, l_true=l_true),
        grid_spec=grid_spec,
        out_shape=jax.ShapeDtypeStruct((b, QROWS, D_N), jnp.float32),
        compiler_params=pltpu.CompilerParams(dimension_semantics=("parallel", "arbitrary"),
                                             vmem_limit_bytes=VMEM_LIMIT_BYTES),
        name="nsa_paged_attend",
    )(page_table, *([cache3] * PAGES_PER_STEP), q8, kcv, new_s, winp, wnew, zg8)


def _extract_top(s, k):
    rows = lax.broadcasted_iota(jnp.int32, s.shape, 0)
    nrow = s.shape[0]
    work = s
    taken = jnp.zeros(s.shape, jnp.bool_)
    tops = []
    for _ in range(k):
        m = jnp.max(work, axis=0, keepdims=True)
        first = jnp.min(jnp.where(work == m, rows, nrow), axis=0, keepdims=True)
        hit = rows == first
        taken = jnp.logical_or(taken, hit)
        work = jnp.where(hit, NEG_INF, work)
        tops.append(m)
    return tops, taken


def _peer_route_kernel(qt_ref, keys_ref, s1_ref, e1_ref, s2_ref, e2_ref, tau_ref):
    halves = []
    for c in range(2):
        q = qt_ref[c * PEER_QHALF:(c + 1) * PEER_QHALF, :].astype(bf16)
        s = jnp.dot(keys_ref[c].astype(bf16), q, preferred_element_type=jnp.float32)
        tops, taken = _extract_top(s, PEER_TOPK)
        halves.append((s, tops, taken))
    (sa, ta, ma), (sb, tb, mb) = halves
    assert PEER_TOPK == 16
    ta_col = jnp.concatenate(ta, axis=0)
    tb_col = jnp.concatenate(tb, axis=0)
    a_idx = lax.broadcasted_iota(jnp.int32, (8, ta_col.shape[1]), 0)
    cand = jnp.concatenate(
        [ta[0] + tb_col, ta[1] + tb_col, ta[2] + tb_col[0:8], ta[3] + tb_col[0:8]]
        + [jnp.where(a_idx >= 4, ta_col[0:8] + tb[b], NEG_INF) for b in range(3)]
        + [ta_col[8:16] + tb[0]], axis=0)
    ctops, _ = _extract_top(cand, PEER_TOPK)
    cmax = ctops[0]
    z = jnp.exp(ctops[0] - cmax)
    for r in range(1, PEER_TOPK):
        z = z + jnp.exp(ctops[r] - cmax)
    tau_ref[...] = ctops[PEER_TOPK - 1]
    s1_ref[...] = jnp.where(ma, sa, NEG_INF)
    s2_ref[...] = jnp.where(mb, sb, NEG_INF)
    e1_ref[...] = jnp.where(ma, jnp.exp(sa - ta[0]), 0.0) / z
    e2_ref[...] = jnp.where(mb, jnp.exp(sb - tb[0]), 0.0)


def peer_route(qt, pkeys, tb):
    n = qt.shape[1]
    big = jax.ShapeDtypeStruct((PEER_HEADS, N_KEYS, n), jnp.float32)
    bspec = pl.BlockSpec((None, N_KEYS, tb), lambda i, h: (h, 0, i))
    return pl.pallas_call(
        _peer_route_kernel,
        grid=(n // tb, PEER_HEADS),
        in_specs=[pl.BlockSpec((2 * PEER_QHALF, tb), lambda i, h: (h, i)),
                  pl.BlockSpec((None, 2, N_KEYS, PEER_QHALF), lambda i, h: (h, 0, 0, 0))],
        out_specs=[bspec, bspec, bspec, bspec, pl.BlockSpec((None, 1, tb), lambda i, h: (h, 0, i))],
        out_shape=[big, big, big, big, jax.ShapeDtypeStruct((PEER_HEADS, 1, n), jnp.float32)],
        compiler_params=pltpu.CompilerParams(dimension_semantics=("parallel", "parallel")),
        name="peer_route",
    )(qt, pkeys)


def _peer_dense_kernel(xt_ref, u_ref, vt_ref, s1_ref, e1_ref, s2_ref, e2_ref, tau_ref, o_ref, *, sub):
    j = pl.program_id(1)

    @pl.when(j == 0)
    def _():
        o_ref[...] = jnp.zeros_like(o_ref)

    ht = jnp.dot(u_ref[...], xt_ref[...], preferred_element_type=jnp.float32)
    acts = []
    for a in range(sub):
        i1 = j * sub + a
        g = None
        for h in range(PEER_HEADS):
            s1row = s1_ref[h, pl.ds(i1, 1), :]
            e1row = e1_ref[h, pl.ds(i1, 1), :]
            c = s2_ref[h] + s1row
            t = jnp.where(c >= tau_ref[h], e2_ref[h], 0.0) * e1row
            g = t if g is None else g + t
        acts.append((_gelu_tanh(ht[a * N_KEYS:(a + 1) * N_KEYS, :]) * g).astype(bf16))
    act = jnp.concatenate(acts, axis=0) if sub > 1 else acts[0]
    o_ref[...] += jnp.dot(vt_ref[...], act, preferred_element_type=jnp.float32)


def peer_dense(xt, u_bf, vt_bf, s1, e1, s2, e2, tau, tb, sub):
    d, n = xt.shape
    e = u_bf.shape[0]
    te = sub * N_KEYS
    once = dict(pipeline_mode=pl.Buffered(1))
    rspec = pl.BlockSpec((PEER_HEADS, N_KEYS, tb), lambda i, j: (0, 0, i), **once)
    return pl.pallas_call(
        functools.partial(_peer_dense_kernel, sub=sub),
        grid=(n // tb, e // te),
        in_specs=[pl.BlockSpec((d, tb), lambda i, j: (0, i), **once),
                  pl.BlockSpec((te, d), lambda i, j: (j, 0)),
                  pl.BlockSpec((d, te), lambda i, j: (0, j)),
                  rspec, rspec, rspec, rspec,
                  pl.BlockSpec((PEER_HEADS, 1, tb), lambda i, j: (0, 0, i), **once)],
        out_specs=pl.BlockSpec((d, tb), lambda i, j: (0, i)),
        out_shape=jax.ShapeDtypeStruct((d, n), jnp.float32),
        compiler_params=pltpu.CompilerParams(dimension_semantics=("parallel", "arbitrary"),
                                             vmem_limit_bytes=BIG_VMEM_LIMIT_BYTES),
        name="peer_dense",
    )(xt, u_bf, vt_bf, s1, e1, s2, e2, tau)


def peer_ffn_t(xt, w_pq_t, pkeys, u_bf, vt_bf):
    n = xt.shape[1]
    qt = pmm(w_pq_t, xt)
    s1, e1, s2, e2, tau = peer_route(qt, pkeys, _pick(n, (256, 128)))
    return peer_dense(xt, u_bf, vt_bf, s1, e1, s2, e2, tau, _pick(n, (512, 256, 128)), PEER_SUB)


def _in_proj_weights(w_in):
    o = _IN_OFFS
    w_main = jnp.concatenate([w_in[:, o[0]:o[3]], w_in[:, o[5]:o[8]], w_in[:, o[9]:o[11]]], axis=1).astype(bf16)
    w_gate = jnp.concatenate([w_in[:, o[3]:o[5]], w_in[:, o[8]:o[9]],
                              jnp.zeros((w_in.shape[0], LANES - 2 * NH_M - 3 * H_N), w_in.dtype)], axis=1).astype(bf16)
    return w_main, w_gate


def kernel(x_prompt, x_sample, cache_kv, state_win_kv, state_conv, state_C, state_n, state_m, page_table,
           norm_mix, norm_ffn, norm_final, w_in, w_conv, b_conv, w_bd, b_gates, w_hnorm, w_skip,
           cmp_pe, cmp_w1, cmp_w2, w_br_m, w_br_n, w_out, w_pq, peer_keys, peer_u, peer_v):
    assert w_in.shape[0] == DEPTH == 1
    f32 = jnp.float32
    l = 0
    Bp, Lp_, _ = x_prompt.shape
    Bs, Ls, _ = x_sample.shape
    w_main, w_gate = _in_proj_weights(w_in[l])
    wbd_full = expand_blockdiag(w_bd[l])
    gate_bias = jnp.concatenate([b_gates[l, 0], b_gates[l, 1], jnp.zeros((LANES - 2 * NH_M,), f32)])[None]
    w_br_m_bf, w_br_n_bf, w_out_bf = w_br_m[l].astype(bf16), w_br_n[l].astype(bf16), w_out[l].astype(bf16)
    w_pq_t = w_pq[l].T.astype(bf16)
    u_bf = peer_u[l].astype(bf16)
    vt_bf = peer_v[l].T.astype(bf16)

    def mixers(x, conv_prev, C0, n0, m0, nsa_fn, ch, lpad):
        B, L, _ = x.shape
        n = B * L
        x2 = x.reshape(n, D_MODEL)
        xn = rmsnorm_rows(x2, norm_mix[l], bf16)
        z2 = pmm(xn, w_main)
        zg2 = pmm(xn, w_gate)
        z3 = z2.reshape(B, L, Z_COLS)
        zg3 = zg2.reshape(B, L, LANES)
        xm = z3[:, :, Z_XM:Z_XM + D_M]
        new_conv = jnp.concatenate([conv_prev, xm], axis=1)[:, L:]
        prev8 = jnp.pad(conv_prev, ((0, 0), (PREV_ROWS - (CONV_W - 1), 0), (0, 0)))
        if lpad == L:
            zm3, zgm3, cols = z3, zg3, (Z_XM // DH_M, Z_VM // DH_M, Z_OM // DH_M)
        else:
            zm3 = jnp.pad(z3[:, :, :Z_Q], ((0, 0), (0, lpad - L), (0, 0)))
            zgm3 = jnp.pad(zg3, ((0, 0), (0, lpad - L), (0, 0)))
            cols = (Z_XM // DH_M, Z_VM // DH_M, Z_OM // DH_M)
        y_m, C1, n1, m1 = mlstm_branch(zm3, zgm3, cols, prev8, w_conv[l], b_conv[l], wbd_full, gate_bias,
                                       C0, n0[:, :, None, :], jnp.broadcast_to(m0[:, :, None, None], (B, NH_M, 1, LANES)),
                                       w_hnorm[l], w_skip[l], L, ch)
        y_m = y_m[:, :L].reshape(n, D_M)
        kv_new = z3[:, :, Z_KV:Z_KV + N_KV_SETS * KVW].reshape(B, L, N_KV_SETS, G_N, DH_N)
        win_new = z3[:, :, Z_WIN:Z_WIN + 2 * KVW].reshape(B, L, 2, G_N, DH_N)
        y_n, new_win = nsa_fn(z3, zg3, kv_new, win_new)
        merged = merge_branches(y_m, y_n.reshape(n, D_N).astype(bf16), w_br_m_bf, w_br_n_bf, z2)
        h2 = out_proj_residual(merged, w_out_bf, x2)
        return h2, kv_new, new_win, new_conv, C1, n1[:, :, 0], m1[:, :, 0, 0]

    def nsa_p(z3, zg3, kv_new, win_new):
        kcv = compress_prompt(z3, Z_KV // DH_N, cmp_pe[l], cmp_w1[l], cmp_w2[l])
        return nsa_prompt(z3, zg3, kcv), win_new[:, -min(WINDOW, z3.shape[1]):]

    def nsa_s(z3, zg3, kv_new, win_new):
        L = z3.shape[1]
        padr = lambda t, rows: jnp.pad(t, ((0, 0), (0, rows - t.shape[1]), (0, 0)))
        page = cache_kv.shape[2]
        cache3 = cache_kv.reshape(DEPTH * cache_kv.shape[1] * page, N_KV_SETS * G_N, DH_N)
        wbuf = state_win_kv.shape[2]
        win3 = state_win_kv.reshape(Bs * wbuf, 2 * G_N, DH_N)
        kcv = paged_compress(page_table, cache3, page, padr(z3[:, :, Z_KV:Z_KV + 2 * KVW], NEW_ROWS),
                             cmp_pe[l], cmp_w1[l].astype(bf16).reshape(2, CMP_LEN // 2, 2 * DH_N, CMP_HID), cmp_w2[l])
        y8 = paged_attend(page_table, cache3, page, padr(z3[:, :, Z_Q:Z_Q + D_N], QROWS), kcv,
                          padr(z3[:, :, Z_KV + 2 * KVW:Z_KV + 4 * KVW], NEW_ROWS),
                          win3, padr(z3[:, :, Z_WIN:Z_WIN + 2 * KVW], NEW_ROWS), padr(zg3, QROWS), L)
        new_win = jnp.concatenate([state_win_kv.reshape(Bs, wbuf, 2, G_N, DH_N)[:, L:], win_new], axis=1)
        return y8[:, :L], new_win

    hp, kvp, winp, convp, Cp, n_p, m_p = mixers(
        x_prompt, jnp.zeros((Bp, CONV_W - 1, D_M), f32), jnp.zeros((Bp, NH_M, DH_M, DH_M), f32),
        jnp.zeros((Bp, NH_M, DH_M), f32), jnp.zeros((Bp, NH_M), f32), nsa_p, 256, Lp_)
    hs, kvs, wins, convs, Cs, n_s, m_s = mixers(
        x_sample, state_conv.reshape(state_conv.shape[1:]), state_C.reshape(state_C.shape[1:]),
        state_n.reshape(state_n.shape[1:]), state_m.reshape(state_m.shape[1:]), nsa_s, 16, 16)

    def ffn_and_norm(h2):
        xt = rmsnorm_rows_t(h2, norm_ffn[l], bf16)
        peer_t = peer_ffn_t(xt, w_pq_t, peer_keys[l], u_bf, vt_bf)
        return final_norm(h2, peer_t, norm_final)

    y_prompt = ffn_and_norm(hp).reshape(x_prompt.shape)
    y_sample = ffn_and_norm(hs).reshape(x_sample.shape)
    st = lambda t: t[None]
    return (y_prompt, y_sample, st(kvp), st(kvs), st(winp), st(wins), st(convp), st(convs),
            st(Cp), st(Cs), st(n_p), st(n_s), st(m_p), st(m_s))
```

```python
import functools

import jax
import jax.numpy as jnp
import numpy as np
from jax import lax
from jax.experimental import pallas as pl
from jax.experimental.pallas import tpu as pltpu

D_MODEL = 4096
DEPTH = 1
D_M = D_MODEL // 2
DH_M = 256
NH_M = D_M // DH_M
CONV_W = 4
QKV_BLOCK = 4
D_N = D_MODEL // 2
DH_N = 128
H_N = D_N // DH_N
G_N = 4
J_N = H_N // G_N
KVW = G_N * DH_N
N_KV_SETS = 4
CMP_STRIDE = 16
CMP_LEN = 2 * CMP_STRIDE
CMP_HID = 2 * DH_N
SEL_LEN = 64
N_SEL = 16
WINDOW = 512
SCALE_N = DH_N ** -0.5
PEER_HEADS = 8
N_KEYS = 128
PEER_TOPK = 16
PEER_QDIM = 256
PEER_QHALF = PEER_QDIM // 2
IN_SIZES = (D_M, D_M, D_M, NH_M, NH_M, D_N, N_KV_SETS * KVW, 2 * KVW, 3 * H_N, D_MODEL, D_MODEL)
EPS = 1e-6
NEG = -1e30
BIG = 1e9
NEG_INF = float('-inf')

LANES = 128
VMEM_LIMIT_BYTES = 48 * 1024 * 1024
BIG_VMEM_LIMIT_BYTES = 56 * 1024 * 1024
PEER_SUB = 4
PREV_ROWS = 8
PAGES_PER_STEP = 16
NEW_ROWS = 16
QROWS = 8
SEL_CHUNK = 512
bf16 = jnp.bfloat16

_IN_OFFS = np.concatenate([[0], np.cumsum(IN_SIZES)]).tolist()
Z_XM, Z_VM, Z_OM = 0, D_M, 2 * D_M
Z_Q = 3 * D_M
Z_KV = Z_Q + D_N
Z_WIN = Z_KV + N_KV_SETS * KVW
Z_GM = Z_WIN + 2 * KVW
Z_GN = Z_GM + D_MODEL
Z_COLS = Z_GN + D_MODEL
ZG_IG, ZG_FG, ZG_GN = 0, NH_M, 2 * NH_M


def _pick(n, cands):
    for c in cands:
        if n % c == 0:
            return c
    return n


def _gelu_tanh(x):
    return 0.5 * x * (1.0 + jnp.tanh(np.float32(np.sqrt(2.0 / np.pi)) * (x + np.float32(0.044715) * (x * x * x))))


def _split3(x):
    hi = x.astype(bf16)
    r = x - hi.astype(jnp.float32)
    mid = r.astype(bf16)
    lo = (r - mid.astype(jnp.float32)).astype(bf16)
    return hi, mid, lo


def _rmsnorm_kernel(x_ref, g_ref, o_ref):
    x = x_ref[...]
    o_ref[...] = (x * lax.rsqrt(jnp.mean(x * x, axis=-1, keepdims=True) + EPS) * g_ref[...]).astype(o_ref.dtype)


def rmsnorm_rows(x2, g, out_dtype):
    n, d = x2.shape
    tr = _pick(n, (256, 128))
    return pl.pallas_call(
        _rmsnorm_kernel,
        grid=(n // tr,),
        in_specs=[pl.BlockSpec((tr, d), lambda i: (i, 0)), pl.BlockSpec((1, d), lambda i: (0, 0))],
        out_specs=pl.BlockSpec((tr, d), lambda i: (i, 0)),
        out_shape=jax.ShapeDtypeStruct((n, d), out_dtype),
        compiler_params=pltpu.CompilerParams(dimension_semantics=("parallel",), vmem_limit_bytes=VMEM_LIMIT_BYTES),
        name="rmsnorm",
    )(x2, g.reshape(1, d))


def _rmsnorm_t_kernel(x_ref, g_ref, o_ref):
    x = x_ref[...]
    o_ref[...] = (x * lax.rsqrt(jnp.mean(x * x, axis=-1, keepdims=True) + EPS) * g_ref[...]).T.astype(o_ref.dtype)


def rmsnorm_rows_t(x2, g, out_dtype):
    n, d = x2.shape
    tr = _pick(n, (256, 128))
    return pl.pallas_call(
        _rmsnorm_t_kernel,
        grid=(n // tr,),
        in_specs=[pl.BlockSpec((tr, d), lambda i: (i, 0)), pl.BlockSpec((1, d), lambda i: (0, 0))],
        out_specs=pl.BlockSpec((d, tr), lambda i: (0, i)),
        out_shape=jax.ShapeDtypeStruct((d, n), out_dtype),
        compiler_params=pltpu.CompilerParams(dimension_semantics=("parallel",), vmem_limit_bytes=VMEM_LIMIT_BYTES),
        name="rmsnorm_t",
    )(x2, g.reshape(1, d))


def _mm_kernel(a_ref, b_ref, o_ref):
    o_ref[...] = jnp.dot(a_ref[...], b_ref[...], preferred_element_type=jnp.float32).astype(o_ref.dtype)


def pmm(a, b, out_dtype=jnp.float32):
    m, kd = a.shape
    n = b.shape[1]
    tm = _pick(m, (1024, 512, 256, 128))
    tn = _pick(n, (1024, 512, 256, 128))
    return pl.pallas_call(
        _mm_kernel,
        grid=(m // tm, n // tn),
        in_specs=[pl.BlockSpec((tm, kd), lambda i, j: (i, 0)), pl.BlockSpec((kd, tn), lambda i, j: (0, j))],
        out_specs=pl.BlockSpec((tm, tn), lambda i, j: (i, j)),
        out_shape=jax.ShapeDtypeStruct((m, n), out_dtype),
        compiler_params=pltpu.CompilerParams(dimension_semantics=("parallel", "parallel"),
                                             vmem_limit_bytes=BIG_VMEM_LIMIT_BYTES),
        name="proj",
    )(a, b)


def _merge_kernel(ym_ref, yn_ref, wm_ref, wn_ref, gm_ref, gn_ref, o_ref):
    pm = jnp.dot(ym_ref[...], wm_ref[...], preferred_element_type=jnp.float32)
    pn = jnp.dot(yn_ref[...], wn_ref[...], preferred_element_type=jnp.float32)
    o_ref[...] = (jax.nn.sigmoid(gm_ref[...]) * pm + jax.nn.sigmoid(gn_ref[...]) * pn).astype(o_ref.dtype)


def merge_branches(ym, yn, wm, wn, z2):
    m = ym.shape[0]
    n = wm.shape[1]
    tm = _pick(m, (1024, 512, 256, 128))
    tn = 512
    return pl.pallas_call(
        _merge_kernel,
        grid=(m // tm, n // tn),
        in_specs=[pl.BlockSpec((tm, D_M), lambda i, j: (i, 0)), pl.BlockSpec((tm, D_N), lambda i, j: (i, 0)),
                  pl.BlockSpec((D_M, tn), lambda i, j: (0, j)), pl.BlockSpec((D_N, tn), lambda i, j: (0, j)),
                  pl.BlockSpec((tm, tn), lambda i, j: (i, Z_GM // tn + j)),
                  pl.BlockSpec((tm, tn), lambda i, j: (i, Z_GN // tn + j))],
        out_specs=pl.BlockSpec((tm, tn), lambda i, j: (i, j)),
        out_shape=jax.ShapeDtypeStruct((m, n), bf16),
        compiler_params=pltpu.CompilerParams(dimension_semantics=("parallel", "parallel"),
                                             vmem_limit_bytes=VMEM_LIMIT_BYTES),
        name="merge_branches",
    )(ym, yn, wm, wn, z2, z2)


def _mm_res_kernel(a_ref, b_ref, r_ref, o_ref):
    o_ref[...] = r_ref[...] + jnp.dot(a_ref[...], b_ref[...], preferred_element_type=jnp.float32)


def out_proj_residual(a, b, r):
    m, kd = a.shape
    n = b.shape[1]
    tm = _pick(m, (1024, 512, 256, 128))
    tn = 512
    return pl.pallas_call(
        _mm_res_kernel,
        grid=(m // tm, n // tn),
        in_specs=[pl.BlockSpec((tm, kd), lambda i, j: (i, 0)), pl.BlockSpec((kd, tn), lambda i, j: (0, j)),
                  pl.BlockSpec((tm, tn), lambda i, j: (i, j))],
        out_specs=pl.BlockSpec((tm, tn), lambda i, j: (i, j)),
        out_shape=jax.ShapeDtypeStruct((m, n), jnp.float32),
        compiler_params=pltpu.CompilerParams(dimension_semantics=("parallel", "parallel"),
                                             vmem_limit_bytes=VMEM_LIMIT_BYTES),
        name="out_proj",
    )(a, b, r)


def _final_kernel(h_ref, pt_ref, g_ref, o_ref):
    x = h_ref[...] + pt_ref[...].T
    o_ref[...] = x * lax.rsqrt(jnp.mean(x * x, axis=-1, keepdims=True) + EPS) * g_ref[...]


def final_norm(h2, peer_t, g):
    n, d = h2.shape
    tr = _pick(n, (256, 128))
    return pl.pallas_call(
        _final_kernel,
        grid=(n // tr,),
        in_specs=[pl.BlockSpec((tr, d), lambda i: (i, 0)), pl.BlockSpec((d, tr), lambda i: (0, i)),
                  pl.BlockSpec((1, d), lambda i: (0, 0))],
        out_specs=pl.BlockSpec((tr, d), lambda i: (i, 0)),
        out_shape=jax.ShapeDtypeStruct((n, d), jnp.float32),
        compiler_params=pltpu.CompilerParams(dimension_semantics=("parallel",), vmem_limit_bytes=VMEM_LIMIT_BYTES),
        name="final_norm",
    )(h2, peer_t, g.reshape(1, d))


def _blockdiag_apply(x, w_ref, which):
    parts = [jnp.dot(x[:, hb * LANES:(hb + 1) * LANES].astype(bf16), w_ref[which, hb].astype(bf16),
                     preferred_element_type=jnp.float32) for hb in range(DH_M // LANES)]
    return jnp.concatenate(parts, axis=1)


def _mlstm_kernel(xm_ref, vm_ref, om_ref, zg_ref, prev_ref, wc_ref, bc_ref, wbd_ref, gb_ref, c0_ref, n0_ref, m0_ref,
                  hn_ref, sk_ref, y_ref, c1_ref, n1_ref, m1_ref, *, ch, l_true):
    h = pl.program_id(1)
    lp = xm_ref.shape[0]
    nchunk = lp // ch
    f32 = jnp.float32
    tri = lax.broadcasted_iota(jnp.int32, (ch, ch), 0) >= lax.broadcasted_iota(jnp.int32, (ch, ch), 1)
    tri_bf = tri.astype(bf16)
    lane_g = lax.broadcasted_iota(jnp.int32, (ch, LANES), 1)
    sub_g = lax.broadcasted_iota(jnp.int32, (LANES, ch), 0)

    def chunk(ci, carry):
        c_st, n_st, m_st = carry
        r0 = pl.multiple_of(ci * ch, ch)
        xm = xm_ref[pl.ds(r0, ch), :]
        before = xm_ref[pl.ds(pl.multiple_of(jnp.maximum(r0 - PREV_ROWS, 0), PREV_ROWS), PREV_ROWS), :]
        before = jnp.where(ci == 0, prev_ref[...], before)
        xe = jnp.concatenate([before, xm], axis=0)
        lo = PREV_ROWS - (CONV_W - 1)
        conv = bc_ref[...] + sum(xe[lo + w:lo + w + ch, :] * wc_ref[w:w + 1, :] for w in range(CONV_W))
        c = conv * jax.nn.sigmoid(conv)
        q = _blockdiag_apply(c, wbd_ref, 0)
        k = _blockdiag_apply(c, wbd_ref, 1) * (DH_M ** -0.5)
        v = _blockdiag_apply(vm_ref[pl.ds(r0, ch), :], wbd_ref, 2)
        zg = zg_ref[pl.ds(r0, ch), :] + gb_ref[...]
        live = (r0 + lax.broadcasted_iota(jnp.int32, (ch, LANES), 0)) < l_true
        a_all = jnp.where(live, zg, NEG_INF)
        lf_all = jnp.where(live, jnp.minimum(zg, 0.0) - jnp.log1p(jnp.exp(-jnp.abs(zg))), 0.0)
        b_all = jnp.zeros((ch, LANES), f32)
        for part in _split3(lf_all):
            b_all = b_all + jnp.dot(tri_bf, part, preferred_element_type=f32)
        a_col = jnp.sum(jnp.where(lane_g == ZG_IG + h, a_all, 0.0), axis=1, keepdims=True)
        b_col = jnp.sum(jnp.where(lane_g == ZG_FG + h, b_all, 0.0), axis=1, keepdims=True)
        a_row = jnp.sum(jnp.where(sub_g == ZG_IG + h, a_all.T, 0.0), axis=0, keepdims=True)
        b_row = jnp.sum(jnp.where(sub_g == ZG_FG + h, b_all.T, 0.0), axis=0, keepdims=True)
        dmat = jnp.where(tri, b_col - b_row + a_row, NEG_INF)
        inter = b_col + m_st
        m = jnp.maximum(inter, jnp.max(dmat, axis=1, keepdims=True))
        w_intra = jnp.exp(dmat - m)
        w_inter = jnp.exp(inter - m)
        qb = q.astype(bf16)
        kb = k.astype(bf16)
        vb = v.astype(bf16)
        qk = lax.dot_general(qb, kb, (((1,), (1,)), ((), ())), preferred_element_type=f32) * w_intra
        num = w_inter * jnp.dot(qb, c_st.astype(bf16), preferred_element_type=f32) \
            + jnp.dot(qk.astype(bf16), vb, preferred_element_type=f32)
        den = w_inter * jnp.sum(q * n_st, axis=1, keepdims=True) + jnp.sum(qk, axis=1, keepdims=True)
        hh = num / jnp.maximum(jnp.abs(den), jnp.exp(-m))
        m_end = m[ch - 1:ch, :]
        b_end = b_col[ch - 1:ch, :]
        w_end = jnp.exp(b_end - b_col + a_col - m_end)
        decay = jnp.exp(b_end + m_st - m_end)
        kw = k * w_end
        c_new = decay * c_st + lax.dot_general(kw.astype(bf16), vb, (((0,), (0,)), ((), ())),
                                               preferred_element_type=f32)
        n_new = decay * n_st + jnp.sum(kw, axis=0, keepdims=True)
        hg = hh * jax.nn.sigmoid(om_ref[pl.ds(r0, ch), :])
        mu = jnp.mean(hg, axis=1, keepdims=True)
        var = jnp.mean(jnp.square(hg - mu), axis=1, keepdims=True)
        y = (hg - mu) * lax.rsqrt(var + EPS) * hn_ref[...] + sk_ref[...] * c
        y_ref[pl.ds(r0, ch), :] = y.astype(y_ref.dtype)
        return c_new, n_new, m_end

    c_fin, n_fin, m_fin = lax.fori_loop(0, nchunk, chunk, (c0_ref[...], n0_ref[...], m0_ref[:, :1]))
    c1_ref[...] = c_fin
    n1_ref[...] = n_fin
    m1_ref[...] = jnp.broadcast_to(m_fin, m1_ref.shape)


def expand_blockdiag(w_bd):
    per = LANES // QKV_BLOCK
    w = w_bd.reshape(3, D_M // LANES, per, QKV_BLOCK, QKV_BLOCK)
    full = jnp.einsum('tbnio,nm->tbnimo', w, jnp.eye(per, dtype=w_bd.dtype))
    return full.reshape(3, D_M // LANES, LANES, LANES)


def mlstm_branch(z3, zg3, col_blocks, prev8, w_conv, b_conv, wbd_full, gate_bias, c0, n0, m0, w_hnorm, w_skip,
                 l_true, ch):
    b, lp, _ = z3.shape
    cx, cv, co = col_blocks
    kern = functools.partial(_mlstm_kernel, ch=ch, l_true=l_true)
    col = lambda c0_: pl.BlockSpec((None, lp, DH_M), lambda bi, h: (bi, 0, c0_ + h))
    vec = pl.BlockSpec((1, DH_M), lambda bi, h: (0, h))
    st = lambda r, c: pl.BlockSpec((None, None, r, c), lambda bi, h: (bi, h, 0, 0))
    return pl.pallas_call(
        kern,
        grid=(b, NH_M),
        in_specs=[col(cx), col(cv), col(co),
                  pl.BlockSpec((None, lp, LANES), lambda bi, h: (bi, 0, 0)),
                  pl.BlockSpec((None, PREV_ROWS, DH_M), lambda bi, h: (bi, 0, h)),
                  pl.BlockSpec((CONV_W, DH_M), lambda bi, h: (0, h)),
                  vec,
                  pl.BlockSpec((3, DH_M // LANES, LANES, LANES), lambda bi, h: (0, h, 0, 0)),
                  pl.BlockSpec((1, LANES), lambda bi, h: (0, 0)),
                  st(DH_M, DH_M), st(1, DH_M), st(1, LANES), vec, vec],
        out_specs=[pl.BlockSpec((None, lp, DH_M), lambda bi, h: (bi, 0, h)),
                   st(DH_M, DH_M), st(1, DH_M), st(1, LANES)],
        out_shape=[jax.ShapeDtypeStruct((b, lp, D_M), bf16),
                   jax.ShapeDtypeStruct((b, NH_M, DH_M, DH_M), jnp.float32),
                   jax.ShapeDtypeStruct((b, NH_M, 1, DH_M), jnp.float32),
                   jax.ShapeDtypeStruct((b, NH_M, 1, LANES), jnp.float32)],
        compiler_params=pltpu.CompilerParams(dimension_semantics=("parallel", "parallel"),
                                             vmem_limit_bytes=VMEM_LIMIT_BYTES),
        name="mlstm_branch",
    )(z3, z3, z3, zg3, prev8, w_conv, b_conv.reshape(1, D_M), wbd_full, gate_bias, c0, n0, m0,
      w_hnorm.reshape(1, D_M), w_skip.reshape(1, D_M))


def _compress_kernel(x_ref, pe_ref, w1_ref, w2_ref, o_ref):
    nchunk = x_ref.shape[0] // CMP_STRIDE
    first = jnp.zeros((nchunk, CMP_HID), jnp.float32)
    second = jnp.zeros((nchunk, CMP_HID), jnp.float32)
    for l in range(CMP_STRIDE):
        xl = x_ref[pl.ds(l, nchunk, stride=CMP_STRIDE), :]
        first += jnp.dot((xl + pe_ref[l:l + 1, :]).astype(bf16), w1_ref[l].astype(bf16),
                         preferred_element_type=jnp.float32)
        second += jnp.dot((xl + pe_ref[CMP_STRIDE + l:CMP_STRIDE + l + 1, :]).astype(bf16),
                          w1_ref[CMP_STRIDE + l].astype(bf16), preferred_element_type=jnp.float32)
    nxt = jnp.concatenate([second[1:], jnp.zeros((1, CMP_HID), jnp.float32)], axis=0)
    row = lax.broadcasted_iota(jnp.int32, (nchunk, CMP_HID), 0)
    hid = jnp.where(row < nchunk - 1, _gelu_tanh(first + nxt), 0.0)
    o_ref[...] = jnp.dot(hid.astype(bf16), w2_ref[...].astype(bf16), preferred_element_type=jnp.float32)


def compress_prompt(z3, col0, cmp_pe, cmp_w1, cmp_w2):
    b, l, _ = z3.shape
    nchunk = l // CMP_STRIDE
    return pl.pallas_call(
        _compress_kernel,
        grid=(b, 2, G_N),
        in_specs=[pl.BlockSpec((None, l, DH_N), lambda bi, s, g: (bi, 0, col0 + s * G_N + g)),
                  pl.BlockSpec((None, CMP_LEN, DH_N), lambda bi, s, g: (s, 0, 0)),
                  pl.BlockSpec((None, CMP_LEN, DH_N, CMP_HID), lambda bi, s, g: (s, 0, 0, 0)),
                  pl.BlockSpec((None, CMP_HID, DH_N), lambda bi, s, g: (s, 0, 0))],
        out_specs=pl.BlockSpec((None, None, None, nchunk, DH_N), lambda bi, s, g: (bi, s, g, 0, 0)),
        out_shape=jax.ShapeDtypeStruct((b, 2, G_N, nchunk, DH_N), jnp.float32),
        compiler_params=pltpu.CompilerParams(dimension_semantics=("parallel", "parallel", "parallel"),
                                             vmem_limit_bytes=VMEM_LIMIT_BYTES),
        name="nsa_compress",
    )(z3, cmp_pe, cmp_w1, cmp_w2)


def _softmax_pv(s, keep, v):
    s = jnp.where(keep, s, NEG)
    m = jnp.max(s, axis=-1, keepdims=True)
    e = jnp.exp(s - m)
    den = jnp.sum(e, axis=-1, keepdims=True)
    return jnp.dot(e.astype(bf16), v, preferred_element_type=jnp.float32) / den


def _nsa_prompt_kernel(q_ref, kc_ref, vc_ref, ks_ref, vs_ref, kw_ref, vw_ref, gn_ref, o_ref, *, tq):
    g = pl.program_id(1)
    qi = pl.program_id(2)
    f32 = jnp.float32
    t = ks_ref.shape[0]
    n_c = kc_ref.shape[0]
    n_s = t // SEL_LEN
    nq = J_N * tq
    q4 = jnp.concatenate([q_ref[:, j * DH_N:(j + 1) * DH_N] for j in range(J_N)], axis=0).astype(bf16)
    nt = (((1,), (1,)), ((), ()))

    def qpos_of(shape):
        return qi * tq + lax.broadcasted_iota(jnp.int32, shape, 0) % tq

    cidx = lax.broadcasted_iota(jnp.int32, (nq, n_c), 1)
    cmask = cidx * CMP_STRIDE + CMP_LEN <= qpos_of((nq, n_c)) + 1
    s = lax.dot_general(q4, kc_ref[...].astype(bf16), nt, preferred_element_type=f32) * SCALE_N
    s = jnp.where(cmask, s, NEG)
    m = jnp.max(s, axis=-1, keepdims=True)
    e = jnp.where(cmask, jnp.exp(s - m), 0.0)
    den = jnp.sum(e, axis=-1, keepdims=True)
    p = e / jnp.where(den > 0.0, den, 1.0)
    o_cmp = jnp.dot(p.astype(bf16), vc_ref[...].astype(bf16), preferred_element_type=f32)
    p_sum = p[0:tq]
    for j in range(1, J_N):
        p_sum = p_sum + p[j * tq:(j + 1) * tq]
    ci = lax.broadcasted_iota(jnp.int32, (n_c, LANES), 0)
    si = lax.broadcasted_iota(jnp.int32, (n_c, LANES), 1)
    cover = jnp.logical_and(ci * CMP_STRIDE < si * SEL_LEN + SEL_LEN,
                            ci * CMP_STRIDE + CMP_LEN > si * SEL_LEN).astype(bf16)
    imp = jnp.zeros((tq, LANES), f32)
    for part in _split3(p_sum):
        imp = imp + jnp.dot(part, cover, preferred_element_type=f32)
    blk = lax.broadcasted_iota(jnp.int32, (tq, LANES), 1)
    cur = (qi * tq + lax.broadcasted_iota(jnp.int32, (tq, LANES), 0)) // SEL_LEN
    valid = blk <= cur
    forced = jnp.logical_or(blk == 0, blk >= cur - 1)
    score = jnp.where(valid, imp + jnp.where(forced, BIG, 0.0), -BIG)
    score = jnp.where(blk < n_s, score, NEG_INF)
    rank = jnp.zeros((tq, LANES), jnp.int32)
    for s2 in range(n_s):
        col = score[:, s2:s2 + 1]
        ahead = jnp.logical_or(col > score, jnp.logical_and(col == score, blk > s2))
        rank = rank + ahead.astype(jnp.int32)
    sel = (rank < N_SEL).astype(bf16)

    ck = SEL_CHUNK
    s_row = lax.broadcasted_iota(jnp.int32, (LANES, ck), 0)
    k_lane = lax.broadcasted_iota(jnp.int32, (LANES, ck), 1)
    kpos1 = lax.broadcasted_iota(jnp.int32, (tq, ck), 1)
    qpos1 = qi * tq + lax.broadcasted_iota(jnp.int32, (tq, ck), 0)

    def kchunk(kb, carry):
        m_o, l_o, acc = carry
        k0 = pl.multiple_of(kb * ck, ck)
        expand = ((k0 + k_lane) // SEL_LEN == s_row).astype(bf16)
        picked = jnp.dot(sel, expand, preferred_element_type=f32)
        keep1 = jnp.logical_and(picked > 0.5, k0 + kpos1 <= qpos1)
        keep = jnp.concatenate([keep1] * J_N, axis=0)
        kk = ks_ref[pl.ds(k0, ck), :].astype(bf16)
        vv = vs_ref[pl.ds(k0, ck), :].astype(bf16)
        sc = lax.dot_general(q4, kk, nt, preferred_element_type=f32) * SCALE_N
        sc = jnp.where(keep, sc, NEG)
        m_n = jnp.maximum(m_o, jnp.max(sc, axis=-1, keepdims=True))
        alpha = jnp.exp(m_o - m_n)
        ee = jnp.where(keep, jnp.exp(sc - m_n), 0.0)
        l_n = alpha * l_o + jnp.sum(ee, axis=-1, keepdims=True)
        return m_n, l_n, alpha * acc + jnp.dot(ee.astype(bf16), vv, preferred_element_type=f32)

    nkc = (qi * tq + tq + ck - 1) // ck
    _, l_f, acc_f = lax.fori_loop(0, nkc, kchunk, (jnp.full((nq, 1), NEG, f32), jnp.zeros((nq, 1), f32),
                                                   jnp.zeros((nq, DH_N), f32)))
    o_sel = acc_f / l_f

    span = WINDOW + tq
    w0 = pl.multiple_of(jnp.clip(qi * tq - WINDOW, 0, t - span), tq)
    kw = kw_ref[pl.ds(w0, span), :].astype(bf16)
    vw = vw_ref[pl.ds(w0, span), :].astype(bf16)
    qpos_w = qpos_of((nq, span))
    kpos_w = w0 + lax.broadcasted_iota(jnp.int32, (nq, span), 1)
    keep_win = jnp.logical_and(kpos_w <= qpos_w, kpos_w > qpos_w - WINDOW)
    sw = lax.dot_general(q4, kw, nt, preferred_element_type=f32) * SCALE_N
    o_win = _softmax_pv(sw, keep_win, vw)

    gates = jax.nn.sigmoid(gn_ref[...])
    gl = lax.broadcasted_iota(jnp.int32, gates.shape, 1)
    outs = []
    for j in range(J_N):
        acc = jnp.zeros((tq, DH_N), f32)
        for which, o in enumerate((o_cmp, o_sel, o_win)):
            col = jnp.sum(jnp.where(gl == ZG_GN + (g * J_N + j) * 3 + which, gates, 0.0), axis=-1, keepdims=True)
            acc = acc + col * o[j * tq:(j + 1) * tq]
        outs.append(acc)
    o_ref[...] = jnp.concatenate(outs, axis=1).astype(o_ref.dtype)


def nsa_prompt(z3, zg3, kcv, tq=256):
    b, l, _ = z3.shape
    assert l >= WINDOW + tq and l % tq == 0 and l % SEL_CHUNK == 0
    nq = l // tq
    n_c = kcv.shape[3]
    full = lambda c0: pl.BlockSpec((None, l, DH_N), lambda bi, g, qi: (bi, 0, c0 // DH_N + g))
    return pl.pallas_call(
        functools.partial(_nsa_prompt_kernel, tq=tq),
        grid=(b, G_N, nq),
        in_specs=[pl.BlockSpec((None, tq, J_N * DH_N), lambda bi, g, qi: (bi, qi, Z_Q // (J_N * DH_N) + g)),
                  pl.BlockSpec((None, None, None, n_c, DH_N), lambda bi, g, qi: (bi, 0, g, 0, 0)),
                  pl.BlockSpec((None, None, None, n_c, DH_N), lambda bi, g, qi: (bi, 1, g, 0, 0)),
                  full(Z_KV + 2 * KVW), full(Z_KV + 3 * KVW), full(Z_WIN), full(Z_WIN + KVW),
                  pl.BlockSpec((None, tq, LANES), lambda bi, g, qi: (bi, qi, 0))],
        out_specs=pl.BlockSpec((None, tq, J_N * DH_N), lambda bi, g, qi: (bi, qi, g)),
        out_shape=jax.ShapeDtypeStruct((b, l, D_N), bf16),
        compiler_params=pltpu.CompilerParams(dimension_semantics=("parallel", "parallel", "arbitrary"),
                                             vmem_limit_bytes=VMEM_LIMIT_BYTES),
        name="nsa_prompt",
    )(z3, kcv, kcv, z3, z3, z3, z3, zg3)


def _paged_compress_kernel(pt_ref, *refs, page, nstep):
    pages = refs[:PAGES_PER_STEP]
    new_ref, pe_ref, w1_ref, w2_ref, o_ref, carry_ref, bias_ref, rows_ref = refs[PAGES_PER_STEP:]
    p = pl.program_id(1)
    nsg = 2 * G_N
    cpp = page // CMP_STRIDE
    nch = PAGES_PER_STEP * cpp
    npair = CMP_STRIDE // 2
    row = lax.broadcasted_iota(jnp.int32, (nch, CMP_HID), 0)
    flat = [pg.reshape(page * nsg, DH_N) for pg in pages]

    @pl.when(p == 0)
    def _():
        for s in range(2):
            for half in range(2):
                acc = jnp.zeros((QROWS, CMP_HID), jnp.float32)
                for k in range(npair):
                    l0 = half * CMP_STRIDE + 2 * k
                    lhs = jnp.concatenate([jnp.broadcast_to(pe_ref[s, l0 + i:l0 + i + 1, :], (QROWS, DH_N))
                                           for i in range(2)], axis=1).astype(bf16)
                    acc += jnp.dot(lhs, w1_ref[s, half * npair + k], preferred_element_type=jnp.float32)
                bias_ref[half * 2 + s] = acc

    @pl.when(p < nstep)
    def _():
        for r in range(PAGES_PER_STEP):
            for sg in range(nsg):
                rows_ref[sg, r * page:(r + 1) * page, :] = flat[r][pl.ds(sg, page, stride=nsg), :]
        for s in range(2):
            first = jnp.zeros((G_N * nch, CMP_HID), jnp.float32)
            second = jnp.zeros((G_N * nch, CMP_HID), jnp.float32)
            for k in range(npair):
                lhs = jnp.concatenate(
                    [jnp.concatenate([rows_ref[s * G_N + g, pl.ds(l, nch, stride=CMP_STRIDE), :] for g in range(G_N)],
                                     axis=0).astype(bf16) for l in (2 * k, 2 * k + 1)], axis=1)
                first += jnp.dot(lhs, w1_ref[s, k], preferred_element_type=jnp.float32)
                second += jnp.dot(lhs, w1_ref[s, npair + k], preferred_element_type=jnp.float32)
            first = first + bias_ref[s, 0:1, :]
            second = second + bias_ref[2 + s, 0:1, :]
            for g in range(G_N):
                f_g = first[g * nch:(g + 1) * nch]
                s_g = second[g * nch:(g + 1) * nch]
                prev = carry_ref[s * G_N + g, 0:1, :]
                shifted = jnp.where(row == 0, prev, pltpu.roll(f_g, 1, axis=0))
                carry_ref[s * G_N + g, 0:1, :] = f_g[nch - 1:nch, :]
                hid = _gelu_tanh(shifted + s_g)
                hid = jnp.where(jnp.logical_and(row == 0, p == 0), 0.0, hid)
                o_ref[s, g] = jnp.dot(hid.astype(bf16), w2_ref[s].astype(bf16), preferred_element_type=jnp.float32)

    @pl.when(p == nstep)
    def _():
        for s in range(2):
            for g in range(G_N):
                c0 = (s * G_N + g) * DH_N
                sec = jnp.zeros((QROWS, CMP_HID), jnp.float32)
                for l in range(CMP_STRIDE):
                    x = jnp.broadcast_to(new_ref[l:l + 1, c0:c0 + DH_N] + pe_ref[s, CMP_STRIDE + l:CMP_STRIDE + l + 1, :],
                                         (QROWS, DH_N))
                    w_l = w1_ref[s, (CMP_STRIDE + l) // 2, (l % 2) * DH_N:(l % 2 + 1) * DH_N, :]
                    sec += jnp.dot(x.astype(bf16), w_l, preferred_element_type=jnp.float32)
                hid = _gelu_tanh(carry_ref[s * G_N + g, 0:1, :] + sec)
                blk = jnp.dot(hid.astype(bf16), w2_ref[s].astype(bf16), preferred_element_type=jnp.float32)
                rows = lax.broadcasted_iota(jnp.int32, (nch, DH_N), 0)
                o_ref[s, g] = jnp.where(rows == 0, jnp.broadcast_to(blk[0:1, :], (nch, DH_N)), 0.0)


def paged_compress(page_table, cache3, page, new_c, cmp_pe, w1_bf, cmp_w2):
    b, npages = page_table.shape
    assert page % CMP_STRIDE == 0 and npages % PAGES_PER_STEP == 0 and new_c.shape[1] == NEW_ROWS == CMP_STRIDE
    nstep = npages // PAGES_PER_STEP
    nch = PAGES_PER_STEP * page // CMP_STRIDE

    def page_spec(r):
        return pl.BlockSpec((page, 2 * G_N, DH_N),
                            lambda bi, p, pt: (pt[bi, jnp.minimum(p * PAGES_PER_STEP + r, npages - 1)], 0, 0))

    grid_spec = pltpu.PrefetchScalarGridSpec(
        num_scalar_prefetch=1,
        grid=(b, nstep + 1),
        in_specs=[page_spec(r) for r in range(PAGES_PER_STEP)] + [
            pl.BlockSpec((None, NEW_ROWS, 2 * KVW), lambda bi, p, pt: (bi, 0, 0)),
            pl.BlockSpec((2, CMP_LEN, DH_N), lambda bi, p, pt: (0, 0, 0)),
            pl.BlockSpec((2, CMP_LEN // 2, 2 * DH_N, CMP_HID), lambda bi, p, pt: (0, 0, 0, 0)),
            pl.BlockSpec((2, CMP_HID, DH_N), lambda bi, p, pt: (0, 0, 0))],
        out_specs=pl.BlockSpec((None, 2, G_N, nch, DH_N), lambda bi, p, pt: (bi, 0, 0, p, 0)),
        scratch_shapes=[pltpu.VMEM((2 * G_N, QROWS, CMP_HID), jnp.float32),
                        pltpu.VMEM((4, QROWS, CMP_HID), jnp.float32),
                        pltpu.VMEM((2 * G_N, PAGES_PER_STEP * page, DH_N), jnp.float32)])
    return pl.pallas_call(
        functools.partial(_paged_compress_kernel, page=page, nstep=nstep),
        grid_spec=grid_spec,
        out_shape=jax.ShapeDtypeStruct((b, 2, G_N, (nstep + 1) * nch, DH_N), jnp.float32),
        compiler_params=pltpu.CompilerParams(dimension_semantics=("parallel", "arbitrary"),
                                             vmem_limit_bytes=VMEM_LIMIT_BYTES),
        name="nsa_paged_compress",
    )(page_table, *([cache3] * PAGES_PER_STEP), new_c, cmp_pe, w1_bf, cmp_w2)


def _paged_attend_kernel(pt_ref, *refs, page, nstep, l_true):
    pages = refs[:PAGES_PER_STEP]
    (q_ref, kcv_ref, new_ref, winp_ref, wnew_ref, zg_ref, o_ref,
     sel_ref, ocmp_ref, m_ref, l_ref, acc_ref) = refs[PAGES_PER_STEP:]
    p = pl.program_id(1)
    past = nstep * PAGES_PER_STEP * page
    tk = PAGES_PER_STEP * page
    n_r = kcv_ref.shape[2]
    n_s = -(-(past + l_true) // SEL_LEN)
    sl = 2 * LANES
    assert n_s <= sl
    nq = J_N * QROWS
    f32 = jnp.float32
    nt = (((1,), (1,)), ((), ()))
    nsg = 2 * G_N
    wbuf = winp_ref.shape[0]
    flat = [pg.reshape(page * nsg, DH_N) for pg in pages]
    win_flat = winp_ref.reshape(wbuf * nsg, DH_N)

    def q_of(g):
        return jnp.concatenate([q_ref[:, (g * J_N + j) * DH_N:(g * J_N + j + 1) * DH_N] for j in range(J_N)],
                               axis=0).astype(bf16)

    def pos_of(shape):
        return lax.broadcasted_iota(jnp.int32, shape, 0) % QROWS

    def online(g, s, keep, v):
        s = jnp.where(keep, s, NEG)
        m_old = m_ref[g]
        m_new = jnp.maximum(m_old, jnp.max(s, axis=-1, keepdims=True))
        alpha = jnp.exp(m_old - m_new)
        e = jnp.where(keep, jnp.exp(s - m_new), 0.0)
        l_ref[g] = alpha * l_ref[g] + jnp.sum(e, axis=-1, keepdims=True)
        acc_ref[g] = alpha * acc_ref[g] + jnp.dot(e.astype(bf16), v, preferred_element_type=f32)
        m_ref[g] = m_new

    @pl.when(p == 0)
    def _():
        r_i = lax.broadcasted_iota(jnp.int32, (nq, n_r), 1)
        qpos = past + pos_of((nq, n_r))
        cmask = jnp.logical_and(r_i >= 1, (r_i - 1) * CMP_STRIDE + CMP_LEN <= qpos + 1)
        ci = lax.broadcasted_iota(jnp.int32, (n_r, sl), 0) - 1
        si = lax.broadcasted_iota(jnp.int32, (n_r, sl), 1)
        cover = jnp.logical_and(jnp.logical_and(ci >= 0, ci * CMP_STRIDE < si * SEL_LEN + SEL_LEN),
                                ci * CMP_STRIDE + CMP_LEN > si * SEL_LEN).astype(bf16)
        blk = lax.broadcasted_iota(jnp.int32, (QROWS, sl), 1)
        cur = (past + lax.broadcasted_iota(jnp.int32, (QROWS, sl), 0)) // SEL_LEN
        valid = blk <= cur
        forced = jnp.logical_or(blk == 0, blk >= cur - 1)
        s_src = lax.broadcasted_iota(jnp.int32, (sl, sl), 0)
        s_dst = lax.broadcasted_iota(jnp.int32, (sl, sl), 1)
        for g in range(G_N):
            q = q_of(g)
            kc = kcv_ref[0, g].astype(bf16)
            vc = kcv_ref[1, g].astype(bf16)
            s = lax.dot_general(q, kc, nt, preferred_element_type=f32) * SCALE_N
            s = jnp.where(cmask, s, NEG)
            m = jnp.max(s, axis=-1, keepdims=True)
            e = jnp.where(cmask, jnp.exp(s - m), 0.0)
            den = jnp.sum(e, axis=-1, keepdims=True)
            pr = e / jnp.where(den > 0.0, den, 1.0)
            ocmp_ref[g] = jnp.dot(pr.astype(bf16), vc, preferred_element_type=f32)
            p_sum = pr[0:QROWS]
            for j in range(1, J_N):
                p_sum = p_sum + pr[j * QROWS:(j + 1) * QROWS]
            imp = jnp.zeros((QROWS, sl), f32)
            for part in _split3(p_sum):
                imp = imp + jnp.dot(part, cover, preferred_element_type=f32)
            score = jnp.where(valid, imp + jnp.where(forced, BIG, 0.0), -BIG)
            score = jnp.where(blk < n_s, score, NEG_INF)
            score_t = score.T
            sel_rows = []
            for qi in range(QROWS):
                col = score_t[:, qi:qi + 1]
                rw = score[qi:qi + 1, :]
                ahead = jnp.logical_or(col > rw, jnp.logical_and(col == rw, s_src < s_dst))
                rank = jnp.sum(ahead.astype(jnp.int32), axis=0, keepdims=True)
                sel_rows.append((rank < N_SEL).astype(f32))
            sel_ref[g] = jnp.concatenate(sel_rows, axis=0)
            m_ref[g] = jnp.full((nq, 1), NEG, f32)
            l_ref[g] = jnp.zeros((nq, 1), f32)
            acc_ref[g] = jnp.zeros((nq, DH_N), f32)

    @pl.when(p < nstep)
    def _():
        k0 = p * tk
        expand = ((k0 + lax.broadcasted_iota(jnp.int32, (sl, tk), 1)) // SEL_LEN
                  == lax.broadcasted_iota(jnp.int32, (sl, tk), 0)).astype(bf16)
        for g in range(G_N):
            picked = jnp.dot(sel_ref[g].astype(bf16), expand, preferred_element_type=f32)
            keep = jnp.concatenate([picked] * J_N, axis=0) > 0.5
            kk = jnp.concatenate([pg[pl.ds(g, page, stride=nsg), :] for pg in flat], axis=0).astype(bf16)
            vv = jnp.concatenate([pg[pl.ds(G_N + g, page, stride=nsg), :] for pg in flat], axis=0).astype(bf16)
            s = lax.dot_general(q_of(g), kk, nt, preferred_element_type=f32) * SCALE_N
            online(g, s, keep, vv)

    @pl.when(p == nstep)
    def _():
        gates = jax.nn.sigmoid(zg_ref[...])
        gl = lax.broadcasted_iota(jnp.int32, gates.shape, 1)
        pos_n = pos_of((nq, NEW_ROWS))
        l_n = lax.broadcasted_iota(jnp.int32, (nq, NEW_ROWS), 1)
        keep_new = jnp.logical_and(l_n <= pos_n, l_n < l_true)
        pos_w = pos_of((nq, wbuf + NEW_ROWS))
        i_w = lax.broadcasted_iota(jnp.int32, (nq, wbuf + NEW_ROWS), 1)
        keep_win = jnp.logical_or(
            jnp.logical_and(i_w < wbuf, i_w - wbuf > pos_w - WINDOW),
            jnp.logical_and(i_w >= wbuf, jnp.logical_and(i_w - wbuf <= pos_w, i_w - wbuf < l_true)))
        for g in range(G_N):
            q = q_of(g)
            kn = new_ref[:, g * DH_N:(g + 1) * DH_N].astype(bf16)
            vn = new_ref[:, KVW + g * DH_N:KVW + (g + 1) * DH_N].astype(bf16)
            s = lax.dot_general(q, kn, nt, preferred_element_type=f32) * SCALE_N
            online(g, s, keep_new, vn)
            o_sel = acc_ref[g] / l_ref[g]
            kw = jnp.concatenate([win_flat[pl.ds(g, wbuf, stride=nsg), :], wnew_ref[:, g * DH_N:(g + 1) * DH_N]],
                                 axis=0).astype(bf16)
            vw = jnp.concatenate([win_flat[pl.ds(G_N + g, wbuf, stride=nsg), :],
                                  wnew_ref[:, KVW + g * DH_N:KVW + (g + 1) * DH_N]], axis=0).astype(bf16)
            s = lax.dot_general(q, kw, nt, preferred_element_type=f32) * SCALE_N
            o_win = _softmax_pv(s, keep_win, vw)
            o_cmp = ocmp_ref[g]
            for j in range(J_N):
                acc = jnp.zeros((QROWS, DH_N), f32)
                for which, o in enumerate((o_cmp, o_sel, o_win)):
                    col = jnp.sum(jnp.where(gl == ZG_GN + (g * J_N + j) * 3 + which, gates, 0.0), axis=-1, keepdims=True)
                    acc = acc + col * o[j * QROWS:(j + 1) * QROWS]
                o_ref[:, (g * J_N + j) * DH_N:(g * J_N + j + 1) * DH_N] = acc.astype(o_ref.dtype)


def paged_attend(page_table, cache3, page, q8, kcv, new_s, winp, wnew, zg8, l_true):
    b, npages = page_table.shape
    nstep = npages // PAGES_PER_STEP
    assert winp.shape[0] == b * WINDOW and l_true <= QROWS and npages % PAGES_PER_STEP == 0
    n_r = kcv.shape[3]
    nq = J_N * QROWS

    def page_spec(r):
        return pl.BlockSpec((page, 2 * G_N, DH_N),
                            lambda bi, p, pt: (pt[bi, jnp.minimum(p * PAGES_PER_STEP + r, npages - 1)], 1, 0))

    per_b = lambda rows, cols: pl.BlockSpec((None, rows, cols), lambda bi, p, pt: (bi, 0, 0))
    grid_spec = pltpu.PrefetchScalarGridSpec(
        num_scalar_prefetch=1,
        grid=(b, nstep + 1),
        in_specs=[page_spec(r) for r in range(PAGES_PER_STEP)] + [
            per_b(QROWS, D_N),
            pl.BlockSpec((None, 2, G_N, n_r, DH_N), lambda bi, p, pt: (bi, 0, 0, 0, 0)),
            per_b(NEW_ROWS, 2 * KVW), pl.BlockSpec((WINDOW, 2 * G_N, DH_N), lambda bi, p, pt: (bi, 0, 0)),
            per_b(NEW_ROWS, 2 * KVW), per_b(QROWS, LANES)],
        out_specs=per_b(QROWS, D_N),
        scratch_shapes=[pltpu.VMEM((G_N, QROWS, 2 * LANES), jnp.float32),
                        pltpu.VMEM((G_N, nq, DH_N), jnp.float32),
                        pltpu.VMEM((G_N, nq, 1), jnp.float32),
                        pltpu.VMEM((G_N, nq, 1), jnp.float32),
                        pltpu.VMEM((G_N, nq, DH_N), jnp.float32)])
    return pl.pallas_call(
        functools.partial(_paged_attend_kernel, page=page, nstep=nstep, l_true=l_true),
        grid_spec=grid_spec,
        out_shape=jax.ShapeDtypeStruct((b, QROWS, D_N), jnp.float32),
        compiler_params=pltpu.CompilerParams(dimension_semantics=("parallel", "arbitrary"),
                                             vmem_limit_bytes=VMEM_LIMIT_BYTES),
        name="nsa_paged_attend",
    )(page_table, *([cache3] * PAGES_PER_STEP), q8, kcv, new_s, winp, wnew, zg8)


def _extract_top(s, k):
    rows = lax.broadcasted_iota(jnp.int32, s.shape, 0)
    nrow = s.shape[0]
    work = s
    taken = jnp.zeros(s.shape, jnp.bool_)
    tops = []
    for _ in range(k):
        m = jnp.max(work, axis=0, keepdims=True)
        first = jnp.min(jnp.where(work == m, rows, nrow), axis=0, keepdims=True)
        hit = rows == first
        taken = jnp.logical_or(taken, hit)
        work = jnp.where(hit, NEG_INF, work)
        tops.append(m)
    return tops, taken


def _peer_route_kernel(qt_ref, keys_ref, s1_ref, e1_ref, s2_ref, e2_ref, tau_ref):
    halves = []
    for c in range(2):
        q = qt_ref[c * PEER_QHALF:(c + 1) * PEER_QHALF, :].astype(bf16)
        s = jnp.dot(keys_ref[c].astype(bf16), q, preferred_element_type=jnp.float32)
        tops, taken = _extract_top(s, PEER_TOPK)
        halves.append((s, tops, taken))
    (sa, ta, ma), (sb, tb, mb) = halves
    assert PEER_TOPK == 16
    ta_col = jnp.concatenate(ta, axis=0)
    tb_col = jnp.concatenate(tb, axis=0)
    a_idx = lax.broadcasted_iota(jnp.int32, (8, ta_col.shape[1]), 0)
    cand = jnp.concatenate(
        [ta[0] + tb_col, ta[1] + tb_col, ta[2] + tb_col[0:8], ta[3] + tb_col[0:8]]
        + [jnp.where(a_idx >= 4, ta_col[0:8] + tb[b], NEG_INF) for b in range(3)]
        + [ta_col[8:16] + tb[0]], axis=0)
    ctops, _ = _extract_top(cand, PEER_TOPK)
    cmax = ctops[0]
    z = jnp.exp(ctops[0] - cmax)
    for r in range(1, PEER_TOPK):
        z = z + jnp.exp(ctops[r] - cmax)
    tau_ref[...] = ctops[PEER_TOPK - 1]
    s1_ref[...] = jnp.where(ma, sa, NEG_INF)
    s2_ref[...] = jnp.where(mb, sb, NEG_INF)
    e1_ref[...] = jnp.where(ma, jnp.exp(sa - ta[0]), 0.0) / z
    e2_ref[...] = jnp.where(mb, jnp.exp(sb - tb[0]), 0.0)


def peer_route(qt, pkeys, tb):
    n = qt.shape[1]
    big = jax.ShapeDtypeStruct((PEER_HEADS, N_KEYS, n), jnp.float32)
    bspec = pl.BlockSpec((None, N_KEYS, tb), lambda i, h: (h, 0, i))
    return pl.pallas_call(
        _peer_route_kernel,
        grid=(n // tb, PEER_HEADS),
        in_specs=[pl.BlockSpec((2 * PEER_QHALF, tb), lambda i, h: (h, i)),
                  pl.BlockSpec((None, 2, N_KEYS, PEER_QHALF), lambda i, h: (h, 0, 0, 0))],
        out_specs=[bspec, bspec, bspec, bspec, pl.BlockSpec((None, 1, tb), lambda i, h: (h, 0, i))],
        out_shape=[big, big, big, big, jax.ShapeDtypeStruct((PEER_HEADS, 1, n), jnp.float32)],
        compiler_params=pltpu.CompilerParams(dimension_semantics=("parallel", "parallel")),
        name="peer_route",
    )(qt, pkeys)


def _peer_dense_kernel(xt_ref, u_ref, vt_ref, s1_ref, e1_ref, s2_ref, e2_ref, tau_ref, o_ref, *, sub):
    j = pl.program_id(1)

    @pl.when(j == 0)
    def _():
        o_ref[...] = jnp.zeros_like(o_ref)

    ht = jnp.dot(u_ref[...], xt_ref[...], preferred_element_type=jnp.float32)
    acts = []
    for a in range(sub):
        i1 = j * sub + a
        g = None
        for h in range(PEER_HEADS):
            s1row = s1_ref[h, pl.ds(i1, 1), :]
            e1row = e1_ref[h, pl.ds(i1, 1), :]
            c = s2_ref[h] + s1row
            t = jnp.where(c >= tau_ref[h], e2_ref[h], 0.0) * e1row
            g = t if g is None else g + t
        acts.append((_gelu_tanh(ht[a * N_KEYS:(a + 1) * N_KEYS, :]) * g).astype(bf16))
    act = jnp.concatenate(acts, axis=0) if sub > 1 else acts[0]
    o_ref[...] += jnp.dot(vt_ref[...], act, preferred_element_type=jnp.float32)


def peer_dense(xt, u_bf, vt_bf, s1, e1, s2, e2, tau, tb, sub):
    d, n = xt.shape
    e = u_bf.shape[0]
    te = sub * N_KEYS
    once = dict(pipeline_mode=pl.Buffered(1))
    rspec = pl.BlockSpec((PEER_HEADS, N_KEYS, tb), lambda i, j: (0, 0, i), **once)
    return pl.pallas_call(
        functools.partial(_peer_dense_kernel, sub=sub),
        grid=(n // tb, e // te),
        in_specs=[pl.BlockSpec((d, tb), lambda i, j: (0, i), **once),
                  pl.BlockSpec((te, d), lambda i, j: (j, 0)),
                  pl.BlockSpec((d, te), lambda i, j: (0, j)),
                  rspec, rspec, rspec, rspec,
                  pl.BlockSpec((PEER_HEADS, 1, tb), lambda i, j: (0, 0, i), **once)],
        out_specs=pl.BlockSpec((d, tb), lambda i, j: (0, i)),
        out_shape=jax.ShapeDtypeStruct((d, n), jnp.float32),
        compiler_params=pltpu.CompilerParams(dimension_semantics=("parallel", "arbitrary"),
                                             vmem_limit_bytes=BIG_VMEM_LIMIT_BYTES),
        name="peer_dense",
    )(xt, u_bf, vt_bf, s1, e1, s2, e2, tau)


def peer_ffn_t(xt, w_pq_t, pkeys, u_bf, vt_bf):
    n = xt.shape[1]
    qt = pmm(w_pq_t, xt)
    s1, e1, s2, e2, tau = peer_route(qt, pkeys, _pick(n, (256, 128)))
    return peer_dense(xt, u_bf, vt_bf, s1, e1, s2, e2, tau, _pick(n, (512, 256, 128)), PEER_SUB)


def _in_proj_weights(w_in):
    o = _IN_OFFS
    w_main = jnp.concatenate([w_in[:, o[0]:o[3]], w_in[:, o[5]:o[8]], w_in[:, o[9]:o[11]]], axis=1).astype(bf16)
    w_gate = jnp.concatenate([w_in[:, o[3]:o[5]], w_in[:, o[8]:o[9]],
                              jnp.zeros((w_in.shape[0], LANES - 2 * NH_M - 3 * H_N), w_in.dtype)], axis=1).astype(bf16)
    return w_main, w_gate


def kernel(x_prompt, x_sample, cache_kv, state_win_kv, state_conv, state_C, state_n, state_m, page_table,
           norm_mix, norm_ffn, norm_final, w_in, w_conv, b_conv, w_bd, b_gates, w_hnorm, w_skip,
           cmp_pe, cmp_w1, cmp_w2, w_br_m, w_br_n, w_out, w_pq, peer_keys, peer_u, peer_v):
    assert w_in.shape[0] == DEPTH == 1
    f32 = jnp.float32
    l = 0
    Bp, Lp_, _ = x_prompt.shape
    Bs, Ls, _ = x_sample.shape
    w_main, w_gate = _in_proj_weights(w_in[l])
    wbd_full = expand_blockdiag(w_bd[l])
    gate_bias = jnp.concatenate([b_gates[l, 0], b_gates[l, 1], jnp.zeros((LANES - 2 * NH_M,), f32)])[None]
    w_br_m_bf, w_br_n_bf, w_out_bf = w_br_m[l].astype(bf16), w_br_n[l].astype(bf16), w_out[l].astype(bf16)
    w_pq_t = w_pq[l].T.astype(bf16)
    u_bf = peer_u[l].astype(bf16)
    vt_bf = peer_v[l].T.astype(bf16)

    def mixers(x, conv_prev, C0, n0, m0, nsa_fn, ch, lpad):
        B, L, _ = x.shape
        n = B * L
        x2 = x.reshape(n, D_MODEL)
        xn = rmsnorm_rows(x2, norm_mix[l], bf16)
        z2 = pmm(xn, w_main)
        zg2 = pmm(xn, w_gate)
        z3 = z2.reshape(B, L, Z_COLS)
        zg3 = zg2.reshape(B, L, LANES)
        xm = z3[:, :, Z_XM:Z_XM + D_M]
        new_conv = jnp.concatenate([conv_prev, xm], axis=1)[:, L:]
        prev8 = jnp.pad(conv_prev, ((0, 0), (PREV_ROWS - (CONV_W - 1), 0), (0, 0)))
        if lpad == L:
            zm3, zgm3, cols = z3, zg3, (Z_XM // DH_M, Z_VM // DH_M, Z_OM // DH_M)
        else:
            zm3 = jnp.pad(z3[:, :, :Z_Q], ((0, 0), (0, lpad - L), (0, 0)))
            zgm3 = jnp.pad(zg3, ((0, 0), (0, lpad - L), (0, 0)))
            cols = (Z_XM // DH_M, Z_VM // DH_M, Z_OM // DH_M)
        y_m, C1, n1, m1 = mlstm_branch(zm3, zgm3, cols, prev8, w_conv[l], b_conv[l], wbd_full, gate_bias,
                                       C0, n0[:, :, None, :], jnp.broadcast_to(m0[:, :, None, None], (B, NH_M, 1, LANES)),
                                       w_hnorm[l], w_skip[l], L, ch)
        y_m = y_m[:, :L].reshape(n, D_M)
        kv_new = z3[:, :, Z_KV:Z_KV + N_KV_SETS * KVW].reshape(B, L, N_KV_SETS, G_N, DH_N)
        win_new = z3[:, :, Z_WIN:Z_WIN + 2 * KVW].reshape(B, L, 2, G_N, DH_N)
        y_n, new_win = nsa_fn(z3, zg3, kv_new, win_new)
        merged = merge_branches(y_m, y_n.reshape(n, D_N).astype(bf16), w_br_m_bf, w_br_n_bf, z2)
        h2 = out_proj_residual(merged, w_out_bf, x2)
        return h2, kv_new, new_win, new_conv, C1, n1[:, :, 0], m1[:, :, 0, 0]

    def nsa_p(z3, zg3, kv_new, win_new):
        kcv = compress_prompt(z3, Z_KV // DH_N, cmp_pe[l], cmp_w1[l], cmp_w2[l])
        return nsa_prompt(z3, zg3, kcv), win_new[:, -min(WINDOW, z3.shape[1]):]

    def nsa_s(z3, zg3, kv_new, win_new):
        L = z3.shape[1]
        padr = lambda t, rows: jnp.pad(t, ((0, 0), (0, rows - t.shape[1]), (0, 0)))
        page = cache_kv.shape[2]
        cache3 = cache_kv.reshape(DEPTH * cache_kv.shape[1] * page, N_KV_SETS * G_N, DH_N)
        wbuf = state_win_kv.shape[2]
        win3 = state_win_kv.reshape(Bs * wbuf, 2 * G_N, DH_N)
        kcv = paged_compress(page_table, cache3, page, padr(z3[:, :, Z_KV:Z_KV + 2 * KVW], NEW_ROWS),
                             cmp_pe[l], cmp_w1[l].astype(bf16).reshape(2, CMP_LEN // 2, 2 * DH_N, CMP_HID), cmp_w2[l])
        y8 = paged_attend(page_table, cache3, page, padr(z3[:, :, Z_Q:Z_Q + D_N], QROWS), kcv,
                          padr(z3[:, :, Z_KV + 2 * KVW:Z_KV + 4 * KVW], NEW_ROWS),
                          win3, padr(z3[:, :, Z_WIN:Z_WIN + 2 * KVW], NEW_ROWS), padr(zg3, QROWS), L)
        new_win = jnp.concatenate([state_win_kv.reshape(Bs, wbuf, 2, G_N, DH_N)[:, L:], win_new], axis=1)
        return y8[:, :L], new_win

    hp, kvp, winp, convp, Cp, n_p, m_p = mixers(
        x_prompt, jnp.zeros((Bp, CONV_W - 1, D_M), f32), jnp.zeros((Bp, NH_M, DH_M, DH_M), f32),
        jnp.zeros((Bp, NH_M, DH_M), f32), jnp.zeros((Bp, NH_M), f32), nsa_p, 256, Lp_)
    hs, kvs, wins, convs, Cs, n_s, m_s = mixers(
        x_sample, state_conv.reshape(state_conv.shape[1:]), state_C.reshape(state_C.shape[1:]),
        state_n.reshape(state_n.shape[1:]), state_m.reshape(state_m.shape[1:]), nsa_s, 16, 16)

    def ffn_and_norm(h2):
        xt = rmsnorm_rows_t(h2, norm_ffn[l], bf16)
        peer_t = peer_ffn_t(xt, w_pq_t, peer_keys[l], u_bf, vt_bf)
        return final_norm(h2, peer_t, norm_final)

    y_prompt = ffn_and_norm(hp).reshape(x_prompt.shape)
    y_sample = ffn_and_norm(hs).reshape(x_sample.shape)
    st = lambda t: t[None]
    return (y_prompt, y_sample, st(kvp), st(kvs), st(winp), st(wins), st(convp), st(convs),
            st(Cp), st(Cs), st(n_p), st(n_s), st(m_p), st(m_s))
```

```python
import functools

import jax
import jax.numpy as jnp
import numpy as np
from jax import lax
from jax.experimental import pallas as pl
from jax.experimental.pallas import tpu as pltpu

D_MODEL = 4096
DEPTH = 1
D_M = D_MODEL // 2
DH_M = 256
NH_M = D_M // DH_M
CONV_W = 4
QKV_BLOCK = 4
D_N = D_MODEL // 2
DH_N = 128
H_N = D_N // DH_N
G_N = 4
J_N = H_N // G_N
KVW = G_N * DH_N
N_KV_SETS = 4
CMP_STRIDE = 16
CMP_LEN = 2 * CMP_STRIDE
CMP_HID = 2 * DH_N
SEL_LEN = 64
N_SEL = 16
WINDOW = 512
SCALE_N = DH_N ** -0.5
PEER_HEADS = 8
N_KEYS = 128
PEER_TOPK = 16
PEER_QDIM = 256
PEER_QHALF = PEER_QDIM // 2
IN_SIZES = (D_M, D_M, D_M, NH_M, NH_M, D_N, N_KV_SETS * KVW, 2 * KVW, 3 * H_N, D_MODEL, D_MODEL)
EPS = 1e-6
NEG = -1e30
BIG = 1e9
NEG_INF = float('-inf')

LANES = 128
VMEM_LIMIT_BYTES = 48 * 1024 * 1024
BIG_VMEM_LIMIT_BYTES = 56 * 1024 * 1024
PEER_SUB = 4
PREV_ROWS = 8
PAGES_PER_STEP = 16
NEW_ROWS = 16
QROWS = 8
SEL_CHUNK = 512
bf16 = jnp.bfloat16

_IN_OFFS = np.concatenate([[0], np.cumsum(IN_SIZES)]).tolist()
Z_XM, Z_VM, Z_OM = 0, D_M, 2 * D_M
Z_Q = 3 * D_M
Z_KV = Z_Q + D_N
Z_WIN = Z_KV + N_KV_SETS * KVW
Z_GM = Z_WIN + 2 * KVW
Z_GN = Z_GM + D_MODEL
Z_COLS = Z_GN + D_MODEL
ZG_IG, ZG_FG, ZG_GN = 0, NH_M, 2 * NH_M


def _pick(n, cands):
    for c in cands:
        if n % c == 0:
            return c
    return n


def _gelu_tanh(x):
    return 0.5 * x * (1.0 + jnp.tanh(np.float32(np.sqrt(2.0 / np.pi)) * (x + np.float32(0.044715) * (x * x * x))))


def _split3(x):
    hi = x.astype(bf16)
    r = x - hi.astype(jnp.float32)
    mid = r.astype(bf16)
    lo = (r - mid.astype(jnp.float32)).astype(bf16)
    return hi, mid, lo


def _rmsnorm_kernel(x_ref, g_ref, o_ref):
    x = x_ref[...]
    o_ref[...] = (x * lax.rsqrt(jnp.mean(x * x, axis=-1, keepdims=True) + EPS) * g_ref[...]).astype(o_ref.dtype)


def rmsnorm_rows(x2, g, out_dtype):
    n, d = x2.shape
    tr = _pick(n, (256, 128))
    return pl.pallas_call(
        _rmsnorm_kernel,
        grid=(n // tr,),
        in_specs=[pl.BlockSpec((tr, d), lambda i: (i, 0)), pl.BlockSpec((1, d), lambda i: (0, 0))],
        out_specs=pl.BlockSpec((tr, d), lambda i: (i, 0)),
        out_shape=jax.ShapeDtypeStruct((n, d), out_dtype),
        compiler_params=pltpu.CompilerParams(dimension_semantics=("parallel",), vmem_limit_bytes=VMEM_LIMIT_BYTES),
        name="rmsnorm",
    )(x2, g.reshape(1, d))


def _rmsnorm_t_kernel(x_ref, g_ref, o_ref):
    x = x_ref[...]
    o_ref[...] = (x * lax.rsqrt(jnp.mean(x * x, axis=-1, keepdims=True) + EPS) * g_ref[...]).T.astype(o_ref.dtype)


def rmsnorm_rows_t(x2, g, out_dtype):
    n, d = x2.shape
    tr = _pick(n, (256, 128))
    return pl.pallas_call(
        _rmsnorm_t_kernel,
        grid=(n // tr,),
        in_specs=[pl.BlockSpec((tr, d), lambda i: (i, 0)), pl.BlockSpec((1, d), lambda i: (0, 0))],
        out_specs=pl.BlockSpec((d, tr), lambda i: (0, i)),
        out_shape=jax.ShapeDtypeStruct((d, n), out_dtype),
        compiler_params=pltpu.CompilerParams(dimension_semantics=("parallel",), vmem_limit_bytes=VMEM_LIMIT_BYTES),
        name="rmsnorm_t",
    )(x2, g.reshape(1, d))


def _mm_kernel(a_ref, b_ref, o_ref):
    o_ref[...] = jnp.dot(a_ref[...], b_ref[...], preferred_element_type=jnp.float32).astype(o_ref.dtype)


def pmm(a, b, out_dtype=jnp.float32):
    m, kd = a.shape
    n = b.shape[1]
    tm = _pick(m, (1024, 512, 256, 128))
    tn = _pick(n, (1024, 512, 256, 128))
    return pl.pallas_call(
        _mm_kernel,
        grid=(m // tm, n // tn),
        in_specs=[pl.BlockSpec((tm, kd), lambda i, j: (i, 0)), pl.BlockSpec((kd, tn), lambda i, j: (0, j))],
        out_specs=pl.BlockSpec((tm, tn), lambda i, j: (i, j)),
        out_shape=jax.ShapeDtypeStruct((m, n), out_dtype),
        compiler_params=pltpu.CompilerParams(dimension_semantics=("parallel", "parallel"),
                                             vmem_limit_bytes=BIG_VMEM_LIMIT_BYTES),
        name="proj",
    )(a, b)


def _merge_kernel(ym_ref, yn_ref, wm_ref, wn_ref, gm_ref, gn_ref, o_ref):
    pm = jnp.dot(ym_ref[...], wm_ref[...], preferred_element_type=jnp.float32)
    pn = jnp.dot(yn_ref[...], wn_ref[...], preferred_element_type=jnp.float32)
    o_ref[...] = (jax.nn.sigmoid(gm_ref[...]) * pm + jax.nn.sigmoid(gn_ref[...]) * pn).astype(o_ref.dtype)


def merge_branches(ym, yn, wm, wn, z2):
    m = ym.shape[0]
    n = wm.shape[1]
    tm = _pick(m, (1024, 512, 256, 128))
    tn = 512
    return pl.pallas_call(
        _merge_kernel,
        grid=(m // tm, n // tn),
        in_specs=[pl.BlockSpec((tm, D_M), lambda i, j: (i, 0)), pl.BlockSpec((tm, D_N), lambda i, j: (i, 0)),
                  pl.BlockSpec((D_M, tn), lambda i, j: (0, j)), pl.BlockSpec((D_N, tn), lambda i, j: (0, j)),
                  pl.BlockSpec((tm, tn), lambda i, j: (i, Z_GM // tn + j)),
                  pl.BlockSpec((tm, tn), lambda i, j: (i, Z_GN // tn + j))],
        out_specs=pl.BlockSpec((tm, tn), lambda i, j: (i, j)),
        out_shape=jax.ShapeDtypeStruct((m, n), bf16),
        compiler_params=pltpu.CompilerParams(dimension_semantics=("parallel", "parallel"),
                                             vmem_limit_bytes=VMEM_LIMIT_BYTES),
        name="merge_branches",
    )(ym, yn, wm, wn, z2, z2)


def _mm_res_kernel(a_ref, b_ref, r_ref, o_ref):
    o_ref[...] = r_ref[...] + jnp.dot(a_ref[...], b_ref[...], preferred_element_type=jnp.float32)


def out_proj_residual(a, b, r):
    m, kd = a.shape
    n = b.shape[1]
    tm = _pick(m, (1024, 512, 256, 128))
    tn = 512
    return pl.pallas_call(
        _mm_res_kernel,
        grid=(m // tm, n // tn),
        in_specs=[pl.BlockSpec((tm, kd), lambda i, j: (i, 0)), pl.BlockSpec((kd, tn), lambda i, j: (0, j)),
                  pl.BlockSpec((tm, tn), lambda i, j: (i, j))],
        out_specs=pl.BlockSpec((tm, tn), lambda i, j: (i, j)),
        out_shape=jax.ShapeDtypeStruct((m, n), jnp.float32),
        compiler_params=pltpu.CompilerParams(dimension_semantics=("parallel", "parallel"),
                                             vmem_limit_bytes=VMEM_LIMIT_BYTES),
        name="out_proj",
    )(a, b, r)


def _final_kernel(h_ref, pt_ref, g_ref, o_ref):
    x = h_ref[...] + pt_ref[...].T
    o_ref[...] = x * lax.rsqrt(jnp.mean(x * x, axis=-1, keepdims=True) + EPS) * g_ref[...]


def final_norm(h2, peer_t, g):
    n, d = h2.shape
    tr = _pick(n, (256, 128))
    return pl.pallas_call(
        _final_kernel,
        grid=(n // tr,),
        in_specs=[pl.BlockSpec((tr, d), lambda i: (i, 0)), pl.BlockSpec((d, tr), lambda i: (0, i)),
                  pl.BlockSpec((1, d), lambda i: (0, 0))],
        out_specs=pl.BlockSpec((tr, d), lambda i: (i, 0)),
        out_shape=jax.ShapeDtypeStruct((n, d), jnp.float32),
        compiler_params=pltpu.CompilerParams(dimension_semantics=("parallel",), vmem_limit_bytes=VMEM_LIMIT_BYTES),
        name="final_norm",
    )(h2, peer_t, g.reshape(1, d))


def _blockdiag_apply(x, w_ref, which):
    parts = [jnp.dot(x[:, hb * LANES:(hb + 1) * LANES].astype(bf16), w_ref[which, hb].astype(bf16),
                     preferred_element_type=jnp.float32) for hb in range(DH_M // LANES)]
    return jnp.concatenate(parts, axis=1)


def _mlstm_kernel(xm_ref, vm_ref, om_ref, zg_ref, prev_ref, wc_ref, bc_ref, wbd_ref, gb_ref, c0_ref, n0_ref, m0_ref,
                  hn_ref, sk_ref, y_ref, c1_ref, n1_ref, m1_ref, *, ch, l_true):
    h = pl.program_id(1)
    lp = xm_ref.shape[0]
    nchunk = lp // ch
    f32 = jnp.float32
    tri = lax.broadcasted_iota(jnp.int32, (ch, ch), 0) >= lax.broadcasted_iota(jnp.int32, (ch, ch), 1)
    tri_bf = tri.astype(bf16)
    lane_g = lax.broadcasted_iota(jnp.int32, (ch, LANES), 1)
    sub_g = lax.broadcasted_iota(jnp.int32, (LANES, ch), 0)

    def chunk(ci, carry):
        c_st, n_st, m_st = carry
        r0 = pl.multiple_of(ci * ch, ch)
        xm = xm_ref[pl.ds(r0, ch), :]
        before = xm_ref[pl.ds(pl.multiple_of(jnp.maximum(r0 - PREV_ROWS, 0), PREV_ROWS), PREV_ROWS), :]
        before = jnp.where(ci == 0, prev_ref[...], before)
        xe = jnp.concatenate([before, xm], axis=0)
        lo = PREV_ROWS - (CONV_W - 1)
        conv = bc_ref[...] + sum(xe[lo + w:lo + w + ch, :] * wc_ref[w:w + 1, :] for w in range(CONV_W))
        c = conv * jax.nn.sigmoid(conv)
        q = _blockdiag_apply(c, wbd_ref, 0)
        k = _blockdiag_apply(c, wbd_ref, 1) * (DH_M ** -0.5)
        v = _blockdiag_apply(vm_ref[pl.ds(r0, ch), :], wbd_ref, 2)
        zg = zg_ref[pl.ds(r0, ch), :] + gb_ref[...]
        live = (r0 + lax.broadcasted_iota(jnp.int32, (ch, LANES), 0)) < l_true
        a_all = jnp.where(live, zg, NEG_INF)
        lf_all = jnp.where(live, jnp.minimum(zg, 0.0) - jnp.log1p(jnp.exp(-jnp.abs(zg))), 0.0)
        b_all = jnp.zeros((ch, LANES), f32)
        for part in _split3(lf_all):
            b_all = b_all + jnp.dot(tri_bf, part, preferred_element_type=f32)
        a_col = jnp.sum(jnp.where(lane_g == ZG_IG + h, a_all, 0.0), axis=1, keepdims=True)
        b_col = jnp.sum(jnp.where(lane_g == ZG_FG + h, b_all, 0.0), axis=1, keepdims=True)
        a_row = jnp.sum(jnp.where(sub_g == ZG_IG + h, a_all.T, 0.0), axis=0, keepdims=True)
        b_row = jnp.sum(jnp.where(sub_g == ZG_FG + h, b_all.T, 0.0), axis=0, keepdims=True)
        dmat = jnp.where(tri, b_col - b_row + a_row, NEG_INF)
        inter = b_col + m_st
        m = jnp.maximum(inter, jnp.max(dmat, axis=1, keepdims=True))
        w_intra = jnp.exp(dmat - m)
        w_inter = jnp.exp(inter - m)
        qb = q.astype(bf16)
        kb = k.astype(bf16)
        vb = v.astype(bf16)
        qk = lax.dot_general(qb, kb, (((1,), (1,)), ((), ())), preferred_element_type=f32) * w_intra
        num = w_inter * jnp.dot(qb, c_st.astype(bf16), preferred_element_type=f32) \
            + jnp.dot(qk.astype(bf16), vb, preferred_element_type=f32)
        den = w_inter * jnp.sum(q * n_st, axis=1, keepdims=True) + jnp.sum(qk, axis=1, keepdims=True)
        hh = num / jnp.maximum(jnp.abs(den), jnp.exp(-m))
        m_end = m[ch - 1:ch, :]
        b_end = b_col[ch - 1:ch, :]
        w_end = jnp.exp(b_end - b_col + a_col - m_end)
        decay = jnp.exp(b_end + m_st - m_end)
        kw = k * w_end
        c_new = decay * c_st + lax.dot_general(kw.astype(bf16), vb, (((0,), (0,)), ((), ())),
                                               preferred_element_type=f32)
        n_new = decay * n_st + jnp.sum(kw, axis=0, keepdims=True)
        hg = hh * jax.nn.sigmoid(om_ref[pl.ds(r0, ch), :])
        mu = jnp.mean(hg, axis=1, keepdims=True)
        var = jnp.mean(jnp.square(hg - mu), axis=1, keepdims=True)
        y = (hg - mu) * lax.rsqrt(var + EPS) * hn_ref[...] + sk_ref[...] * c
        y_ref[pl.ds(r0, ch), :] = y.astype(y_ref.dtype)
        return c_new, n_new, m_end

    c_fin, n_fin, m_fin = lax.fori_loop(0, nchunk, chunk, (c0_ref[...], n0_ref[...], m0_ref[:, :1]))
    c1_ref[...] = c_fin
    n1_ref[...] = n_fin
    m1_ref[...] = jnp.broadcast_to(m_fin, m1_ref.shape)


def expand_blockdiag(w_bd):
    per = LANES // QKV_BLOCK
    w = w_bd.reshape(3, D_M // LANES, per, QKV_BLOCK, QKV_BLOCK)
    full = jnp.einsum('tbnio,nm->tbnimo', w, jnp.eye(per, dtype=w_bd.dtype))
    return full.reshape(3, D_M // LANES, LANES, LANES)


def mlstm_branch(z3, zg3, col_blocks, prev8, w_conv, b_conv, wbd_full, gate_bias, c0, n0, m0, w_hnorm, w_skip,
                 l_true, ch):
    b, lp, _ = z3.shape
    cx, cv, co = col_blocks
    kern = functools.partial(_mlstm_kernel, ch=ch, l_true=l_true)
    col = lambda c0_: pl.BlockSpec((None, lp, DH_M), lambda bi, h: (bi, 0, c0_ + h))
    vec = pl.BlockSpec((1, DH_M), lambda bi, h: (0, h))
    st = lambda r, c: pl.BlockSpec((None, None, r, c), lambda bi, h: (bi, h, 0, 0))
    return pl.pallas_call(
        kern,
        grid=(b, NH_M),
        in_specs=[col(cx), col(cv), col(co),
                  pl.BlockSpec((None, lp, LANES), lambda bi, h: (bi, 0, 0)),
                  pl.BlockSpec((None, PREV_ROWS, DH_M), lambda bi, h: (bi, 0, h)),
                  pl.BlockSpec((CONV_W, DH_M), lambda bi, h: (0, h)),
                  vec,
                  pl.BlockSpec((3, DH_M // LANES, LANES, LANES), lambda bi, h: (0, h, 0, 0)),
                  pl.BlockSpec((1, LANES), lambda bi, h: (0, 0)),
                  st(DH_M, DH_M), st(1, DH_M), st(1, LANES), vec, vec],
        out_specs=[pl.BlockSpec((None, lp, DH_M), lambda bi, h: (bi, 0, h)),
                   st(DH_M, DH_M), st(1, DH_M), st(1, LANES)],
        out_shape=[jax.ShapeDtypeStruct((b, lp, D_M), bf16),
                   jax.ShapeDtypeStruct((b, NH_M, DH_M, DH_M), jnp.float32),
                   jax.ShapeDtypeStruct((b, NH_M, 1, DH_M), jnp.float32),
                   jax.ShapeDtypeStruct((b, NH_M, 1, LANES), jnp.float32)],
        compiler_params=pltpu.CompilerParams(dimension_semantics=("parallel", "parallel"),
                                             vmem_limit_bytes=VMEM_LIMIT_BYTES),
        name="mlstm_branch",
    )(z3, z3, z3, zg3, prev8, w_conv, b_conv.reshape(1, D_M), wbd_full, gate_bias, c0, n0, m0,
      w_hnorm.reshape(1, D_M), w_skip.reshape(1, D_M))


def _compress_kernel(x_ref, pe_ref, w1_ref, w2_ref, o_ref):
    nchunk = x_ref.shape[0] // CMP_STRIDE
    first = jnp.zeros((nchunk, CMP_HID), jnp.float32)
    second = jnp.zeros((nchunk, CMP_HID), jnp.float32)
    for l in range(CMP_STRIDE):
        xl = x_ref[pl.ds(l, nchunk, stride=CMP_STRIDE), :]
        first += jnp.dot((xl + pe_ref[l:l + 1, :]).astype(bf16), w1_ref[l].astype(bf16),
                         preferred_element_type=jnp.float32)
        second += jnp.dot((xl + pe_ref[CMP_STRIDE + l:CMP_STRIDE + l + 1, :]).astype(bf16),
                          w1_ref[CMP_STRIDE + l].astype(bf16), preferred_element_type=jnp.float32)
    nxt = jnp.concatenate([second[1:], jnp.zeros((1, CMP_HID), jnp.float32)], axis=0)
    row = lax.broadcasted_iota(jnp.int32, (nchunk, CMP_HID), 0)
    hid = jnp.where(row < nchunk - 1, _gelu_tanh(first + nxt), 0.0)
    o_ref[...] = jnp.dot(hid.astype(bf16), w2_ref[...].astype(bf16), preferred_element_type=jnp.float32)


def compress_prompt(z3, col0, cmp_pe, cmp_w1, cmp_w2):
    b, l, _ = z3.shape
    nchunk = l // CMP_STRIDE
    return pl.pallas_call(
        _compress_kernel,
        grid=(b, 2, G_N),
        in_specs=[pl.BlockSpec((None, l, DH_N), lambda bi, s, g: (bi, 0, col0 + s * G_N + g)),
                  pl.BlockSpec((None, CMP_LEN, DH_N), lambda bi, s, g: (s, 0, 0)),
                  pl.BlockSpec((None, CMP_LEN, DH_N, CMP_HID), lambda bi, s, g: (s, 0, 0, 0)),
                  pl.BlockSpec((None, CMP_HID, DH_N), lambda bi, s, g: (s, 0, 0))],
        out_specs=pl.BlockSpec((None, None, None, nchunk, DH_N), lambda bi, s, g: (bi, s, g, 0, 0)),
        out_shape=jax.ShapeDtypeStruct((b, 2, G_N, nchunk, DH_N), jnp.float32),
        compiler_params=pltpu.CompilerParams(dimension_semantics=("parallel", "parallel", "parallel"),
                                             vmem_limit_bytes=VMEM_LIMIT_BYTES),
        name="nsa_compress",
    )(z3, cmp_pe, cmp_w1, cmp_w2)


def _softmax_pv(s, keep, v):
    s = jnp.where(keep, s, NEG)
    m = jnp.max(s, axis=-1, keepdims=True)
    e = jnp.exp(s - m)
    den = jnp.sum(e, axis=-1, keepdims=True)
    return jnp.dot(e.astype(bf16), v, preferred_element_type=jnp.float32) / den


def _nsa_prompt_kernel(q_ref, kc_ref, vc_ref, ks_ref, vs_ref, kw_ref, vw_ref, gn_ref, o_ref, *, tq):
    g = pl.program_id(1)
    qi = pl.program_id(2)
    f32 = jnp.float32
    t = ks_ref.shape[0]
    n_c = kc_ref.shape[0]
    n_s = t // SEL_LEN
    nq = J_N * tq
    q4 = jnp.concatenate([q_ref[:, j * DH_N:(j + 1) * DH_N] for j in range(J_N)], axis=0).astype(bf16)
    nt = (((1,), (1,)), ((), ()))

    def qpos_of(shape):
        return qi * tq + lax.broadcasted_iota(jnp.int32, shape, 0) % tq

    cidx = lax.broadcasted_iota(jnp.int32, (nq, n_c), 1)
    cmask = cidx * CMP_STRIDE + CMP_LEN <= qpos_of((nq, n_c)) + 1
    s = lax.dot_general(q4, kc_ref[...].astype(bf16), nt, preferred_element_type=f32) * SCALE_N
    s = jnp.where(cmask, s, NEG)
    m = jnp.max(s, axis=-1, keepdims=True)
    e = jnp.where(cmask, jnp.exp(s - m), 0.0)
    den = jnp.sum(e, axis=-1, keepdims=True)
    p = e / jnp.where(den > 0.0, den, 1.0)
    o_cmp = jnp.dot(p.astype(bf16), vc_ref[...].astype(bf16), preferred_element_type=f32)
    p_sum = p[0:tq]
    for j in range(1, J_N):
        p_sum = p_sum + p[j * tq:(j + 1) * tq]
    ci = lax.broadcasted_iota(jnp.int32, (n_c, LANES), 0)
    si = lax.broadcasted_iota(jnp.int32, (n_c, LANES), 1)
    cover = jnp.logical_and(ci * CMP_STRIDE < si * SEL_LEN + SEL_LEN,
                            ci * CMP_STRIDE + CMP_LEN > si * SEL_LEN).astype(bf16)
    imp = jnp.zeros((tq, LANES), f32)
    for part in _split3(p_sum):
        imp = imp + jnp.dot(part, cover, preferred_element_type=f32)
    blk = lax.broadcasted_iota(jnp.int32, (tq, LANES), 1)
    cur = (qi * tq + lax.broadcasted_iota(jnp.int32, (tq, LANES), 0)) // SEL_LEN
    valid = blk <= cur
    forced = jnp.logical_or(blk == 0, blk >= cur - 1)
    score = jnp.where(valid, imp + jnp.where(forced, BIG, 0.0), -BIG)
    score = jnp.where(blk < n_s, score, NEG_INF)
    rank = jnp.zeros((tq, LANES), jnp.int32)
    for s2 in range(n_s):
        col = score[:, s2:s2 + 1]
        ahead = jnp.logical_or(col > score, jnp.logical_and(col == score, blk > s2))
        rank = rank + ahead.astype(jnp.int32)
    sel = (rank < N_SEL).astype(bf16)

    ck = SEL_CHUNK
    s_row = lax.broadcasted_iota(jnp.int32, (LANES, ck), 0)
    k_lane = lax.broadcasted_iota(jnp.int32, (LANES, ck), 1)
    kpos1 = lax.broadcasted_iota(jnp.int32, (tq, ck), 1)
    qpos1 = qi * tq + lax.broadcasted_iota(jnp.int32, (tq, ck), 0)

    def kchunk(kb, carry):
        m_o, l_o, acc = carry
        k0 = pl.multiple_of(kb * ck, ck)
        expand = ((k0 + k_lane) // SEL_LEN == s_row).astype(bf16)
        picked = jnp.dot(sel, expand, preferred_element_type=f32)
        keep1 = jnp.logical_and(picked > 0.5, k0 + kpos1 <= qpos1)
        keep = jnp.concatenate([keep1] * J_N, axis=0)
        kk = ks_ref[pl.ds(k0, ck), :].astype(bf16)
        vv = vs_ref[pl.ds(k0, ck), :].astype(bf16)
        sc = lax.dot_general(q4, kk, nt, preferred_element_type=f32) * SCALE_N
        sc = jnp.where(keep, sc, NEG)
        m_n = jnp.maximum(m_o, jnp.max(sc, axis=-1, keepdims=True))
        alpha = jnp.exp(m_o - m_n)
        ee = jnp.where(keep, jnp.exp(sc - m_n), 0.0)
        l_n = alpha * l_o + jnp.sum(ee, axis=-1, keepdims=True)
        return m_n, l_n, alpha * acc + jnp.dot(ee.astype(bf16), vv, preferred_element_type=f32)

    nkc = (qi * tq + tq + ck - 1) // ck
    _, l_f, acc_f = lax.fori_loop(0, nkc, kchunk, (jnp.full((nq, 1), NEG, f32), jnp.zeros((nq, 1), f32),
                                                   jnp.zeros((nq, DH_N), f32)))
    o_sel = acc_f / l_f

    span = WINDOW + tq
    w0 = pl.multiple_of(jnp.clip(qi * tq - WINDOW, 0, t - span), tq)
    kw = kw_ref[pl.ds(w0, span), :].astype(bf16)
    vw = vw_ref[pl.ds(w0, span), :].astype(bf16)
    qpos_w = qpos_of((nq, span))
    kpos_w = w0 + lax.broadcasted_iota(jnp.int32, (nq, span), 1)
    keep_win = jnp.logical_and(kpos_w <= qpos_w, kpos_w > qpos_w - WINDOW)
    sw = lax.dot_general(q4, kw, nt, preferred_element_type=f32) * SCALE_N
    o_win = _softmax_pv(sw, keep_win, vw)

    gates = jax.nn.sigmoid(gn_ref[...])
    gl = lax.broadcasted_iota(jnp.int32, gates.shape, 1)
    outs = []
    for j in range(J_N):
        acc = jnp.zeros((tq, DH_N), f32)
        for which, o in enumerate((o_cmp, o_sel, o_win)):
            col = jnp.sum(jnp.where(gl == ZG_GN + (g * J_N + j) * 3 + which, gates, 0.0), axis=-1, keepdims=True)
            acc = acc + col * o[j * tq:(j + 1) * tq]
        outs.append(acc)
    o_ref[...] = jnp.concatenate(outs, axis=1).astype(o_ref.dtype)


def nsa_prompt(z3, zg3, kcv, tq=256):
    b, l, _ = z3.shape
    assert l >= WINDOW + tq and l % tq == 0 and l % SEL_CHUNK == 0
    nq = l // tq
    n_c = kcv.shape[3]
    full = lambda c0: pl.BlockSpec((None, l, DH_N), lambda bi, g, qi: (bi, 0, c0 // DH_N + g))
    return pl.pallas_call(
        functools.partial(_nsa_prompt_kernel, tq=tq),
        grid=(b, G_N, nq),
        in_specs=[pl.BlockSpec((None, tq, J_N * DH_N), lambda bi, g, qi: (bi, qi, Z_Q // (J_N * DH_N) + g)),
                  pl.BlockSpec((None, None, None, n_c, DH_N), lambda bi, g, qi: (bi, 0, g, 0, 0)),
                  pl.BlockSpec((None, None, None, n_c, DH_N), lambda bi, g, qi: (bi, 1, g, 0, 0)),
                  full(Z_KV + 2 * KVW), full(Z_KV + 3 * KVW), full(Z_WIN), full(Z_WIN + KVW),
                  pl.BlockSpec((None, tq, LANES), lambda bi, g, qi: (bi, qi, 0))],
        out_specs=pl.BlockSpec((None, tq, J_N * DH_N), lambda bi, g, qi: (bi, qi, g)),
        out_shape=jax.ShapeDtypeStruct((b, l, D_N), bf16),
        compiler_params=pltpu.CompilerParams(dimension_semantics=("parallel", "parallel", "arbitrary"),
                                             vmem_limit_bytes=VMEM_LIMIT_BYTES),
        name="nsa_prompt",
    )(z3, kcv, kcv, z3, z3, z3, z3, zg3)


def _paged_compress_kernel(pt_ref, *refs, page, nstep):
    pages = refs[:PAGES_PER_STEP]
    new_ref, pe_ref, w1_ref, w2_ref, o_ref, carry_ref = refs[PAGES_PER_STEP:]
    p = pl.program_id(1)
    nsg = 2 * G_N
    cpp = page // CMP_STRIDE
    nch = PAGES_PER_STEP * cpp
    row = lax.broadcasted_iota(jnp.int32, (nch, CMP_HID), 0)
    flat = [pg.reshape(page * nsg, DH_N) for pg in pages]

    @pl.when(p < nstep)
    def _():
        for s in range(2):
            first = jnp.zeros((G_N * nch, CMP_HID), jnp.float32)
            second = jnp.zeros((G_N * nch, CMP_HID), jnp.float32)
            for k in range(CMP_STRIDE // 2):
                xs = [jnp.concatenate([flat[r][pl.ds(l * nsg + s * G_N + g, cpp, stride=CMP_STRIDE * nsg), :]
                                       for g in range(G_N) for r in range(PAGES_PER_STEP)], axis=0)
                      for l in (2 * k, 2 * k + 1)]
                lhs = lambda off: jnp.concatenate(
                    [(xs[i] + pe_ref[s, off + 2 * k + i:off + 2 * k + i + 1, :]).astype(bf16) for i in range(2)], axis=1)
                first += jnp.dot(lhs(0), w1_ref[s, k], preferred_element_type=jnp.float32)
                second += jnp.dot(lhs(CMP_STRIDE), w1_ref[s, CMP_STRIDE // 2 + k], preferred_element_type=jnp.float32)
            for g in range(G_N):
                f_g = first[g * nch:(g + 1) * nch]
                s_g = second[g * nch:(g + 1) * nch]
                prev = carry_ref[s * G_N + g, 0:1, :]
                shifted = jnp.where(row == 0, prev, pltpu.roll(f_g, 1, axis=0))
                carry_ref[s * G_N + g, 0:1, :] = f_g[nch - 1:nch, :]
                hid = _gelu_tanh(shifted + s_g)
                hid = jnp.where(jnp.logical_and(row == 0, p == 0), 0.0, hid)
                o_ref[s, g] = jnp.dot(hid.astype(bf16), w2_ref[s].astype(bf16), preferred_element_type=jnp.float32)

    @pl.when(p == nstep)
    def _():
        for s in range(2):
            for g in range(G_N):
                c0 = (s * G_N + g) * DH_N
                sec = jnp.zeros((QROWS, CMP_HID), jnp.float32)
                for l in range(CMP_STRIDE):
                    x = jnp.broadcast_to(new_ref[l:l + 1, c0:c0 + DH_N] + pe_ref[s, CMP_STRIDE + l:CMP_STRIDE + l + 1, :],
                                         (QROWS, DH_N))
                    w_l = w1_ref[s, (CMP_STRIDE + l) // 2, (l % 2) * DH_N:(l % 2 + 1) * DH_N, :]
                    sec += jnp.dot(x.astype(bf16), w_l, preferred_element_type=jnp.float32)
                hid = _gelu_tanh(carry_ref[s * G_N + g, 0:1, :] + sec)
                blk = jnp.dot(hid.astype(bf16), w2_ref[s].astype(bf16), preferred_element_type=jnp.float32)
                rows = lax.broadcasted_iota(jnp.int32, (nch, DH_N), 0)
                o_ref[s, g] = jnp.where(rows == 0, jnp.broadcast_to(blk[0:1, :], (nch, DH_N)), 0.0)


def paged_compress(page_table, cache3, page, new_c, cmp_pe, w1_bf, cmp_w2):
    b, npages = page_table.shape
    assert page % CMP_STRIDE == 0 and npages % PAGES_PER_STEP == 0 and new_c.shape[1] == NEW_ROWS == CMP_STRIDE
    nstep = npages // PAGES_PER_STEP
    nch = PAGES_PER_STEP * page // CMP_STRIDE

    def page_spec(r):
        return pl.BlockSpec((page, 2 * G_N, DH_N),
                            lambda bi, p, pt: (pt[bi, jnp.minimum(p * PAGES_PER_STEP + r, npages - 1)], 0, 0))

    grid_spec = pltpu.PrefetchScalarGridSpec(
        num_scalar_prefetch=1,
        grid=(b, nstep + 1),
        in_specs=[page_spec(r) for r in range(PAGES_PER_STEP)] + [
            pl.BlockSpec((None, NEW_ROWS, 2 * KVW), lambda bi, p, pt: (bi, 0, 0)),
            pl.BlockSpec((2, CMP_LEN, DH_N), lambda bi, p, pt: (0, 0, 0)),
            pl.BlockSpec((2, CMP_LEN // 2, 2 * DH_N, CMP_HID), lambda bi, p, pt: (0, 0, 0, 0)),
            pl.BlockSpec((2, CMP_HID, DH_N), lambda bi, p, pt: (0, 0, 0))],
        out_specs=pl.BlockSpec((None, 2, G_N, nch, DH_N), lambda bi, p, pt: (bi, 0, 0, p, 0)),
        scratch_shapes=[pltpu.VMEM((2 * G_N, QROWS, CMP_HID), jnp.float32)])
    return pl.pallas_call(
        functools.partial(_paged_compress_kernel, page=page, nstep=nstep),
        grid_spec=grid_spec,
        out_shape=jax.ShapeDtypeStruct((b, 2, G_N, (nstep + 1) * nch, DH_N), jnp.float32),
        compiler_params=pltpu.CompilerParams(dimension_semantics=("parallel", "arbitrary"),
                                             vmem_limit_bytes=VMEM_LIMIT_BYTES),
        name="nsa_paged_compress",
    )(page_table, *([cache3] * PAGES_PER_STEP), new_c, cmp_pe, w1_bf, cmp_w2)


def _paged_attend_kernel(pt_ref, *refs, page, nstep, l_true):
    pages = refs[:PAGES_PER_STEP]
    (q_ref, kcv_ref, new_ref, winp_ref, wnew_ref, zg_ref, o_ref,
     sel_ref, ocmp_ref, m_ref, l_ref, acc_ref) = refs[PAGES_PER_STEP:]
    p = pl.program_id(1)
    past = nstep * PAGES_PER_STEP * page
    tk = PAGES_PER_STEP * page
    n_r = kcv_ref.shape[2]
    n_s = -(-(past + l_true) // SEL_LEN)
    sl = 2 * LANES
    assert n_s <= sl
    nq = J_N * QROWS
    f32 = jnp.float32
    nt = (((1,), (1,)), ((), ()))
    nsg = 2 * G_N
    wbuf = winp_ref.shape[0]
    flat = [pg.reshape(page * nsg, DH_N) for pg in pages]
    win_flat = winp_ref.reshape(wbuf * nsg, DH_N)

    def q_of(g):
        return jnp.concatenate([q_ref[:, (g * J_N + j) * DH_N:(g * J_N + j + 1) * DH_N] for j in range(J_N)],
                               axis=0).astype(bf16)

    def pos_of(shape):
        return lax.broadcasted_iota(jnp.int32, shape, 0) % QROWS

    def online(g, s, keep, v):
        s = jnp.where(keep, s, NEG)
        m_old = m_ref[g]
        m_new = jnp.maximum(m_old, jnp.max(s, axis=-1, keepdims=True))
        alpha = jnp.exp(m_old - m_new)
        e = jnp.where(keep, jnp.exp(s - m_new), 0.0)
        l_ref[g] = alpha * l_ref[g] + jnp.sum(e, axis=-1, keepdims=True)
        acc_ref[g] = alpha * acc_ref[g] + jnp.dot(e.astype(bf16), v, preferred_element_type=f32)
        m_ref[g] = m_new

    @pl.when(p == 0)
    def _():
        r_i = lax.broadcasted_iota(jnp.int32, (nq, n_r), 1)
        qpos = past + pos_of((nq, n_r))
        cmask = jnp.logical_and(r_i >= 1, (r_i - 1) * CMP_STRIDE + CMP_LEN <= qpos + 1)
        ci = lax.broadcasted_iota(jnp.int32, (n_r, sl), 0) - 1
        si = lax.broadcasted_iota(jnp.int32, (n_r, sl), 1)
        cover = jnp.logical_and(jnp.logical_and(ci >= 0, ci * CMP_STRIDE < si * SEL_LEN + SEL_LEN),
                                ci * CMP_STRIDE + CMP_LEN > si * SEL_LEN).astype(bf16)
        blk = lax.broadcasted_iota(jnp.int32, (QROWS, sl), 1)
        cur = (past + lax.broadcasted_iota(jnp.int32, (QROWS, sl), 0)) // SEL_LEN
        valid = blk <= cur
        forced = jnp.logical_or(blk == 0, blk >= cur - 1)
        s_src = lax.broadcasted_iota(jnp.int32, (sl, sl), 0)
        s_dst = lax.broadcasted_iota(jnp.int32, (sl, sl), 1)
        for g in range(G_N):
            q = q_of(g)
            kc = kcv_ref[0, g].astype(bf16)
            vc = kcv_ref[1, g].astype(bf16)
            s = lax.dot_general(q, kc, nt, preferred_element_type=f32) * SCALE_N
            s = jnp.where(cmask, s, NEG)
            m = jnp.max(s, axis=-1, keepdims=True)
            e = jnp.where(cmask, jnp.exp(s - m), 0.0)
            den = jnp.sum(e, axis=-1, keepdims=True)
            pr = e / jnp.where(den > 0.0, den, 1.0)
            ocmp_ref[g] = jnp.dot(pr.astype(bf16), vc, preferred_element_type=f32)
            p_sum = pr[0:QROWS]
            for j in range(1, J_N):
                p_sum = p_sum + pr[j * QROWS:(j + 1) * QROWS]
            imp = jnp.zeros((QROWS, sl), f32)
            for part in _split3(p_sum):
                imp = imp + jnp.dot(part, cover, preferred_element_type=f32)
            score = jnp.where(valid, imp + jnp.where(forced, BIG, 0.0), -BIG)
            score = jnp.where(blk < n_s, score, NEG_INF)
            score_t = score.T
            sel_rows = []
            for qi in range(QROWS):
                col = score_t[:, qi:qi + 1]
                rw = score[qi:qi + 1, :]
                ahead = jnp.logical_or(col > rw, jnp.logical_and(col == rw, s_src < s_dst))
                rank = jnp.sum(ahead.astype(jnp.int32), axis=0, keepdims=True)
                sel_rows.append((rank < N_SEL).astype(f32))
            sel_ref[g] = jnp.concatenate(sel_rows, axis=0)
            m_ref[g] = jnp.full((nq, 1), NEG, f32)
            l_ref[g] = jnp.zeros((nq, 1), f32)
            acc_ref[g] = jnp.zeros((nq, DH_N), f32)

    @pl.when(p < nstep)
    def _():
        k0 = p * tk
        expand = ((k0 + lax.broadcasted_iota(jnp.int32, (sl, tk), 1)) // SEL_LEN
                  == lax.broadcasted_iota(jnp.int32, (sl, tk), 0)).astype(bf16)
        for g in range(G_N):
            picked = jnp.dot(sel_ref[g].astype(bf16), expand, preferred_element_type=f32)
            keep = jnp.concatenate([picked] * J_N, axis=0) > 0.5
            kk = jnp.concatenate([pg[pl.ds(g, page, stride=nsg), :] for pg in flat], axis=0).astype(bf16)
            vv = jnp.concatenate([pg[pl.ds(G_N + g, page, stride=nsg), :] for pg in flat], axis=0).astype(bf16)
            s = lax.dot_general(q_of(g), kk, nt, preferred_element_type=f32) * SCALE_N
            online(g, s, keep, vv)

    @pl.when(p == nstep)
    def _():
        gates = jax.nn.sigmoid(zg_ref[...])
        gl = lax.broadcasted_iota(jnp.int32, gates.shape, 1)
        pos_n = pos_of((nq, NEW_ROWS))
        l_n = lax.broadcasted_iota(jnp.int32, (nq, NEW_ROWS), 1)
        keep_new = jnp.logical_and(l_n <= pos_n, l_n < l_true)
        pos_w = pos_of((nq, wbuf + NEW_ROWS))
        i_w = lax.broadcasted_iota(jnp.int32, (nq, wbuf + NEW_ROWS), 1)
        keep_win = jnp.logical_or(
            jnp.logical_and(i_w < wbuf, i_w - wbuf > pos_w - WINDOW),
            jnp.logical_and(i_w >= wbuf, jnp.logical_and(i_w - wbuf <= pos_w, i_w - wbuf < l_true)))
        for g in range(G_N):
            q = q_of(g)
            kn = new_ref[:, g * DH_N:(g + 1) * DH_N].astype(bf16)
            vn = new_ref[:, KVW + g * DH_N:KVW + (g + 1) * DH_N].astype(bf16)
            s = lax.dot_general(q, kn, nt, preferred_element_type=f32) * SCALE_N
            online(g, s, keep_new, vn)
            o_sel = acc_ref[g] / l_ref[g]
            kw = jnp.concatenate([win_flat[pl.ds(g, wbuf, stride=nsg), :], wnew_ref[:, g * DH_N:(g + 1) * DH_N]],
                                 axis=0).astype(bf16)
            vw = jnp.concatenate([win_flat[pl.ds(G_N + g, wbuf, stride=nsg), :],
                                  wnew_ref[:, KVW + g * DH_N:KVW + (g + 1) * DH_N]], axis=0).astype(bf16)
            s = lax.dot_general(q, kw, nt, preferred_element_type=f32) * SCALE_N
            o_win = _softmax_pv(s, keep_win, vw)
            o_cmp = ocmp_ref[g]
            for j in range(J_N):
                acc = jnp.zeros((QROWS, DH_N), f32)
                for which, o in enumerate((o_cmp, o_sel, o_win)):
                    col = jnp.sum(jnp.where(gl == ZG_GN + (g * J_N + j) * 3 + which, gates, 0.0), axis=-1, keepdims=True)
                    acc = acc + col * o[j * QROWS:(j + 1) * QROWS]
                o_ref[:, (g * J_N + j) * DH_N:(g * J_N + j + 1) * DH_N] = acc.astype(o_ref.dtype)


def paged_attend(page_table, cache3, page, q8, kcv, new_s, winp, wnew, zg8, l_true):
    b, npages = page_table.shape
    nstep = npages // PAGES_PER_STEP
    assert winp.shape[0] == b * WINDOW and l_true <= QROWS and npages % PAGES_PER_STEP == 0
    n_r = kcv.shape[3]
    nq = J_N * QROWS

    def page_spec(r):
        return pl.BlockSpec((page, 2 * G_N, DH_N),
                            lambda bi, p, pt: (pt[bi, jnp.minimum(p * PAGES_PER_STEP + r, npages - 1)], 1, 0))

    per_b = lambda rows, cols: pl.BlockSpec((None, rows, cols), lambda bi, p, pt: (bi, 0, 0))
    grid_spec = pltpu.PrefetchScalarGridSpec(
        num_scalar_prefetch=1,
        grid=(b, nstep + 1),
        in_specs=[page_spec(r) for r in range(PAGES_PER_STEP)] + [
            per_b(QROWS, D_N),
            pl.BlockSpec((None, 2, G_N, n_r, DH_N), lambda bi, p, pt: (bi, 0, 0, 0, 0)),
            per_b(NEW_ROWS, 2 * KVW), pl.BlockSpec((WINDOW, 2 * G_N, DH_N), lambda bi, p, pt: (bi, 0, 0)),
            per_b(NEW_ROWS, 2 * KVW), per_b(QROWS, LANES)],
        out_specs=per_b(QROWS, D_N),
        scratch_shapes=[pltpu.VMEM((G_N, QROWS, 2 * LANES), jnp.float32),
                        pltpu.VMEM((G_N, nq, DH_N), jnp.float32),
                        pltpu.VMEM((G_N, nq, 1), jnp.float32),
                        pltpu.VMEM((G_N, nq, 1), jnp.float32),
                        pltpu.VMEM((G_N, nq, DH_N), jnp.float32)])
    return pl.pallas_call(
        functools.partial(_paged_attend_kernel, page=page, nstep=nstep, l_true=l_true),
        grid_spec=grid_spec,
        out_shape=jax.ShapeDtypeStruct((b, QROWS, D_N), jnp.float32),
        compiler_params=pltpu.CompilerParams(dimension_semantics=("parallel", "arbitrary"),
                                             vmem_limit_bytes=VMEM_LIMIT_BYTES),
        name="nsa_paged_attend",
    )(page_table, *([cache3] * PAGES_PER_STEP), q8, kcv, new_s, winp, wnew, zg8)


def _extract_top(s, k):
    rows = lax.broadcasted_iota(jnp.int32, s.shape, 0)
    nrow = s.shape[0]
    work = s
    taken = jnp.zeros(s.shape, jnp.bool_)
    tops = []
    for _ in range(k):
        m = jnp.max(work, axis=0, keepdims=True)
        first = jnp.min(jnp.where(work == m, rows, nrow), axis=0, keepdims=True)
        hit = rows == first
        taken = jnp.logical_or(taken, hit)
        work = jnp.where(hit, NEG_INF, work)
        tops.append(m)
    return tops, taken


def _peer_route_kernel(qt_ref, keys_ref, s1_ref, e1_ref, s2_ref, e2_ref, tau_ref):
    halves = []
    for c in range(2):
        q = qt_ref[c * PEER_QHALF:(c + 1) * PEER_QHALF, :].astype(bf16)
        s = jnp.dot(keys_ref[c].astype(bf16), q, preferred_element_type=jnp.float32)
        tops, taken = _extract_top(s, PEER_TOPK)
        halves.append((s, tops, taken))
    (sa, ta, ma), (sb, tb, mb) = halves
    assert PEER_TOPK == 16
    ta_col = jnp.concatenate(ta, axis=0)
    tb_col = jnp.concatenate(tb, axis=0)
    a_idx = lax.broadcasted_iota(jnp.int32, (8, ta_col.shape[1]), 0)
    cand = jnp.concatenate(
        [ta[0] + tb_col, ta[1] + tb_col, ta[2] + tb_col[0:8], ta[3] + tb_col[0:8]]
        + [jnp.where(a_idx >= 4, ta_col[0:8] + tb[b], NEG_INF) for b in range(3)]
        + [ta_col[8:16] + tb[0]], axis=0)
    ctops, _ = _extract_top(cand, PEER_TOPK)
    cmax = ctops[0]
    z = jnp.exp(ctops[0] - cmax)
    for r in range(1, PEER_TOPK):
        z = z + jnp.exp(ctops[r] - cmax)
    tau_ref[...] = ctops[PEER_TOPK - 1]
    s1_ref[...] = jnp.where(ma, sa, NEG_INF)
    s2_ref[...] = jnp.where(mb, sb, NEG_INF)
    e1_ref[...] = jnp.where(ma, jnp.exp(sa - ta[0]), 0.0) / z
    e2_ref[...] = jnp.where(mb, jnp.exp(sb - tb[0]), 0.0)


def peer_route(qt, pkeys, tb):
    n = qt.shape[1]
    big = jax.ShapeDtypeStruct((PEER_HEADS, N_KEYS, n), jnp.float32)
    bspec = pl.BlockSpec((None, N_KEYS, tb), lambda i, h: (h, 0, i))
    return pl.pallas_call(
        _peer_route_kernel,
        grid=(n // tb, PEER_HEADS),
        in_specs=[pl.BlockSpec((2 * PEER_QHALF, tb), lambda i, h: (h, i)),
                  pl.BlockSpec((None, 2, N_KEYS, PEER_QHALF), lambda i, h: (h, 0, 0, 0))],
        out_specs=[bspec, bspec, bspec, bspec, pl.BlockSpec((None, 1, tb), lambda i, h: (h, 0, i))],
        out_shape=[big, big, big, big, jax.ShapeDtypeStruct((PEER_HEADS, 1, n), jnp.float32)],
        compiler_params=pltpu.CompilerParams(dimension_semantics=("parallel", "parallel")),
        name="peer_route",
    )(qt, pkeys)


def _peer_dense_kernel(xt_ref, u_ref, vt_ref, s1_ref, e1_ref, s2_ref, e2_ref, tau_ref, o_ref, *, sub):
    j = pl.program_id(1)

    @pl.when(j == 0)
    def _():
        o_ref[...] = jnp.zeros_like(o_ref)

    ht = jnp.dot(u_ref[...], xt_ref[...], preferred_element_type=jnp.float32)
    acts = []
    for a in range(sub):
        i1 = j * sub + a
        g = None
        for h in range(PEER_HEADS):
            s1row = s1_ref[h, pl.ds(i1, 1), :]
            e1row = e1_ref[h, pl.ds(i1, 1), :]
            c = s2_ref[h] + s1row
            t = jnp.where(c >= tau_ref[h], e2_ref[h], 0.0) * e1row
            g = t if g is None else g + t
        acts.append((_gelu_tanh(ht[a * N_KEYS:(a + 1) * N_KEYS, :]) * g).astype(bf16))
    act = jnp.concatenate(acts, axis=0) if sub > 1 else acts[0]
    o_ref[...] += jnp.dot(vt_ref[...], act, preferred_element_type=jnp.float32)


def peer_dense(xt, u_bf, vt_bf, s1, e1, s2, e2, tau, tb, sub):
    d, n = xt.shape
    e = u_bf.shape[0]
    te = sub * N_KEYS
    once = dict(pipeline_mode=pl.Buffered(1))
    rspec = pl.BlockSpec((PEER_HEADS, N_KEYS, tb), lambda i, j: (0, 0, i), **once)
    return pl.pallas_call(
        functools.partial(_peer_dense_kernel, sub=sub),
        grid=(n // tb, e // te),
        in_specs=[pl.BlockSpec((d, tb), lambda i, j: (0, i), **once),
                  pl.BlockSpec((te, d), lambda i, j: (j, 0)),
                  pl.BlockSpec((d, te), lambda i, j: (0, j)),
                  rspec, rspec, rspec, rspec,
                  pl.BlockSpec((PEER_HEADS, 1, tb), lambda i, j: (0, 0, i), **once)],
        out_specs=pl.BlockSpec((d, tb), lambda i, j: (0, i)),
        out_shape=jax.ShapeDtypeStruct((d, n), jnp.float32),
        compiler_params=pltpu.CompilerParams(dimension_semantics=("parallel", "arbitrary"),
                                             vmem_limit_bytes=BIG_VMEM_LIMIT_BYTES),
        name="peer_dense",
    )(xt, u_bf, vt_bf, s1, e1, s2, e2, tau)


def peer_ffn_t(xt, w_pq_t, pkeys, u_bf, vt_bf):
    n = xt.shape[1]
    qt = pmm(w_pq_t, xt)
    s1, e1, s2, e2, tau = peer_route(qt, pkeys, _pick(n, (256, 128)))
    return peer_dense(xt, u_bf, vt_bf, s1, e1, s2, e2, tau, _pick(n, (512, 256, 128)), PEER_SUB)


def _in_proj_weights(w_in):
    o = _IN_OFFS
    w_main = jnp.concatenate([w_in[:, o[0]:o[3]], w_in[:, o[5]:o[8]], w_in[:, o[9]:o[11]]], axis=1).astype(bf16)
    w_gate = jnp.concatenate([w_in[:, o[3]:o[5]], w_in[:, o[8]:o[9]],
                              jnp.zeros((w_in.shape[0], LANES - 2 * NH_M - 3 * H_N), w_in.dtype)], axis=1).astype(bf16)
    return w_main, w_gate


def kernel(x_prompt, x_sample, cache_kv, state_win_kv, state_conv, state_C, state_n, state_m, page_table,
           norm_mix, norm_ffn, norm_final, w_in, w_conv, b_conv, w_bd, b_gates, w_hnorm, w_skip,
           cmp_pe, cmp_w1, cmp_w2, w_br_m, w_br_n, w_out, w_pq, peer_keys, peer_u, peer_v):
    assert w_in.shape[0] == DEPTH == 1
    f32 = jnp.float32
    l = 0
    Bp, Lp_, _ = x_prompt.shape
    Bs, Ls, _ = x_sample.shape
    w_main, w_gate = _in_proj_weights(w_in[l])
    wbd_full = expand_blockdiag(w_bd[l])
    gate_bias = jnp.concatenate([b_gates[l, 0], b_gates[l, 1], jnp.zeros((LANES - 2 * NH_M,), f32)])[None]
    w_br_m_bf, w_br_n_bf, w_out_bf = w_br_m[l].astype(bf16), w_br_n[l].astype(bf16), w_out[l].astype(bf16)
    w_pq_t = w_pq[l].T.astype(bf16)
    u_bf = peer_u[l].astype(bf16)
    vt_bf = peer_v[l].T.astype(bf16)

    def mixers(x, conv_prev, C0, n0, m0, nsa_fn, ch, lpad):
        B, L, _ = x.shape
        n = B * L
        x2 = x.reshape(n, D_MODEL)
        xn = rmsnorm_rows(x2, norm_mix[l], bf16)
        z2 = pmm(xn, w_main)
        zg2 = pmm(xn, w_gate)
        z3 = z2.reshape(B, L, Z_COLS)
        zg3 = zg2.reshape(B, L, LANES)
        xm = z3[:, :, Z_XM:Z_XM + D_M]
        new_conv = jnp.concatenate([conv_prev, xm], axis=1)[:, L:]
        prev8 = jnp.pad(conv_prev, ((0, 0), (PREV_ROWS - (CONV_W - 1), 0), (0, 0)))
        if lpad == L:
            zm3, zgm3, cols = z3, zg3, (Z_XM // DH_M, Z_VM // DH_M, Z_OM // DH_M)
        else:
            zm3 = jnp.pad(z3[:, :, :Z_Q], ((0, 0), (0, lpad - L), (0, 0)))
            zgm3 = jnp.pad(zg3, ((0, 0), (0, lpad - L), (0, 0)))
            cols = (Z_XM // DH_M, Z_VM // DH_M, Z_OM // DH_M)
        y_m, C1, n1, m1 = mlstm_branch(zm3, zgm3, cols, prev8, w_conv[l], b_conv[l], wbd_full, gate_bias,
                                       C0, n0[:, :, None, :], jnp.broadcast_to(m0[:, :, None, None], (B, NH_M, 1, LANES)),
                                       w_hnorm[l], w_skip[l], L, ch)
        y_m = y_m[:, :L].reshape(n, D_M)
        kv_new = z3[:, :, Z_KV:Z_KV + N_KV_SETS * KVW].reshape(B, L, N_KV_SETS, G_N, DH_N)
        win_new = z3[:, :, Z_WIN:Z_WIN + 2 * KVW].reshape(B, L, 2, G_N, DH_N)
        y_n, new_win = nsa_fn(z3, zg3, kv_new, win_new)
        merged = merge_branches(y_m, y_n.reshape(n, D_N).astype(bf16), w_br_m_bf, w_br_n_bf, z2)
        h2 = out_proj_residual(merged, w_out_bf, x2)
        return h2, kv_new, new_win, new_conv, C1, n1[:, :, 0], m1[:, :, 0, 0]

    def nsa_p(z3, zg3, kv_new, win_new):
        kcv = compress_prompt(z3, Z_KV // DH_N, cmp_pe[l], cmp_w1[l], cmp_w2[l])
        return nsa_prompt(z3, zg3, kcv), win_new[:, -min(WINDOW, z3.shape[1]):]

    def nsa_s(z3, zg3, kv_new, win_new):
        L = z3.shape[1]
        padr = lambda t, rows: jnp.pad(t, ((0, 0), (0, rows - t.shape[1]), (0, 0)))
        page = cache_kv.shape[2]
        cache3 = cache_kv.reshape(DEPTH * cache_kv.shape[1] * page, N_KV_SETS * G_N, DH_N)
        wbuf = state_win_kv.shape[2]
        win3 = state_win_kv.reshape(Bs * wbuf, 2 * G_N, DH_N)
        kcv = paged_compress(page_table, cache3, page, padr(z3[:, :, Z_KV:Z_KV + 2 * KVW], NEW_ROWS),
                             cmp_pe[l], cmp_w1[l].astype(bf16).reshape(2, CMP_LEN // 2, 2 * DH_N, CMP_HID), cmp_w2[l])
        y8 = paged_attend(page_table, cache3, page, padr(z3[:, :, Z_Q:Z_Q + D_N], QROWS), kcv,
                          padr(z3[:, :, Z_KV + 2 * KVW:Z_KV + 4 * KVW], NEW_ROWS),
                          win3, padr(z3[:, :, Z_WIN:Z_WIN + 2 * KVW], NEW_ROWS), padr(zg3, QROWS), L)
        new_win = jnp.concatenate([state_win_kv.reshape(Bs, wbuf, 2, G_N, DH_N)[:, L:], win_new], axis=1)
        return y8[:, :L], new_win

    hp, kvp, winp, convp, Cp, n_p, m_p = mixers(
        x_prompt, jnp.zeros((Bp, CONV_W - 1, D_M), f32), jnp.zeros((Bp, NH_M, DH_M, DH_M), f32),
        jnp.zeros((Bp, NH_M, DH_M), f32), jnp.zeros((Bp, NH_M), f32), nsa_p, 256, Lp_)
    hs, kvs, wins, convs, Cs, n_s, m_s = mixers(
        x_sample, state_conv.reshape(state_conv.shape[1:]), state_C.reshape(state_C.shape[1:]),
        state_n.reshape(state_n.shape[1:]), state_m.reshape(state_m.shape[1:]), nsa_s, 16, 16)

    def ffn_and_norm(h2):
        xt = rmsnorm_rows_t(h2, norm_ffn[l], bf16)
        peer_t = peer_ffn_t(xt, w_pq_t, peer_keys[l], u_bf, vt_bf)
        return final_norm(h2, peer_t, norm_final)

    y_prompt = ffn_and_norm(hp).reshape(x_prompt.shape)
    y_sample = ffn_and_norm(hs).reshape(x_sample.shape)
    st = lambda t: t[None]
    return (y_prompt, y_sample, st(kvp), st(kvs), st(winp), st(wins), st(convp), st(convs),
            st(Cp), st(Cs), st(n_p), st(n_s), st(m_p), st(m_s))
```

```python
import functools

import jax
import jax.numpy as jnp
import numpy as np
from jax import lax
from jax.experimental import pallas as pl
from jax.experimental.pallas import tpu as pltpu

D_MODEL = 4096
DEPTH = 1
D_M = D_MODEL // 2
DH_M = 256
NH_M = D_M // DH_M
CONV_W = 4
QKV_BLOCK = 4
D_N = D_MODEL // 2
DH_N = 128
H_N = D_N // DH_N
G_N = 4
J_N = H_N // G_N
KVW = G_N * DH_N
N_KV_SETS = 4
CMP_STRIDE = 16
CMP_LEN = 2 * CMP_STRIDE
CMP_HID = 2 * DH_N
SEL_LEN = 64
N_SEL = 16
WINDOW = 512
SCALE_N = DH_N ** -0.5
PEER_HEADS = 8
N_KEYS = 128
PEER_TOPK = 16
PEER_QDIM = 256
PEER_QHALF = PEER_QDIM // 2
IN_SIZES = (D_M, D_M, D_M, NH_M, NH_M, D_N, N_KV_SETS * KVW, 2 * KVW, 3 * H_N, D_MODEL, D_MODEL)
EPS = 1e-6
NEG = -1e30
BIG = 1e9
NEG_INF = float('-inf')

LANES = 128
VMEM_LIMIT_BYTES = 48 * 1024 * 1024
BIG_VMEM_LIMIT_BYTES = 56 * 1024 * 1024
PEER_SUB = 4
PREV_ROWS = 8
PAGES_PER_STEP = 16
NEW_ROWS = 16
QROWS = 8
SEL_CHUNK = 512
bf16 = jnp.bfloat16

_IN_OFFS = np.concatenate([[0], np.cumsum(IN_SIZES)]).tolist()
Z_XM, Z_VM, Z_OM = 0, D_M, 2 * D_M
Z_Q = 3 * D_M
Z_KV = Z_Q + D_N
Z_WIN = Z_KV + N_KV_SETS * KVW
Z_GM = Z_WIN + 2 * KVW
Z_GN = Z_GM + D_MODEL
Z_COLS = Z_GN + D_MODEL
ZG_IG, ZG_FG, ZG_GN = 0, NH_M, 2 * NH_M


def _pick(n, cands):
    for c in cands:
        if n % c == 0:
            return c
    return n


def _gelu_tanh(x):
    return 0.5 * x * (1.0 + jnp.tanh(np.float32(np.sqrt(2.0 / np.pi)) * (x + np.float32(0.044715) * (x * x * x))))


def _split3(x):
    hi = x.astype(bf16)
    r = x - hi.astype(jnp.float32)
    mid = r.astype(bf16)
    lo = (r - mid.astype(jnp.float32)).astype(bf16)
    return hi, mid, lo


def _rmsnorm_kernel(x_ref, g_ref, o_ref):
    x = x_ref[...]
    o_ref[...] = (x * lax.rsqrt(jnp.mean(x * x, axis=-1, keepdims=True) + EPS) * g_ref[...]).astype(o_ref.dtype)


def rmsnorm_rows(x2, g, out_dtype):
    n, d = x2.shape
    tr = _pick(n, (256, 128))
    return pl.pallas_call(
        _rmsnorm_kernel,
        grid=(n // tr,),
        in_specs=[pl.BlockSpec((tr, d), lambda i: (i, 0)), pl.BlockSpec((1, d), lambda i: (0, 0))],
        out_specs=pl.BlockSpec((tr, d), lambda i: (i, 0)),
        out_shape=jax.ShapeDtypeStruct((n, d), out_dtype),
        compiler_params=pltpu.CompilerParams(dimension_semantics=("parallel",), vmem_limit_bytes=VMEM_LIMIT_BYTES),
        name="rmsnorm",
    )(x2, g.reshape(1, d))


def _rmsnorm_t_kernel(x_ref, g_ref, o_ref):
    x = x_ref[...]
    o_ref[...] = (x * lax.rsqrt(jnp.mean(x * x, axis=-1, keepdims=True) + EPS) * g_ref[...]).T.astype(o_ref.dtype)


def rmsnorm_rows_t(x2, g, out_dtype):
    n, d = x2.shape
    tr = _pick(n, (256, 128))
    return pl.pallas_call(
        _rmsnorm_t_kernel,
        grid=(n // tr,),
        in_specs=[pl.BlockSpec((tr, d), lambda i: (i, 0)), pl.BlockSpec((1, d), lambda i: (0, 0))],
        out_specs=pl.BlockSpec((d, tr), lambda i: (0, i)),
        out_shape=jax.ShapeDtypeStruct((d, n), out_dtype),
        compiler_params=pltpu.CompilerParams(dimension_semantics=("parallel",), vmem_limit_bytes=VMEM_LIMIT_BYTES),
        name="rmsnorm_t",
    )(x2, g.reshape(1, d))


def _mm_kernel(a_ref, b_ref, o_ref):
    o_ref[...] = jnp.dot(a_ref[...], b_ref[...], preferred_element_type=jnp.float32).astype(o_ref.dtype)


def pmm(a, b, out_dtype=jnp.float32):
    m, kd = a.shape
    n = b.shape[1]
    tm = _pick(m, (1024, 512, 256, 128))
    tn = _pick(n, (1024, 512, 256, 128))
    return pl.pallas_call(
        _mm_kernel,
        grid=(m // tm, n // tn),
        in_specs=[pl.BlockSpec((tm, kd), lambda i, j: (i, 0)), pl.BlockSpec((kd, tn), lambda i, j: (0, j))],
        out_specs=pl.BlockSpec((tm, tn), lambda i, j: (i, j)),
        out_shape=jax.ShapeDtypeStruct((m, n), out_dtype),
        compiler_params=pltpu.CompilerParams(dimension_semantics=("parallel", "parallel"),
                                             vmem_limit_bytes=BIG_VMEM_LIMIT_BYTES),
        name="proj",
    )(a, b)


def _merge_kernel(ym_ref, yn_ref, wm_ref, wn_ref, gm_ref, gn_ref, o_ref):
    pm = jnp.dot(ym_ref[...], wm_ref[...], preferred_element_type=jnp.float32)
    pn = jnp.dot(yn_ref[...], wn_ref[...], preferred_element_type=jnp.float32)
    o_ref[...] = (jax.nn.sigmoid(gm_ref[...]) * pm + jax.nn.sigmoid(gn_ref[...]) * pn).astype(o_ref.dtype)


def merge_branches(ym, yn, wm, wn, z2):
    m = ym.shape[0]
    n = wm.shape[1]
    tm = _pick(m, (1024, 512, 256, 128))
    tn = 512
    return pl.pallas_call(
        _merge_kernel,
        grid=(m // tm, n // tn),
        in_specs=[pl.BlockSpec((tm, D_M), lambda i, j: (i, 0)), pl.BlockSpec((tm, D_N), lambda i, j: (i, 0)),
                  pl.BlockSpec((D_M, tn), lambda i, j: (0, j)), pl.BlockSpec((D_N, tn), lambda i, j: (0, j)),
                  pl.BlockSpec((tm, tn), lambda i, j: (i, Z_GM // tn + j)),
                  pl.BlockSpec((tm, tn), lambda i, j: (i, Z_GN // tn + j))],
        out_specs=pl.BlockSpec((tm, tn), lambda i, j: (i, j)),
        out_shape=jax.ShapeDtypeStruct((m, n), bf16),
        compiler_params=pltpu.CompilerParams(dimension_semantics=("parallel", "parallel"),
                                             vmem_limit_bytes=VMEM_LIMIT_BYTES),
        name="merge_branches",
    )(ym, yn, wm, wn, z2, z2)


def _mm_res_kernel(a_ref, b_ref, r_ref, o_ref):
    o_ref[...] = r_ref[...] + jnp.dot(a_ref[...], b_ref[...], preferred_element_type=jnp.float32)


def out_proj_residual(a, b, r):
    m, kd = a.shape
    n = b.shape[1]
    tm = _pick(m, (1024, 512, 256, 128))
    tn = 512
    return pl.pallas_call(
        _mm_res_kernel,
        grid=(m // tm, n // tn),
        in_specs=[pl.BlockSpec((tm, kd), lambda i, j: (i, 0)), pl.BlockSpec((kd, tn), lambda i, j: (0, j)),
                  pl.BlockSpec((tm, tn), lambda i, j: (i, j))],
        out_specs=pl.BlockSpec((tm, tn), lambda i, j: (i, j)),
        out_shape=jax.ShapeDtypeStruct((m, n), jnp.float32),
        compiler_params=pltpu.CompilerParams(dimension_semantics=("parallel", "parallel"),
                                             vmem_limit_bytes=VMEM_LIMIT_BYTES),
        name="out_proj",
    )(a, b, r)


def _final_kernel(h_ref, pt_ref, g_ref, o_ref):
    x = h_ref[...] + pt_ref[...].T
    o_ref[...] = x * lax.rsqrt(jnp.mean(x * x, axis=-1, keepdims=True) + EPS) * g_ref[...]


def final_norm(h2, peer_t, g):
    n, d = h2.shape
    tr = _pick(n, (256, 128))
    return pl.pallas_call(
        _final_kernel,
        grid=(n // tr,),
        in_specs=[pl.BlockSpec((tr, d), lambda i: (i, 0)), pl.BlockSpec((d, tr), lambda i: (0, i)),
                  pl.BlockSpec((1, d), lambda i: (0, 0))],
        out_specs=pl.BlockSpec((tr, d), lambda i: (i, 0)),
        out_shape=jax.ShapeDtypeStruct((n, d), jnp.float32),
        compiler_params=pltpu.CompilerParams(dimension_semantics=("parallel",), vmem_limit_bytes=VMEM_LIMIT_BYTES),
        name="final_norm",
    )(h2, peer_t, g.reshape(1, d))


def _blockdiag_apply(x, w_ref, which):
    parts = [jnp.dot(x[:, hb * LANES:(hb + 1) * LANES].astype(bf16), w_ref[which, hb].astype(bf16),
                     preferred_element_type=jnp.float32) for hb in range(DH_M // LANES)]
    return jnp.concatenate(parts, axis=1)


def _mlstm_kernel(xm_ref, vm_ref, om_ref, zg_ref, prev_ref, wc_ref, bc_ref, wbd_ref, gb_ref, c0_ref, n0_ref, m0_ref,
                  hn_ref, sk_ref, y_ref, c1_ref, n1_ref, m1_ref, *, ch, l_true):
    h = pl.program_id(1)
    lp = xm_ref.shape[0]
    nchunk = lp // ch
    f32 = jnp.float32
    tri = lax.broadcasted_iota(jnp.int32, (ch, ch), 0) >= lax.broadcasted_iota(jnp.int32, (ch, ch), 1)
    tri_bf = tri.astype(bf16)
    lane_g = lax.broadcasted_iota(jnp.int32, (ch, LANES), 1)
    sub_g = lax.broadcasted_iota(jnp.int32, (LANES, ch), 0)

    def chunk(ci, carry):
        c_st, n_st, m_st = carry
        r0 = pl.multiple_of(ci * ch, ch)
        xm = xm_ref[pl.ds(r0, ch), :]
        before = xm_ref[pl.ds(pl.multiple_of(jnp.maximum(r0 - PREV_ROWS, 0), PREV_ROWS), PREV_ROWS), :]
        before = jnp.where(ci == 0, prev_ref[...], before)
        xe = jnp.concatenate([before, xm], axis=0)
        lo = PREV_ROWS - (CONV_W - 1)
        conv = bc_ref[...] + sum(xe[lo + w:lo + w + ch, :] * wc_ref[w:w + 1, :] for w in range(CONV_W))
        c = conv * jax.nn.sigmoid(conv)
        q = _blockdiag_apply(c, wbd_ref, 0)
        k = _blockdiag_apply(c, wbd_ref, 1) * (DH_M ** -0.5)
        v = _blockdiag_apply(vm_ref[pl.ds(r0, ch), :], wbd_ref, 2)
        zg = zg_ref[pl.ds(r0, ch), :] + gb_ref[...]
        live = (r0 + lax.broadcasted_iota(jnp.int32, (ch, LANES), 0)) < l_true
        a_all = jnp.where(live, zg, NEG_INF)
        lf_all = jnp.where(live, jnp.minimum(zg, 0.0) - jnp.log1p(jnp.exp(-jnp.abs(zg))), 0.0)
        b_all = jnp.zeros((ch, LANES), f32)
        for part in _split3(lf_all):
            b_all = b_all + jnp.dot(tri_bf, part, preferred_element_type=f32)
        a_col = jnp.sum(jnp.where(lane_g == ZG_IG + h, a_all, 0.0), axis=1, keepdims=True)
        b_col = jnp.sum(jnp.where(lane_g == ZG_FG + h, b_all, 0.0), axis=1, keepdims=True)
        a_row = jnp.sum(jnp.where(sub_g == ZG_IG + h, a_all.T, 0.0), axis=0, keepdims=True)
        b_row = jnp.sum(jnp.where(sub_g == ZG_FG + h, b_all.T, 0.0), axis=0, keepdims=True)
        dmat = jnp.where(tri, b_col - b_row + a_row, NEG_INF)
        inter = b_col + m_st
        m = jnp.maximum(inter, jnp.max(dmat, axis=1, keepdims=True))
        w_intra = jnp.exp(dmat - m)
        w_inter = jnp.exp(inter - m)
        qb = q.astype(bf16)
        kb = k.astype(bf16)
        vb = v.astype(bf16)
        qk = lax.dot_general(qb, kb, (((1,), (1,)), ((), ())), preferred_element_type=f32) * w_intra
        num = w_inter * jnp.dot(qb, c_st.astype(bf16), preferred_element_type=f32) \
            + jnp.dot(qk.astype(bf16), vb, preferred_element_type=f32)
        den = w_inter * jnp.sum(q * n_st, axis=1, keepdims=True) + jnp.sum(qk, axis=1, keepdims=True)
        hh = num / jnp.maximum(jnp.abs(den), jnp.exp(-m))
        m_end = m[ch - 1:ch, :]
        b_end = b_col[ch - 1:ch, :]
        w_end = jnp.exp(b_end - b_col + a_col - m_end)
        decay = jnp.exp(b_end + m_st - m_end)
        kw = k * w_end
        c_new = decay * c_st + lax.dot_general(kw.astype(bf16), vb, (((0,), (0,)), ((), ())),
                                               preferred_element_type=f32)
        n_new = decay * n_st + jnp.sum(kw, axis=0, keepdims=True)
        hg = hh * jax.nn.sigmoid(om_ref[pl.ds(r0, ch), :])
        mu = jnp.mean(hg, axis=1, keepdims=True)
        var = jnp.mean(jnp.square(hg - mu), axis=1, keepdims=True)
        y = (hg - mu) * lax.rsqrt(var + EPS) * hn_ref[...] + sk_ref[...] * c
        y_ref[pl.ds(r0, ch), :] = y.astype(y_ref.dtype)
        return c_new, n_new, m_end

    c_fin, n_fin, m_fin = lax.fori_loop(0, nchunk, chunk, (c0_ref[...], n0_ref[...], m0_ref[:, :1]))
    c1_ref[...] = c_fin
    n1_ref[...] = n_fin
    m1_ref[...] = jnp.broadcast_to(m_fin, m1_ref.shape)


def expand_blockdiag(w_bd):
    per = LANES // QKV_BLOCK
    w = w_bd.reshape(3, D_M // LANES, per, QKV_BLOCK, QKV_BLOCK)
    full = jnp.einsum('tbnio,nm->tbnimo', w, jnp.eye(per, dtype=w_bd.dtype))
    return full.reshape(3, D_M // LANES, LANES, LANES)


def mlstm_branch(z3, zg3, col_blocks, prev8, w_conv, b_conv, wbd_full, gate_bias, c0, n0, m0, w_hnorm, w_skip,
                 l_true, ch):
    b, lp, _ = z3.shape
    cx, cv, co = col_blocks
    kern = functools.partial(_mlstm_kernel, ch=ch, l_true=l_true)
    col = lambda c0_: pl.BlockSpec((None, lp, DH_M), lambda bi, h: (bi, 0, c0_ + h))
    vec = pl.BlockSpec((1, DH_M), lambda bi, h: (0, h))
    st = lambda r, c: pl.BlockSpec((None, None, r, c), lambda bi, h: (bi, h, 0, 0))
    return pl.pallas_call(
        kern,
        grid=(b, NH_M),
        in_specs=[col(cx), col(cv), col(co),
                  pl.BlockSpec((None, lp, LANES), lambda bi, h: (bi, 0, 0)),
                  pl.BlockSpec((None, PREV_ROWS, DH_M), lambda bi, h: (bi, 0, h)),
                  pl.BlockSpec((CONV_W, DH_M), lambda bi, h: (0, h)),
                  vec,
                  pl.BlockSpec((3, DH_M // LANES, LANES, LANES), lambda bi, h: (0, h, 0, 0)),
                  pl.BlockSpec((1, LANES), lambda bi, h: (0, 0)),
                  st(DH_M, DH_M), st(1, DH_M), st(1, LANES), vec, vec],
        out_specs=[pl.BlockSpec((None, lp, DH_M), lambda bi, h: (bi, 0, h)),
                   st(DH_M, DH_M), st(1, DH_M), st(1, LANES)],
        out_shape=[jax.ShapeDtypeStruct((b, lp, D_M), bf16),
                   jax.ShapeDtypeStruct((b, NH_M, DH_M, DH_M), jnp.float32),
                   jax.ShapeDtypeStruct((b, NH_M, 1, DH_M), jnp.float32),
                   jax.ShapeDtypeStruct((b, NH_M, 1, LANES), jnp.float32)],
        compiler_params=pltpu.CompilerParams(dimension_semantics=("parallel", "parallel"),
                                             vmem_limit_bytes=VMEM_LIMIT_BYTES),
        name="mlstm_branch",
    )(z3, z3, z3, zg3, prev8, w_conv, b_conv.reshape(1, D_M), wbd_full, gate_bias, c0, n0, m0,
      w_hnorm.reshape(1, D_M), w_skip.reshape(1, D_M))


def _compress_kernel(x_ref, pe_ref, w1_ref, w2_ref, o_ref):
    nchunk = x_ref.shape[0] // CMP_STRIDE
    first = jnp.zeros((nchunk, CMP_HID), jnp.float32)
    second = jnp.zeros((nchunk, CMP_HID), jnp.float32)
    for l in range(CMP_STRIDE):
        xl = x_ref[pl.ds(l, nchunk, stride=CMP_STRIDE), :]
        first += jnp.dot((xl + pe_ref[l:l + 1, :]).astype(bf16), w1_ref[l].astype(bf16),
                         preferred_element_type=jnp.float32)
        second += jnp.dot((xl + pe_ref[CMP_STRIDE + l:CMP_STRIDE + l + 1, :]).astype(bf16),
                          w1_ref[CMP_STRIDE + l].astype(bf16), preferred_element_type=jnp.float32)
    nxt = jnp.concatenate([second[1:], jnp.zeros((1, CMP_HID), jnp.float32)], axis=0)
    row = lax.broadcasted_iota(jnp.int32, (nchunk, CMP_HID), 0)
    hid = jnp.where(row < nchunk - 1, _gelu_tanh(first + nxt), 0.0)
    o_ref[...] = jnp.dot(hid.astype(bf16), w2_ref[...].astype(bf16), preferred_element_type=jnp.float32)


def compress_prompt(z3, col0, cmp_pe, cmp_w1, cmp_w2):
    b, l, _ = z3.shape
    nchunk = l // CMP_STRIDE
    return pl.pallas_call(
        _compress_kernel,
        grid=(b, 2, G_N),
        in_specs=[pl.BlockSpec((None, l, DH_N), lambda bi, s, g: (bi, 0, col0 + s * G_N + g)),
                  pl.BlockSpec((None, CMP_LEN, DH_N), lambda bi, s, g: (s, 0, 0)),
                  pl.BlockSpec((None, CMP_LEN, DH_N, CMP_HID), lambda bi, s, g: (s, 0, 0, 0)),
                  pl.BlockSpec((None, CMP_HID, DH_N), lambda bi, s, g: (s, 0, 0))],
        out_specs=pl.BlockSpec((None, None, None, nchunk, DH_N), lambda bi, s, g: (bi, s, g, 0, 0)),
        out_shape=jax.ShapeDtypeStruct((b, 2, G_N, nchunk, DH_N), jnp.float32),
        compiler_params=pltpu.CompilerParams(dimension_semantics=("parallel", "parallel", "parallel"),
                                             vmem_limit_bytes=VMEM_LIMIT_BYTES),
        name="nsa_compress",
    )(z3, cmp_pe, cmp_w1, cmp_w2)


def _softmax_pv(s, keep, v):
    s = jnp.where(keep, s, NEG)
    m = jnp.max(s, axis=-1, keepdims=True)
    e = jnp.exp(s - m)
    den = jnp.sum(e, axis=-1, keepdims=True)
    return jnp.dot(e.astype(bf16), v, preferred_element_type=jnp.float32) / den


def _nsa_prompt_kernel(q_ref, kc_ref, vc_ref, ks_ref, vs_ref, kw_ref, vw_ref, gn_ref, o_ref, *, tq):
    g = pl.program_id(1)
    qi = pl.program_id(2)
    f32 = jnp.float32
    t = ks_ref.shape[0]
    n_c = kc_ref.shape[0]
    n_s = t // SEL_LEN
    nq = J_N * tq
    q4 = jnp.concatenate([q_ref[:, j * DH_N:(j + 1) * DH_N] for j in range(J_N)], axis=0).astype(bf16)
    nt = (((1,), (1,)), ((), ()))

    def qpos_of(shape):
        return qi * tq + lax.broadcasted_iota(jnp.int32, shape, 0) % tq

    cidx = lax.broadcasted_iota(jnp.int32, (nq, n_c), 1)
    cmask = cidx * CMP_STRIDE + CMP_LEN <= qpos_of((nq, n_c)) + 1
    s = lax.dot_general(q4, kc_ref[...].astype(bf16), nt, preferred_element_type=f32) * SCALE_N
    s = jnp.where(cmask, s, NEG)
    m = jnp.max(s, axis=-1, keepdims=True)
    e = jnp.where(cmask, jnp.exp(s - m), 0.0)
    den = jnp.sum(e, axis=-1, keepdims=True)
    p = e / jnp.where(den > 0.0, den, 1.0)
    o_cmp = jnp.dot(p.astype(bf16), vc_ref[...].astype(bf16), preferred_element_type=f32)
    p_sum = p[0:tq]
    for j in range(1, J_N):
        p_sum = p_sum + p[j * tq:(j + 1) * tq]
    ci = lax.broadcasted_iota(jnp.int32, (n_c, LANES), 0)
    si = lax.broadcasted_iota(jnp.int32, (n_c, LANES), 1)
    cover = jnp.logical_and(ci * CMP_STRIDE < si * SEL_LEN + SEL_LEN,
                            ci * CMP_STRIDE + CMP_LEN > si * SEL_LEN).astype(bf16)
    imp = jnp.zeros((tq, LANES), f32)
    for part in _split3(p_sum):
        imp = imp + jnp.dot(part, cover, preferred_element_type=f32)
    blk = lax.broadcasted_iota(jnp.int32, (tq, LANES), 1)
    cur = (qi * tq + lax.broadcasted_iota(jnp.int32, (tq, LANES), 0)) // SEL_LEN
    valid = blk <= cur
    forced = jnp.logical_or(blk == 0, blk >= cur - 1)
    score = jnp.where(valid, imp + jnp.where(forced, BIG, 0.0), -BIG)
    score = jnp.where(blk < n_s, score, NEG_INF)
    rank = jnp.zeros((tq, LANES), jnp.int32)
    for s2 in range(n_s):
        col = score[:, s2:s2 + 1]
        ahead = jnp.logical_or(col > score, jnp.logical_and(col == score, blk > s2))
        rank = rank + ahead.astype(jnp.int32)
    sel = (rank < N_SEL).astype(bf16)

    ck = SEL_CHUNK
    s_row = lax.broadcasted_iota(jnp.int32, (LANES, ck), 0)
    k_lane = lax.broadcasted_iota(jnp.int32, (LANES, ck), 1)
    kpos1 = lax.broadcasted_iota(jnp.int32, (tq, ck), 1)
    qpos1 = qi * tq + lax.broadcasted_iota(jnp.int32, (tq, ck), 0)

    def kchunk(kb, carry):
        m_o, l_o, acc = carry
        k0 = pl.multiple_of(kb * ck, ck)
        expand = ((k0 + k_lane) // SEL_LEN == s_row).astype(bf16)
        picked = jnp.dot(sel, expand, preferred_element_type=f32)
        keep1 = jnp.logical_and(picked > 0.5, k0 + kpos1 <= qpos1)
        keep = jnp.concatenate([keep1] * J_N, axis=0)
        kk = ks_ref[pl.ds(k0, ck), :].astype(bf16)
        vv = vs_ref[pl.ds(k0, ck), :].astype(bf16)
        sc = lax.dot_general(q4, kk, nt, preferred_element_type=f32) * SCALE_N
        sc = jnp.where(keep, sc, NEG)
        m_n = jnp.maximum(m_o, jnp.max(sc, axis=-1, keepdims=True))
        alpha = jnp.exp(m_o - m_n)
        ee = jnp.where(keep, jnp.exp(sc - m_n), 0.0)
        l_n = alpha * l_o + jnp.sum(ee, axis=-1, keepdims=True)
        return m_n, l_n, alpha * acc + jnp.dot(ee.astype(bf16), vv, preferred_element_type=f32)

    nkc = (qi * tq + tq + ck - 1) // ck
    _, l_f, acc_f = lax.fori_loop(0, nkc, kchunk, (jnp.full((nq, 1), NEG, f32), jnp.zeros((nq, 1), f32),
                                                   jnp.zeros((nq, DH_N), f32)))
    o_sel = acc_f / l_f

    span = WINDOW + tq
    w0 = pl.multiple_of(jnp.clip(qi * tq - WINDOW, 0, t - span), tq)
    kw = kw_ref[pl.ds(w0, span), :].astype(bf16)
    vw = vw_ref[pl.ds(w0, span), :].astype(bf16)
    qpos_w = qpos_of((nq, span))
    kpos_w = w0 + lax.broadcasted_iota(jnp.int32, (nq, span), 1)
    keep_win = jnp.logical_and(kpos_w <= qpos_w, kpos_w > qpos_w - WINDOW)
    sw = lax.dot_general(q4, kw, nt, preferred_element_type=f32) * SCALE_N
    o_win = _softmax_pv(sw, keep_win, vw)

    gates = jax.nn.sigmoid(gn_ref[...])
    gl = lax.broadcasted_iota(jnp.int32, gates.shape, 1)
    outs = []
    for j in range(J_N):
        acc = jnp.zeros((tq, DH_N), f32)
        for which, o in enumerate((o_cmp, o_sel, o_win)):
            col = jnp.sum(jnp.where(gl == ZG_GN + (g * J_N + j) * 3 + which, gates, 0.0), axis=-1, keepdims=True)
            acc = acc + col * o[j * tq:(j + 1) * tq]
        outs.append(acc)
    o_ref[...] = jnp.concatenate(outs, axis=1).astype(o_ref.dtype)


def nsa_prompt(z3, zg3, kcv, tq=256):
    b, l, _ = z3.shape
    assert l >= WINDOW + tq and l % tq == 0 and l % SEL_CHUNK == 0
    nq = l // tq
    n_c = kcv.shape[3]
    full = lambda c0: pl.BlockSpec((None, l, DH_N), lambda bi, g, qi: (bi, 0, c0 // DH_N + g))
    return pl.pallas_call(
        functools.partial(_nsa_prompt_kernel, tq=tq),
        grid=(b, G_N, nq),
        in_specs=[pl.BlockSpec((None, tq, J_N * DH_N), lambda bi, g, qi: (bi, qi, Z_Q // (J_N * DH_N) + g)),
                  pl.BlockSpec((None, None, None, n_c, DH_N), lambda bi, g, qi: (bi, 0, g, 0, 0)),
                  pl.BlockSpec((None, None, None, n_c, DH_N), lambda bi, g, qi: (bi, 1, g, 0, 0)),
                  full(Z_KV + 2 * KVW), full(Z_KV + 3 * KVW), full(Z_WIN), full(Z_WIN + KVW),
                  pl.BlockSpec((None, tq, LANES), lambda bi, g, qi: (bi, qi, 0))],
        out_specs=pl.BlockSpec((None, tq, J_N * DH_N), lambda bi, g, qi: (bi, qi, g)),
        out_shape=jax.ShapeDtypeStruct((b, l, D_N), bf16),
        compiler_params=pltpu.CompilerParams(dimension_semantics=("parallel", "parallel", "arbitrary"),
                                             vmem_limit_bytes=VMEM_LIMIT_BYTES),
        name="nsa_prompt",
    )(z3, kcv, kcv, z3, z3, z3, z3, zg3)


def _paged_compress_kernel(pt_ref, *refs, page, nstep):
    pages = refs[:PAGES_PER_STEP]
    new_ref, pe_ref, w1_ref, w2_ref, o_ref, carry_ref = refs[PAGES_PER_STEP:]
    p = pl.program_id(1)
    nsg = 2 * G_N
    cpp = page // CMP_STRIDE
    nch = PAGES_PER_STEP * cpp
    row = lax.broadcasted_iota(jnp.int32, (nch, CMP_HID), 0)
    flat = [pg.reshape(page * nsg, DH_N) for pg in pages]

    @pl.when(p == 0)
    def _():
        carry_ref[...] = jnp.zeros_like(carry_ref)

    @pl.when(p < nstep)
    def _():
        for s in range(2):
            first = jnp.zeros((G_N * nch, CMP_HID), jnp.float32)
            second = jnp.zeros((G_N * nch, CMP_HID), jnp.float32)
            for k in range(CMP_STRIDE // 2):
                xs = [jnp.concatenate([flat[r][pl.ds(l * nsg + s * G_N + g, cpp, stride=CMP_STRIDE * nsg), :]
                                       for g in range(G_N) for r in range(PAGES_PER_STEP)], axis=0)
                      for l in (2 * k, 2 * k + 1)]
                lhs = lambda off: jnp.concatenate(
                    [(xs[i] + pe_ref[s, off + 2 * k + i:off + 2 * k + i + 1, :]).astype(bf16) for i in range(2)], axis=1)
                first += jnp.dot(lhs(0), w1_ref[s, k], preferred_element_type=jnp.float32)
                second += jnp.dot(lhs(CMP_STRIDE), w1_ref[s, CMP_STRIDE // 2 + k], preferred_element_type=jnp.float32)
            for g in range(G_N):
                f_g = first[g * nch:(g + 1) * nch]
                s_g = second[g * nch:(g + 1) * nch]
                prev = carry_ref[s * G_N + g, 0:1, :]
                shifted = jnp.where(row == 0, prev, pltpu.roll(f_g, 1, axis=0))
                carry_ref[s * G_N + g, 0:1, :] = f_g[nch - 1:nch, :]
                hid = _gelu_tanh(shifted + s_g)
                hid = jnp.where(jnp.logical_and(row == 0, p == 0), 0.0, hid)
                o_ref[s, g] = jnp.dot(hid.astype(bf16), w2_ref[s].astype(bf16), preferred_element_type=jnp.float32)

    @pl.when(p == nstep)
    def _():
        for s in range(2):
            for g in range(G_N):
                c0 = (s * G_N + g) * DH_N
                sec = jnp.zeros((QROWS, CMP_HID), jnp.float32)
                for l in range(CMP_STRIDE):
                    x = jnp.broadcast_to(new_ref[l:l + 1, c0:c0 + DH_N] + pe_ref[s, CMP_STRIDE + l:CMP_STRIDE + l + 1, :],
                                         (QROWS, DH_N))
                    w_l = w1_ref[s, (CMP_STRIDE + l) // 2, (l % 2) * DH_N:(l % 2 + 1) * DH_N, :]
                    sec += jnp.dot(x.astype(bf16), w_l, preferred_element_type=jnp.float32)
                hid = _gelu_tanh(carry_ref[s * G_N + g, 0:1, :] + sec)
                blk = jnp.dot(hid.astype(bf16), w2_ref[s].astype(bf16), preferred_element_type=jnp.float32)
                rows = lax.broadcasted_iota(jnp.int32, (nch, DH_N), 0)
                o_ref[s, g] = jnp.where(rows == 0, jnp.broadcast_to(blk[0:1, :], (nch, DH_N)), 0.0)


def paged_compress(page_table, cache3, page, new_c, cmp_pe, w1_bf, cmp_w2):
    b, npages = page_table.shape
    assert page % CMP_STRIDE == 0 and npages % PAGES_PER_STEP == 0 and new_c.shape[1] == NEW_ROWS == CMP_STRIDE
    nstep = npages // PAGES_PER_STEP
    nch = PAGES_PER_STEP * page // CMP_STRIDE

    def page_spec(r):
        return pl.BlockSpec((page, 2 * G_N, DH_N),
                            lambda bi, p, pt: (pt[bi, jnp.minimum(p * PAGES_PER_STEP + r, npages - 1)], 0, 0))

    grid_spec = pltpu.PrefetchScalarGridSpec(
        num_scalar_prefetch=1,
        grid=(b, nstep + 1),
        in_specs=[page_spec(r) for r in range(PAGES_PER_STEP)] + [
            pl.BlockSpec((None, NEW_ROWS, 2 * KVW), lambda bi, p, pt: (bi, 0, 0)),
            pl.BlockSpec((2, CMP_LEN, DH_N), lambda bi, p, pt: (0, 0, 0)),
            pl.BlockSpec((2, CMP_LEN // 2, 2 * DH_N, CMP_HID), lambda bi, p, pt: (0, 0, 0, 0)),
            pl.BlockSpec((2, CMP_HID, DH_N), lambda bi, p, pt: (0, 0, 0))],
        out_specs=pl.BlockSpec((None, 2, G_N, nch, DH_N), lambda bi, p, pt: (bi, 0, 0, p, 0)),
        scratch_shapes=[pltpu.VMEM((2 * G_N, QROWS, CMP_HID), jnp.float32)])
    return pl.pallas_call(
        functools.partial(_paged_compress_kernel, page=page, nstep=nstep),
        grid_spec=grid_spec,
        out_shape=jax.ShapeDtypeStruct((b, 2, G_N, (nstep + 1) * nch, DH_N), jnp.float32),
        compiler_params=pltpu.CompilerParams(dimension_semantics=("parallel", "arbitrary"),
                                             vmem_limit_bytes=VMEM_LIMIT_BYTES),
        name="nsa_paged_compress",
    )(page_table, *([cache3] * PAGES_PER_STEP), new_c, cmp_pe, w1_bf, cmp_w2)


def _paged_attend_kernel(pt_ref, *refs, page, nstep, l_true):
    pages = refs[:PAGES_PER_STEP]
    (q_ref, kcv_ref, new_ref, winp_ref, wnew_ref, zg_ref, o_ref,
     sel_ref, ocmp_ref, m_ref, l_ref, acc_ref) = refs[PAGES_PER_STEP:]
    p = pl.program_id(1)
    past = nstep * PAGES_PER_STEP * page
    tk = PAGES_PER_STEP * page
    n_r = kcv_ref.shape[2]
    n_s = -(-(past + l_true) // SEL_LEN)
    sl = 2 * LANES
    assert n_s <= sl
    nq = J_N * QROWS
    f32 = jnp.float32
    nt = (((1,), (1,)), ((), ()))
    nsg = 2 * G_N
    wbuf = winp_ref.shape[0]
    flat = [pg.reshape(page * nsg, DH_N) for pg in pages]
    win_flat = winp_ref.reshape(wbuf * nsg, DH_N)

    def q_of(g):
        return jnp.concatenate([q_ref[:, (g * J_N + j) * DH_N:(g * J_N + j + 1) * DH_N] for j in range(J_N)],
                               axis=0).astype(bf16)

    def pos_of(shape):
        return lax.broadcasted_iota(jnp.int32, shape, 0) % QROWS

    def online(g, s, keep, v):
        s = jnp.where(keep, s, NEG)
        m_old = m_ref[g]
        m_new = jnp.maximum(m_old, jnp.max(s, axis=-1, keepdims=True))
        alpha = jnp.exp(m_old - m_new)
        e = jnp.where(keep, jnp.exp(s - m_new), 0.0)
        l_ref[g] = alpha * l_ref[g] + jnp.sum(e, axis=-1, keepdims=True)
        acc_ref[g] = alpha * acc_ref[g] + jnp.dot(e.astype(bf16), v, preferred_element_type=f32)
        m_ref[g] = m_new

    @pl.when(p == 0)
    def _():
        r_i = lax.broadcasted_iota(jnp.int32, (nq, n_r), 1)
        qpos = past + pos_of((nq, n_r))
        cmask = jnp.logical_and(r_i >= 1, (r_i - 1) * CMP_STRIDE + CMP_LEN <= qpos + 1)
        ci = lax.broadcasted_iota(jnp.int32, (n_r, sl), 0) - 1
        si = lax.broadcasted_iota(jnp.int32, (n_r, sl), 1)
        cover = jnp.logical_and(jnp.logical_and(ci >= 0, ci * CMP_STRIDE < si * SEL_LEN + SEL_LEN),
                                ci * CMP_STRIDE + CMP_LEN > si * SEL_LEN).astype(bf16)
        blk = lax.broadcasted_iota(jnp.int32, (QROWS, sl), 1)
        cur = (past + lax.broadcasted_iota(jnp.int32, (QROWS, sl), 0)) // SEL_LEN
        valid = blk <= cur
        forced = jnp.logical_or(blk == 0, blk >= cur - 1)
        s_src = lax.broadcasted_iota(jnp.int32, (sl, sl), 0)
        s_dst = lax.broadcasted_iota(jnp.int32, (sl, sl), 1)
        for g in range(G_N):
            q = q_of(g)
            kc = kcv_ref[0, g].astype(bf16)
            vc = kcv_ref[1, g].astype(bf16)
            s = lax.dot_general(q, kc, nt, preferred_element_type=f32) * SCALE_N
            s = jnp.where(cmask, s, NEG)
            m = jnp.max(s, axis=-1, keepdims=True)
            e = jnp.where(cmask, jnp.exp(s - m), 0.0)
            den = jnp.sum(e, axis=-1, keepdims=True)
            pr = e / jnp.where(den > 0.0, den, 1.0)
            ocmp_ref[g] = jnp.dot(pr.astype(bf16), vc, preferred_element_type=f32)
            p_sum = pr[0:QROWS]
            for j in range(1, J_N):
                p_sum = p_sum + pr[j * QROWS:(j + 1) * QROWS]
            imp = jnp.zeros((QROWS, sl), f32)
            for part in _split3(p_sum):
                imp = imp + jnp.dot(part, cover, preferred_element_type=f32)
            score = jnp.where(valid, imp + jnp.where(forced, BIG, 0.0), -BIG)
            score = jnp.where(blk < n_s, score, NEG_INF)
            score_t = score.T
            sel_rows = []
            for qi in range(QROWS):
                col = score_t[:, qi:qi + 1]
                rw = score[qi:qi + 1, :]
                ahead = jnp.logical_or(col > rw, jnp.logical_and(col == rw, s_src < s_dst))
                rank = jnp.sum(ahead.astype(jnp.int32), axis=0, keepdims=True)
                sel_rows.append((rank < N_SEL).astype(f32))
            sel_ref[g] = jnp.concatenate(sel_rows, axis=0)
            m_ref[g] = jnp.full((nq, 1), NEG, f32)
            l_ref[g] = jnp.zeros((nq, 1), f32)
            acc_ref[g] = jnp.zeros((nq, DH_N), f32)

    @pl.when(p < nstep)
    def _():
        k0 = p * tk
        expand = ((k0 + lax.broadcasted_iota(jnp.int32, (sl, tk), 1)) // SEL_LEN
                  == lax.broadcasted_iota(jnp.int32, (sl, tk), 0)).astype(bf16)
        for g in range(G_N):
            picked = jnp.dot(sel_ref[g].astype(bf16), expand, preferred_element_type=f32)
            keep = jnp.concatenate([picked] * J_N, axis=0) > 0.5
            kk = jnp.concatenate([pg[pl.ds(g, page, stride=nsg), :] for pg in flat], axis=0).astype(bf16)
            vv = jnp.concatenate([pg[pl.ds(G_N + g, page, stride=nsg), :] for pg in flat], axis=0).astype(bf16)
            s = lax.dot_general(q_of(g), kk, nt, preferred_element_type=f32) * SCALE_N
            online(g, s, keep, vv)

    @pl.when(p == nstep)
    def _():
        gates = jax.nn.sigmoid(zg_ref[...])
        gl = lax.broadcasted_iota(jnp.int32, gates.shape, 1)
        pos_n = pos_of((nq, NEW_ROWS))
        l_n = lax.broadcasted_iota(jnp.int32, (nq, NEW_ROWS), 1)
        keep_new = jnp.logical_and(l_n <= pos_n, l_n < l_true)
        pos_w = pos_of((nq, wbuf + NEW_ROWS))
        i_w = lax.broadcasted_iota(jnp.int32, (nq, wbuf + NEW_ROWS), 1)
        keep_win = jnp.logical_or(
            jnp.logical_and(i_w < wbuf, i_w - wbuf > pos_w - WINDOW),
            jnp.logical_and(i_w >= wbuf, jnp.logical_and(i_w - wbuf <= pos_w, i_w - wbuf < l_true)))
        for g in range(G_N):
            q = q_of(g)
            kn = new_ref[:, g * DH_N:(g + 1) * DH_N].astype(bf16)
            vn = new_ref[:, KVW + g * DH_N:KVW + (g + 1) * DH_N].astype(bf16)
            s = lax.dot_general(q, kn, nt, preferred_element_type=f32) * SCALE_N
            online(g, s, keep_new, vn)
            o_sel = acc_ref[g] / l_ref[g]
            kw = jnp.concatenate([win_flat[pl.ds(g, wbuf, stride=nsg), :], wnew_ref[:, g * DH_N:(g + 1) * DH_N]],
                                 axis=0).astype(bf16)
            vw = jnp.concatenate([win_flat[pl.ds(G_N + g, wbuf, stride=nsg), :],
                                  wnew_ref[:, KVW + g * DH_N:KVW + (g + 1) * DH_N]], axis=0).astype(bf16)
            s = lax.dot_general(q, kw, nt, preferred_element_type=f32) * SCALE_N
            o_win = _softmax_pv(s, keep_win, vw)
            o_cmp = ocmp_ref[g]
            for j in range(J_N):
                acc = jnp.zeros((QROWS, DH_N), f32)
                for which, o in enumerate((o_cmp, o_sel, o_win)):
                    col = jnp.sum(jnp.where(gl == ZG_GN + (g * J_N + j) * 3 + which, gates, 0.0), axis=-1, keepdims=True)
                    acc = acc + col * o[j * QROWS:(j + 1) * QROWS]
                o_ref[:, (g * J_N + j) * DH_N:(g * J_N + j + 1) * DH_N] = acc.astype(o_ref.dtype)


def paged_attend(page_table, cache3, page, q8, kcv, new_s, winp, wnew, zg8, l_true):
    b, npages = page_table.shape
    nstep = npages // PAGES_PER_STEP
    assert winp.shape[0] == b * WINDOW and l_true <= QROWS and npages % PAGES_PER_STEP == 0
    n_r = kcv.shape[3]
    nq = J_N * QROWS

    def page_spec(r):
        return pl.BlockSpec((page, 2 * G_N, DH_N),
                            lambda bi, p, pt: (pt[bi, jnp.minimum(p * PAGES_PER_STEP + r, npages - 1)], 1, 0))

    per_b = lambda rows, cols: pl.BlockSpec((None, rows, cols), lambda bi, p, pt: (bi, 0, 0))
    grid_spec = pltpu.PrefetchScalarGridSpec(
        num_scalar_prefetch=1,
        grid=(b, nstep + 1),
        in_specs=[page_spec(r) for r in range(PAGES_PER_STEP)] + [
            per_b(QROWS, D_N),
            pl.BlockSpec((None, 2, G_N, n_r, DH_N), lambda bi, p, pt: (bi, 0, 0, 0, 0)),
            per_b(NEW_ROWS, 2 * KVW), pl.BlockSpec((WINDOW, 2 * G_N, DH_N), lambda bi, p, pt: (bi, 0, 0)),
            per_b(NEW_ROWS, 2 * KVW), per_b(QROWS, LANES)],
        out_specs=per_b(QROWS, D_N),
        scratch_shapes=[pltpu.VMEM((G_N, QROWS, 2 * LANES), jnp.float32),
                        pltpu.VMEM((G_N, nq, DH_N), jnp.float32),
                        pltpu.VMEM((G_N, nq, 1), jnp.float32),
                        pltpu.VMEM((G_N, nq, 1), jnp.float32),
                        pltpu.VMEM((G_N, nq, DH_N), jnp.float32)])
    return pl.pallas_call(
        functools.partial(_paged_attend_kernel, page=page, nstep=nstep, l_true=l_true),
        grid_spec=grid_spec,
        out_shape=jax.ShapeDtypeStruct((b, QROWS, D_N), jnp.float32),
        compiler_params=pltpu.CompilerParams(dimension_semantics=("parallel", "arbitrary"),
                                             vmem_limit_bytes=VMEM_LIMIT_BYTES),
        name="nsa_paged_attend",
    )(page_table, *([cache3] * PAGES_PER_STEP), q8, kcv, new_s, winp, wnew, zg8)


def _extract_top(s, k):
    rows = lax.broadcasted_iota(jnp.int32, s.shape, 0)
    nrow = s.shape[0]
    work = s
    taken = jnp.zeros(s.shape, jnp.bool_)
    tops = []
    for _ in range(k):
        m = jnp.max(work, axis=0, keepdims=True)
        first = jnp.min(jnp.where(work == m, rows, nrow), axis=0, keepdims=True)
        hit = rows == first
        taken = jnp.logical_or(taken, hit)
        work = jnp.where(hit, NEG_INF, work)
        tops.append(m)
    return tops, taken


def _peer_route_kernel(qt_ref, keys_ref, s1_ref, e1_ref, s2_ref, e2_ref, tau_ref):
    halves = []
    for c in range(2):
        q = qt_ref[c * PEER_QHALF:(c + 1) * PEER_QHALF, :].astype(bf16)
        s = jnp.dot(keys_ref[c].astype(bf16), q, preferred_element_type=jnp.float32)
        tops, taken = _extract_top(s, PEER_TOPK)
        halves.append((s, tops, taken))
    (sa, ta, ma), (sb, tb, mb) = halves
    assert PEER_TOPK == 16
    ta_col = jnp.concatenate(ta, axis=0)
    tb_col = jnp.concatenate(tb, axis=0)
    a_idx = lax.broadcasted_iota(jnp.int32, (8, ta_col.shape[1]), 0)
    cand = jnp.concatenate(
        [ta[0] + tb_col, ta[1] + tb_col, ta[2] + tb_col[0:8], ta[3] + tb_col[0:8]]
        + [jnp.where(a_idx >= 4, ta_col[0:8] + tb[b], NEG_INF) for b in range(3)]
        + [ta_col[8:16] + tb[0]], axis=0)
    ctops, _ = _extract_top(cand, PEER_TOPK)
    cmax = ctops[0]
    z = jnp.exp(ctops[0] - cmax)
    for r in range(1, PEER_TOPK):
        z = z + jnp.exp(ctops[r] - cmax)
    tau_ref[...] = ctops[PEER_TOPK - 1]
    s1_ref[...] = jnp.where(ma, sa, NEG_INF)
    s2_ref[...] = jnp.where(mb, sb, NEG_INF)
    e1_ref[...] = jnp.where(ma, jnp.exp(sa - ta[0]), 0.0) / z
    e2_ref[...] = jnp.where(mb, jnp.exp(sb - tb[0]), 0.0)


def peer_route(qt, pkeys, tb):
    n = qt.shape[1]
    big = jax.ShapeDtypeStruct((PEER_HEADS, N_KEYS, n), jnp.float32)
    bspec = pl.BlockSpec((None, N_KEYS, tb), lambda i, h: (h, 0, i))
    return pl.pallas_call(
        _peer_route_kernel,
        grid=(n // tb, PEER_HEADS),
        in_specs=[pl.BlockSpec((2 * PEER_QHALF, tb), lambda i, h: (h, i)),
                  pl.BlockSpec((None, 2, N_KEYS, PEER_QHALF), lambda i, h: (h, 0, 0, 0))],
        out_specs=[bspec, bspec, bspec, bspec, pl.BlockSpec((None, 1, tb), lambda i, h: (h, 0, i))],
        out_shape=[big, big, big, big, jax.ShapeDtypeStruct((PEER_HEADS, 1, n), jnp.float32)],
        compiler_params=pltpu.CompilerParams(dimension_semantics=("parallel", "parallel")),
        name="peer_route",
    )(qt, pkeys)


def _peer_dense_kernel(xt_ref, u_ref, vt_ref, s1_ref, e1_ref, s2_ref, e2_ref, tau_ref, o_ref, *, sub):
    j = pl.program_id(1)

    @pl.when(j == 0)
    def _():
        o_ref[...] = jnp.zeros_like(o_ref)

    ht = jnp.dot(u_ref[...], xt_ref[...], preferred_element_type=jnp.float32)
    acts = []
    for a in range(sub):
        i1 = j * sub + a
        g = None
        for h in range(PEER_HEADS):
            s1row = s1_ref[h, pl.ds(i1, 1), :]
            e1row = e1_ref[h, pl.ds(i1, 1), :]
            c = s2_ref[h] + s1row
            t = jnp.where(c >= tau_ref[h], e2_ref[h], 0.0) * e1row
            g = t if g is None else g + t
        acts.append((_gelu_tanh(ht[a * N_KEYS:(a + 1) * N_KEYS, :]) * g).astype(bf16))
    act = jnp.concatenate(acts, axis=0) if sub > 1 else acts[0]
    o_ref[...] += jnp.dot(vt_ref[...], act, preferred_element_type=jnp.float32)


def peer_dense(xt, u_bf, vt_bf, s1, e1, s2, e2, tau, tb, sub):
    d, n = xt.shape
    e = u_bf.shape[0]
    te = sub * N_KEYS
    once = dict(pipeline_mode=pl.Buffered(1))
    rspec = pl.BlockSpec((PEER_HEADS, N_KEYS, tb), lambda i, j: (0, 0, i), **once)
    return pl.pallas_call(
        functools.partial(_peer_dense_kernel, sub=sub),
        grid=(n // tb, e // te),
        in_specs=[pl.BlockSpec((d, tb), lambda i, j: (0, i), **once),
                  pl.BlockSpec((te, d), lambda i, j: (j, 0)),
                  pl.BlockSpec((d, te), lambda i, j: (0, j)),
                  rspec, rspec, rspec, rspec,
                  pl.BlockSpec((PEER_HEADS, 1, tb), lambda i, j: (0, 0, i), **once)],
        out_specs=pl.BlockSpec((d, tb), lambda i, j: (0, i)),
        out_shape=jax.ShapeDtypeStruct((d, n), jnp.float32),
        compiler_params=pltpu.CompilerParams(dimension_semantics=("parallel", "arbitrary"),
                                             vmem_limit_bytes=BIG_VMEM_LIMIT_BYTES),
        name="peer_dense",
    )(xt, u_bf, vt_bf, s1, e1, s2, e2, tau)


def peer_ffn_t(xt, w_pq_t, pkeys, u_bf, vt_bf):
    n = xt.shape[1]
    qt = pmm(w_pq_t, xt)
    s1, e1, s2, e2, tau = peer_route(qt, pkeys, _pick(n, (256, 128)))
    return peer_dense(xt, u_bf, vt_bf, s1, e1, s2, e2, tau, _pick(n, (512, 256, 128)), PEER_SUB)


def _in_proj_weights(w_in):
    o = _IN_OFFS
    w_main = jnp.concatenate([w_in[:, o[0]:o[3]], w_in[:, o[5]:o[8]], w_in[:, o[9]:o[11]]], axis=1).astype(bf16)
    w_gate = jnp.concatenate([w_in[:, o[3]:o[5]], w_in[:, o[8]:o[9]],
                              jnp.zeros((w_in.shape[0], LANES - 2 * NH_M - 3 * H_N), w_in.dtype)], axis=1).astype(bf16)
    return w_main, w_gate


def kernel(x_prompt, x_sample, cache_kv, state_win_kv, state_conv, state_C, state_n, state_m, page_table,
           norm_mix, norm_ffn, norm_final, w_in, w_conv, b_conv, w_bd, b_gates, w_hnorm, w_skip,
           cmp_pe, cmp_w1, cmp_w2, w_br_m, w_br_n, w_out, w_pq, peer_keys, peer_u, peer_v):
    assert w_in.shape[0] == DEPTH == 1
    f32 = jnp.float32
    l = 0
    Bp, Lp_, _ = x_prompt.shape
    Bs, Ls, _ = x_sample.shape
    w_main, w_gate = _in_proj_weights(w_in[l])
    wbd_full = expand_blockdiag(w_bd[l])
    gate_bias = jnp.concatenate([b_gates[l, 0], b_gates[l, 1], jnp.zeros((LANES - 2 * NH_M,), f32)])[None]
    w_br_m_bf, w_br_n_bf, w_out_bf = w_br_m[l].astype(bf16), w_br_n[l].astype(bf16), w_out[l].astype(bf16)
    w_pq_t = w_pq[l].T.astype(bf16)
    u_bf = peer_u[l].astype(bf16)
    vt_bf = peer_v[l].T.astype(bf16)

    def mixers(x, conv_prev, C0, n0, m0, nsa_fn, ch, lpad):
        B, L, _ = x.shape
        n = B * L
        x2 = x.reshape(n, D_MODEL)
        xn = rmsnorm_rows(x2, norm_mix[l], bf16)
        z2 = pmm(xn, w_main)
        zg2 = pmm(xn, w_gate)
        z3 = z2.reshape(B, L, Z_COLS)
        zg3 = zg2.reshape(B, L, LANES)
        xm = z3[:, :, Z_XM:Z_XM + D_M]
        new_conv = jnp.concatenate([conv_prev, xm], axis=1)[:, L:]
        prev8 = jnp.pad(conv_prev, ((0, 0), (PREV_ROWS - (CONV_W - 1), 0), (0, 0)))
        if lpad == L:
            zm3, zgm3, cols = z3, zg3, (Z_XM // DH_M, Z_VM // DH_M, Z_OM // DH_M)
        else:
            zm3 = jnp.pad(z3[:, :, :Z_Q], ((0, 0), (0, lpad - L), (0, 0)))
            zgm3 = jnp.pad(zg3, ((0, 0), (0, lpad - L), (0, 0)))
            cols = (Z_XM // DH_M, Z_VM // DH_M, Z_OM // DH_M)
        y_m, C1, n1, m1 = mlstm_branch(zm3, zgm3, cols, prev8, w_conv[l], b_conv[l], wbd_full, gate_bias,
                                       C0, n0[:, :, None, :], jnp.broadcast_to(m0[:, :, None, None], (B, NH_M, 1, LANES)),
                                       w_hnorm[l], w_skip[l], L, ch)
        y_m = y_m[:, :L].reshape(n, D_M)
        kv_new = z3[:, :, Z_KV:Z_KV + N_KV_SETS * KVW].reshape(B, L, N_KV_SETS, G_N, DH_N)
        win_new = z3[:, :, Z_WIN:Z_WIN + 2 * KVW].reshape(B, L, 2, G_N, DH_N)
        y_n, new_win = nsa_fn(z3, zg3, kv_new, win_new)
        merged = merge_branches(y_m, y_n.reshape(n, D_N).astype(bf16), w_br_m_bf, w_br_n_bf, z2)
        h2 = out_proj_residual(merged, w_out_bf, x2)
        return h2, kv_new, new_win, new_conv, C1, n1[:, :, 0], m1[:, :, 0, 0]

    def nsa_p(z3, zg3, kv_new, win_new):
        kcv = compress_prompt(z3, Z_KV // DH_N, cmp_pe[l], cmp_w1[l], cmp_w2[l])
        return nsa_prompt(z3, zg3, kcv), win_new[:, -min(WINDOW, z3.shape[1]):]

    def nsa_s(z3, zg3, kv_new, win_new):
        L = z3.shape[1]
        padr = lambda t, rows: jnp.pad(t, ((0, 0), (0, rows - t.shape[1]), (0, 0)))
        page = cache_kv.shape[2]
        cache3 = cache_kv.reshape(DEPTH * cache_kv.shape[1] * page, N_KV_SETS * G_N, DH_N)
        wbuf = state_win_kv.shape[2]
        win3 = state_win_kv.reshape(Bs * wbuf, 2 * G_N, DH_N)
        kcv = paged_compress(page_table, cache3, page, padr(z3[:, :, Z_KV:Z_KV + 2 * KVW], NEW_ROWS),
                             cmp_pe[l], cmp_w1[l].astype(bf16).reshape(2, CMP_LEN // 2, 2 * DH_N, CMP_HID), cmp_w2[l])
        y8 = paged_attend(page_table, cache3, page, padr(z3[:, :, Z_Q:Z_Q + D_N], QROWS), kcv,
                          padr(z3[:, :, Z_KV + 2 * KVW:Z_KV + 4 * KVW], NEW_ROWS),
                          win3, padr(z3[:, :, Z_WIN:Z_WIN + 2 * KVW], NEW_ROWS), padr(zg3, QROWS), L)
        new_win = jnp.concatenate([state_win_kv.reshape(Bs, wbuf, 2, G_N, DH_N)[:, L:], win_new], axis=1)
        return y8[:, :L], new_win

    hp, kvp, winp, convp, Cp, n_p, m_p = mixers(
        x_prompt, jnp.zeros((Bp, CONV_W - 1, D_M), f32), jnp.zeros((Bp, NH_M, DH_M, DH_M), f32),
        jnp.zeros((Bp, NH_M, DH_M), f32), jnp.zeros((Bp, NH_M), f32), nsa_p, 256, Lp_)
    hs, kvs, wins, convs, Cs, n_s, m_s = mixers(
        x_sample, state_conv.reshape(state_conv.shape[1:]), state_C.reshape(state_C.shape[1:]),
        state_n.reshape(state_n.shape[1:]), state_m.reshape(state_m.shape[1:]), nsa_s, 16, 16)

    def ffn_and_norm(h2):
        xt = rmsnorm_rows_t(h2, norm_ffn[l], bf16)
        peer_t = peer_ffn_t(xt, w_pq_t, peer_keys[l], u_bf, vt_bf)
        return final_norm(h2, peer_t, norm_final)

    y_prompt = ffn_and_norm(hp).reshape(x_prompt.shape)
    y_sample = ffn_and_norm(hs).reshape(x_sample.shape)
    st = lambda t: t[None]
    return (y_prompt, y_sample, st(kvp), st(kvs), st(winp), st(wins), st(convp), st(convs),
            st(Cp), st(Cs), st(n_p), st(n_s), st(m_p), st(m_s))
```
